```python
import math
import jax, jax.numpy as jnp
from jax import lax
import numpy as np

D_MODEL = 1024
BATCH = 4
SEQ = 4096
DEPTH = 4
DEC_BATCH = 8
DEC_SEQ = 2048
PAST_LEN = 128

MLA_HEADS = 8
MLA_NOPE_DIM = 64
MLA_ROPE_DIM = 32
MLA_V_DIM = 64
Q_LORA_RANK = 256
KV_LORA_RANK = 128
ROPE_THETA = 10000.0
DIFF_HEADS = 4
DIFF_HEAD_DIM = 64
MLA_WIDTH = MLA_HEADS * MLA_V_DIM
DIFF_WIDTH = DIFF_HEADS * 2 * DIFF_HEAD_DIM
MIX_WIDTH = MLA_WIDTH + DIFF_WIDTH
IN_SIZES = (Q_LORA_RANK, KV_LORA_RANK, MLA_ROPE_DIM, DIFF_WIDTH, DIFF_WIDTH, DIFF_WIDTH)
IN_WIDTH = Q_LORA_RANK + KV_LORA_RANK + MLA_ROPE_DIM + 3 * DIFF_WIDTH
N_EXPERTS = 16
N_GROUPS = 4
EXPERTS_PER_GROUP = N_EXPERTS // N_GROUPS
TOP_K = 2
D_FF_EXPERT = 512
Q_BLOCK = 128
LN_EPS = 1e-5
RMS_EPS = 1e-6
ALPHA = (2 * DEPTH) ** 0.25
BETA = (8 * DEPTH) ** -0.25

kernel_name = "hybrid_mla_diffattn_grouped_moe_encoder"


def layer_norm(x, g, b):
    xf = x.astype(jnp.float32)
    mu = jnp.mean(xf, axis=-1, keepdims=True)
    var = jnp.mean(jnp.square(xf - mu), axis=-1, keepdims=True)
    return ((xf - mu) * lax.rsqrt(var + LN_EPS) * g + b).astype(x.dtype)


def rms_norm(x, g, eps):
    xf = x.astype(jnp.float32)
    return (xf * lax.rsqrt(jnp.mean(jnp.square(xf), axis=-1, keepdims=True) + eps) * g).astype(x.dtype)


def rope_tables(seq):
    inv = 1.0 / (ROPE_THETA ** (jnp.arange(0, MLA_ROPE_DIM, 2, dtype=jnp.float32) / MLA_ROPE_DIM))
    ang = jnp.arange(seq, dtype=jnp.float32)[:, None] * inv[None, :]
    return jnp.cos(ang), jnp.sin(ang)


def apply_rope(x, cos, sin):
    x1, x2 = jnp.split(x.astype(jnp.float32), 2, axis=-1)
    return jnp.concatenate([x1 * cos - x2 * sin, x2 * cos + x1 * sin], axis=-1).astype(x.dtype)


def to_blocks(x):
    b, s = x.shape[0], x.shape[1]
    x = x.reshape((b, s // Q_BLOCK, Q_BLOCK) + x.shape[2:])
    return jnp.moveaxis(x, 1, 0)


def from_blocks(o):
    o = jnp.moveaxis(o, 0, 1)
    return o.reshape((o.shape[0], -1) + o.shape[3:])


def mla_attention(q_nope, q_rope, k_nope, k_rope, v):
    scale = (MLA_NOPE_DIM + MLA_ROPE_DIM) ** -0.5

    def block(args):
        qn, qr = args
        s = (jnp.einsum('bqhd,bkhd->bhqk', qn, k_nope).astype(jnp.float32)
             + jnp.einsum('bqhr,bkr->bhqk', qr, k_rope).astype(jnp.float32))
        p = jax.nn.softmax(s * scale, axis=-1).astype(v.dtype)
        return jnp.einsum('bhqk,bkhd->bqhd', p, v)

    return from_blocks(lax.map(block, (to_blocks(q_nope), to_blocks(q_rope))))


def diff_attention(q1, q2, k1, k2, v, lam):
    s_len = k1.shape[1]
    scale = DIFF_HEAD_DIM ** -0.5
    slopes = 2.0 ** (-8.0 * jnp.arange(1, DIFF_HEADS + 1, dtype=jnp.float32) / DIFF_HEADS)
    kpos = jnp.arange(s_len, dtype=jnp.float32)
    starts = jnp.arange(s_len // Q_BLOCK, dtype=jnp.float32) * Q_BLOCK

    def block(args):
        q1b, q2b, start = args
        qpos = start + jnp.arange(Q_BLOCK, dtype=jnp.float32)
        bias = -slopes[:, None, None] * jnp.abs(qpos[:, None] - kpos[None, :])[None]
        p1 = jax.nn.softmax(jnp.einsum('bqhd,bkhd->bhqk', q1b, k1).astype(jnp.float32) * scale + bias, axis=-1)
        p2 = jax.nn.softmax(jnp.einsum('bqhd,bkhd->bhqk', q2b, k2).astype(jnp.float32) * scale + bias, axis=-1)
        p = (p1 - lam * p2).astype(v.dtype)
        return jnp.einsum('bhqk,bkhd->bqhd', p, v)

    return from_blocks(lax.map(block, (to_blocks(q1), to_blocks(q2), starts)))


def token_mixer(x, layer_idx, w_in, g_q, g_kv, w_uq, w_ukv, lam_q1, lam_k1, lam_q2, lam_k2, g_sub, w_o):
    b, s, _ = x.shape
    h = x @ w_in
    offs = np.cumsum(IN_SIZES)[:-1].tolist()
    c_q, c_kv, k_rope, dq, dk, dv = jnp.split(h, offs, axis=-1)
    q = (rms_norm(c_q, g_q, RMS_EPS) @ w_uq).reshape(b, s, MLA_HEADS, MLA_NOPE_DIM + MLA_ROPE_DIM)
    q_nope, q_rope = q[..., :MLA_NOPE_DIM], q[..., MLA_NOPE_DIM:]
    kv = (rms_norm(c_kv, g_kv, RMS_EPS) @ w_ukv).reshape(b, s, MLA_HEADS, MLA_NOPE_DIM + MLA_V_DIM)
    k_nope, v_mla = kv[..., :MLA_NOPE_DIM], kv[..., MLA_NOPE_DIM:]
    cos, sin = rope_tables(s)
    q_rope = apply_rope(q_rope, cos[None, :, None, :], sin[None, :, None, :])
    k_rope = apply_rope(k_rope, cos[None], sin[None])
    o_mla = mla_attention(q_nope, q_rope, k_nope, k_rope, v_mla).reshape(b, s, MLA_WIDTH)
    dq = dq.reshape(b, s, DIFF_HEADS, 2, DIFF_HEAD_DIM)
    dk = dk.reshape(b, s, DIFF_HEADS, 2, DIFF_HEAD_DIM)
    dv = dv.reshape(b, s, DIFF_HEADS, 2 * DIFF_HEAD_DIM)
    lam_init = 0.8 - 0.6 * math.exp(-0.3 * layer_idx)
    lam = (jnp.exp(jnp.sum(lam_q1.astype(jnp.float32) * lam_k1.astype(jnp.float32)))
           - jnp.exp(jnp.sum(lam_q2.astype(jnp.float32) * lam_k2.astype(jnp.float32))) + lam_init)
    o_d = diff_attention(dq[..., 0, :], dq[..., 1, :], dk[..., 0, :], dk[..., 1, :], dv, lam)
    o_d = (rms_norm(o_d, g_sub, LN_EPS) * (1.0 - lam_init)).reshape(b, s, DIFF_WIDTH)
    return jnp.concatenate([o_mla, o_d], axis=-1) @ w_o


def grouped_moe(x, w_router, b_router, w_gate, w_up, w_down):
    b, s, d = x.shape
    xt = x.reshape(-1, d)
    scores = jax.nn.sigmoid((xt @ w_router).astype(jnp.float32))
    biased = scores + b_router.astype(jnp.float32)
    grp = biased.reshape(-1, N_GROUPS, EXPERTS_PER_GROUP)
    grp_score = jnp.sum(lax.top_k(grp, TOP_K)[0], axis=-1)
    g_idx = jnp.argmax(grp_score, axis=-1)
    in_group = (jnp.arange(N_EXPERTS) // EXPERTS_PER_GROUP)[None, :] == g_idx[:, None]
    _, e_idx = lax.top_k(jnp.where(in_group, biased, -jnp.inf), TOP_K)
    w_sel = jnp.take_along_axis(scores, e_idx, axis=-1)
    w_sel = w_sel / jnp.sum(w_sel, axis=-1, keepdims=True)
    gates = jnp.sum(jax.nn.one_hot(e_idx, N_EXPERTS, dtype=jnp.float32) * w_sel[..., None], axis=1)
    gates = gates.astype(x.dtype)
    y = jnp.zeros_like(xt)
    for e in range(N_EXPERTS):
        h = jax.nn.silu(xt @ w_gate[e]) * (xt @ w_up[e])
        y = y + gates[:, e:e + 1] * (h @ w_down[e])
    return y.reshape(b, s, d)


def trunk(x, ln0_g, ln0_b, w_in, g_q, g_kv, w_uq, w_ukv, lam_q1, lam_k1, lam_q2, lam_k2,
          g_sub, w_o, ln1_g, ln1_b, w_router, b_router, w_gate, w_up, w_down, ln2_g, ln2_b):
    x = layer_norm(x, ln0_g, ln0_b)
    for l in range(DEPTH):
        mix = token_mixer(x, l, w_in[l], g_q[l], g_kv[l], w_uq[l], w_ukv[l], lam_q1[l], lam_k1[l],
                          lam_q2[l], lam_k2[l], g_sub[l], w_o[l])
        x = layer_norm(ALPHA * x + mix, ln1_g[l], ln1_b[l])
        ffn = grouped_moe(x, w_router, b_router, w_gate[l], w_up[l], w_down[l])
        x = layer_norm(ALPHA * x + ffn, ln2_g[l], ln2_b[l])
    return x


def setup_inputs(seed: int = 0) -> dict:
    key = jax.random.key(seed)
    ks = jax.random.split(key, 26)

    def nrm(k, shape, scale):
        return jax.random.normal(k, shape, jnp.float32) * scale

    return {
        'x_prompt': nrm(ks[0], (BATCH, SEQ, D_MODEL), 1.0),
        'x_sample': nrm(ks[1], (DEC_BATCH, DEC_SEQ, D_MODEL), 1.0),
        'ln0_g': 1.0 + nrm(ks[2], (D_MODEL,), 0.01),
        'ln0_b': nrm(ks[3], (D_MODEL,), 0.01),
        'w_in': nrm(ks[4], (DEPTH, D_MODEL, IN_WIDTH), D_MODEL ** -0.5),
        'g_q': 1.0 + nrm(ks[5], (DEPTH, Q_LORA_RANK), 0.01),
        'g_kv': 1.0 + nrm(ks[6], (DEPTH, KV_LORA_RANK), 0.01),
        'w_uq': nrm(ks[7], (DEPTH, Q_LORA_RANK, MLA_HEADS * (MLA_NOPE_DIM + MLA_ROPE_DIM)), Q_LORA_RANK ** -0.5),
        'w_ukv': nrm(ks[8], (DEPTH, KV_LORA_RANK, MLA_HEADS * (MLA_NOPE_DIM + MLA_V_DIM)), KV_LORA_RANK ** -0.5),
        'lam_q1': nrm(ks[9], (DEPTH, DIFF_HEAD_DIM), 0.1),
        'lam_k1': nrm(ks[10], (DEPTH, DIFF_HEAD_DIM), 0.1),
        'lam_q2': nrm(ks[11], (DEPTH, DIFF_HEAD_DIM), 0.1),
        'lam_k2': nrm(ks[12], (DEPTH, DIFF_HEAD_DIM), 0.1),
        'g_sub': 1.0 + nrm(ks[13], (DEPTH, 2 * DIFF_HEAD_DIM), 0.01),
        'w_o': nrm(ks[14], (DEPTH, MIX_WIDTH, D_MODEL), BETA * MIX_WIDTH ** -0.5),
        'ln1_g': 1.0 + nrm(ks[15], (DEPTH, D_MODEL), 0.01),
        'ln1_b': nrm(ks[16], (DEPTH, D_MODEL), 0.01),
        'w_router': nrm(ks[17], (D_MODEL, N_EXPERTS), D_MODEL ** -0.5),
        'b_router': nrm(ks[18], (N_EXPERTS,), 0.01),
        'w_gate': nrm(ks[19], (DEPTH, N_EXPERTS, D_MODEL, D_FF_EXPERT), D_MODEL ** -0.5),
        'w_up': nrm(ks[20], (DEPTH, N_EXPERTS, D_MODEL, D_FF_EXPERT), D_MODEL ** -0.5),
        'w_down': nrm(ks[21], (DEPTH, N_EXPERTS, D_FF_EXPERT, D_MODEL), BETA * D_FF_EXPERT ** -0.5),
        'ln2_g': 1.0 + nrm(ks[22], (DEPTH, D_MODEL), 0.01),
        'ln2_b': nrm(ks[23], (DEPTH, D_MODEL), 0.01),
    }


def reference(x_prompt, x_sample, ln0_g, ln0_b, w_in, g_q, g_kv, w_uq, w_ukv, lam_q1, lam_k1,
              lam_q2, lam_k2, g_sub, w_o, ln1_g, ln1_b, w_router, b_router, w_gate, w_up, w_down,
              ln2_g, ln2_b):
    y_prompt = trunk(x_prompt, ln0_g, ln0_b, w_in, g_q, g_kv, w_uq, w_ukv, lam_q1, lam_k1, lam_q2, lam_k2,
                     g_sub, w_o, ln1_g, ln1_b, w_router, b_router, w_gate, w_up, w_down, ln2_g, ln2_b)
    y_sample = trunk(x_sample, ln0_g, ln0_b, w_in, g_q, g_kv, w_uq, w_ukv, lam_q1, lam_k1, lam_q2, lam_k2,
                     g_sub, w_o, ln1_g, ln1_b, w_router, b_router, w_gate, w_up, w_down, ln2_g, ln2_b)
    return (y_prompt, y_sample)
```

```python
import functools
import math

import jax
import jax.numpy as jnp
from jax import lax
from jax.experimental import pallas as pl
from jax.experimental.pallas import tpu as pltpu

F32 = jnp.float32
BF16 = jnp.bfloat16

D_MODEL = 1024
DEPTH = 4
MLA_HEADS = 8
MLA_NOPE_DIM = 64
MLA_ROPE_DIM = 32
MLA_V_DIM = 64
Q_LORA_RANK = 256
KV_LORA_RANK = 128
ROPE_THETA = 10000.0
DIFF_HEADS = 4
DIFF_HEAD_DIM = 64
MLA_WIDTH = MLA_HEADS * MLA_V_DIM
DIFF_WIDTH = DIFF_HEADS * 2 * DIFF_HEAD_DIM
N_EXPERTS = 16
N_GROUPS = 4
EXPERTS_PER_GROUP = N_EXPERTS // N_GROUPS
D_FF_EXPERT = 512
LN_EPS = 1e-5
RMS_EPS = 1e-6
ALPHA = (2 * DEPTH) ** 0.25

LANES = 128
HEAD_W = LANES
MLA_SLAB = MLA_HEADS * HEAD_W
ROPE_HALF = MLA_ROPE_DIM // 2
C_Q = 0
C_KV = C_Q + Q_LORA_RANK
C_KR_A = C_KV + KV_LORA_RANK
C_KR_B = C_KR_A + HEAD_W
C_D = C_KR_B + HEAD_W
IN_W = C_D + 3 * DIFF_WIDTH

VMEM_LIMIT_BYTES = 48 * 1024 * 1024


def _cparams(*sem):
    return pltpu.CompilerParams(dimension_semantics=sem, vmem_limit_bytes=VMEM_LIMIT_BYTES)


def _layer_norm(x, g, b):
    mu = jnp.mean(x, axis=-1, keepdims=True)
    xc = x - mu
    var = jnp.mean(xc * xc, axis=-1, keepdims=True)
    return xc * lax.rsqrt(var + LN_EPS) * g + b


def _rms(x, g, eps):
    return x * lax.rsqrt(jnp.mean(x * x, axis=-1, keepdims=True) + eps) * g


def _ln_kernel(x_ref, g_ref, b_ref, o_ref):
    o_ref[...] = _layer_norm(x_ref[...], g_ref[...], b_ref[...])


def _ln_call(x, g, b, tm):
    t = x.shape[0]
    return pl.pallas_call(
        _ln_kernel,
        grid=(t // tm,),
        in_specs=[pl.BlockSpec((tm, D_MODEL), lambda i: (i, 0)),
                  pl.BlockSpec((1, D_MODEL), lambda i: (0, 0)),
                  pl.BlockSpec((1, D_MODEL), lambda i: (0, 0))],
        out_specs=pl.BlockSpec((tm, D_MODEL), lambda i: (i, 0)),
        out_shape=jax.ShapeDtypeStruct((t, D_MODEL), F32),
        compiler_params=_cparams("parallel"),
        name="ln0",
    )(x, g, b)


def _proj_kernel(x_ref, tab_ref, win_ref, gq_ref, gkv_ref, wqa_ref, wqb_ref, wk_ref, wv_ref,
                 qm_ref, km_ref, vm_ref, d_ref):
    h = jnp.dot(x_ref[...].astype(BF16), win_ref[...], preferred_element_type=F32)
    ct_q = tab_ref[:, 0 * HEAD_W:1 * HEAD_W]
    st_q = tab_ref[:, 1 * HEAD_W:2 * HEAD_W]
    ct_k = tab_ref[:, 2 * HEAD_W:3 * HEAD_W]
    st_k = tab_ref[:, 3 * HEAD_W:4 * HEAD_W]

    cq = _rms(h[:, C_Q:C_KV], gq_ref[...], RMS_EPS).astype(BF16)
    qa = jnp.dot(cq, wqa_ref[...], preferred_element_type=F32)
    qb = jnp.dot(cq, wqb_ref[...], preferred_element_type=F32)
    ckv = _rms(h[:, C_KV:C_KR_A], gkv_ref[...], RMS_EPS).astype(BF16)
    kn = jnp.dot(ckv, wk_ref[...], preferred_element_type=F32)
    vm_ref[...] = jnp.dot(ckv, wv_ref[...], preferred_element_type=F32).astype(BF16)
    kr = h[:, C_KR_A:C_KR_B] * ct_k + h[:, C_KR_B:C_D] * st_k
    for hd in range(MLA_HEADS):
        sl = slice(hd * HEAD_W, (hd + 1) * HEAD_W)
        qm_ref[:, sl] = (qa[:, sl] * ct_q + qb[:, sl] * st_q).astype(BF16)
        km_ref[:, sl] = (kn[:, sl] + kr).astype(BF16)
    d_ref[...] = h[:, C_D:].astype(BF16)


def _proj_call(x, tab, lw, seq, tm):
    t = x.shape[0]
    nblk = seq // tm
    const = lambda i: (0, 0)
    row = lambda i: (i, 0)
    return pl.pallas_call(
        _proj_kernel,
        grid=(t // tm,),
        in_specs=[pl.BlockSpec((tm, D_MODEL), row),
                  pl.BlockSpec((tm, 4 * HEAD_W), lambda i: (i % nblk, 0)),
                  pl.BlockSpec((D_MODEL, IN_W), const),
                  pl.BlockSpec((1, Q_LORA_RANK), const),
                  pl.BlockSpec((1, KV_LORA_RANK), const),
                  pl.BlockSpec((Q_LORA_RANK, MLA_SLAB), const),
                  pl.BlockSpec((Q_LORA_RANK, MLA_SLAB), const),
                  pl.BlockSpec((KV_LORA_RANK, MLA_SLAB), const),
                  pl.BlockSpec((KV_LORA_RANK, MLA_SLAB), const)],
        out_specs=[pl.BlockSpec((tm, MLA_SLAB), row),
                   pl.BlockSpec((tm, MLA_SLAB), row),
                   pl.BlockSpec((tm, MLA_SLAB), row),
                   pl.BlockSpec((tm, 3 * DIFF_WIDTH), row)],
        out_shape=[jax.ShapeDtypeStruct((t, MLA_SLAB), BF16),
                   jax.ShapeDtypeStruct((t, MLA_SLAB), BF16),
                   jax.ShapeDtypeStruct((t, MLA_SLAB), BF16),
                   jax.ShapeDtypeStruct((t, 3 * DIFF_WIDTH), BF16)],
        compiler_params=_cparams("parallel"),
        name="proj",
    )(x, tab, lw["w_in"], lw["g_q"], lw["g_kv"], lw["wqa"], lw["wqb"], lw["wk"], lw["wv"])


def _mla_kernel(q_ref, k_ref, v_ref, o_ref, *, tk):
    tq = q_ref.shape[0]
    nk = k_ref.shape[0] // tk
    q = q_ref[...]

    def body(j, carry):
        m, l, acc = carry
        off = pl.multiple_of(j * tk, tk)
        s = lax.dot_general(q, k_ref[pl.ds(off, tk), :], (((1,), (1,)), ((), ())),
                            preferred_element_type=F32)
        m_new = jnp.maximum(m, jnp.max(s, axis=-1, keepdims=True))
        a = jnp.exp(m - m_new)
        p = jnp.exp(s - m_new)
        l = a * l + jnp.sum(p, axis=-1, keepdims=True)
        acc = a * acc + jnp.dot(p.astype(BF16), v_ref[pl.ds(off, tk), :], preferred_element_type=F32)
        return m_new, l, acc

    m, l, acc = lax.fori_loop(
        0, nk, body,
        (jnp.full((tq, 1), -jnp.inf, F32), jnp.zeros((tq, 1), F32), jnp.zeros((tq, HEAD_W), F32)))
    o_ref[...] = (acc / l).astype(BF16)


def _mla_call(qm, km, vm, batch, seq, tq, tk):
    qm, km, vm = (a.reshape(batch, seq, MLA_SLAB) for a in (qm, km, vm))
    out = pl.pallas_call(
        functools.partial(_mla_kernel, tk=tk),
        grid=(batch, MLA_HEADS, seq // tq),
        in_specs=[pl.BlockSpec((None, tq, HEAD_W), lambda b, h, i: (b, i, h)),
                  pl.BlockSpec((None, seq, HEAD_W), lambda b, h, i: (b, 0, h)),
                  pl.BlockSpec((None, seq, HEAD_W), lambda b, h, i: (b, 0, h))],
        out_specs=pl.BlockSpec((None, tq, HEAD_W), lambda b, h, i: (b, i, h)),
        out_shape=jax.ShapeDtypeStruct((batch, seq, MLA_SLAB), BF16),
        compiler_params=_cparams("parallel", "parallel", "parallel"),
        name="mla",
    )(qm, km, vm)
    return out.reshape(batch * seq, MLA_SLAB)


def _diff_kernel(lam_ref, gsub_ref, q_ref, k_ref, v_ref, o_ref, *, tk, lam_init):
    tq = q_ref.shape[0]
    nk = k_ref.shape[0] // tk
    hd = pl.program_id(1)
    q0 = pl.program_id(2) * tq
    slope = jnp.exp2(-2.0 * (hd + 1).astype(F32) * jnp.ones((1, 1), F32))
    lane = lax.broadcasted_iota(jnp.int32, (1, HEAD_W), 1)
    q = q_ref[...]
    q1 = jnp.where(lane < DIFF_HEAD_DIM, q, jnp.zeros_like(q))
    q2 = jnp.where(lane >= DIFF_HEAD_DIM, q, jnp.zeros_like(q))
    qpos = (q0 + lax.broadcasted_iota(jnp.int32, (tq, 1), 0)).astype(F32)

    def softmax_step(qx, ks, vs, bias, m, l, acc):
        s = lax.dot_general(qx, ks, (((1,), (1,)), ((), ())), preferred_element_type=F32) + bias
        m_new = jnp.maximum(m, jnp.max(s, axis=-1, keepdims=True))
        a = jnp.exp(m - m_new)
        p = jnp.exp(s - m_new)
        l = a * l + jnp.sum(p, axis=-1, keepdims=True)
        acc = a * acc + jnp.dot(p.astype(BF16), vs, preferred_element_type=F32)
        return m_new, l, acc

    def body(j, carry):
        m1, l1, acc1, m2, l2, acc2 = carry
        off = pl.multiple_of(j * tk, tk)
        ks = k_ref[pl.ds(off, tk), :]
        vs = v_ref[pl.ds(off, tk), :]
        kpos = (off + lax.broadcasted_iota(jnp.int32, (1, tk), 1)).astype(F32)
        bias = -slope * jnp.abs(qpos - kpos)
        m1, l1, acc1 = softmax_step(q1, ks, vs, bias, m1, l1, acc1)
        m2, l2, acc2 = softmax_step(q2, ks, vs, bias, m2, l2, acc2)
        return m1, l1, acc1, m2, l2, acc2

    neg = jnp.full((tq, 1), -jnp.inf, F32)
    zero1 = jnp.zeros((tq, 1), F32)
    zacc = jnp.zeros((tq, HEAD_W), F32)
    _, l1, acc1, _, l2, acc2 = lax.fori_loop(0, nk, body, (neg, zero1, zacc, neg, zero1, zacc))

    lam = (jnp.exp(jnp.sum(lam_ref[0:1, :] * lam_ref[1:2, :], axis=-1, keepdims=True))
           - jnp.exp(jnp.sum(lam_ref[2:3, :] * lam_ref[3:4, :], axis=-1, keepdims=True)) + lam_init)
    o = acc1 / l1 - lam * (acc2 / l2)
    o_ref[...] = (_rms(o, gsub_ref[...], LN_EPS) * (1.0 - lam_init)).astype(BF16)


def _diff_call(dqkv, lamv, g_sub, lam_init, batch, seq, tq, tk):
    d = dqkv.reshape(batch, seq, 3 * DIFF_WIDTH)
    out = pl.pallas_call(
        functools.partial(_diff_kernel, tk=tk, lam_init=lam_init),
        grid=(batch, DIFF_HEADS, seq // tq),
        in_specs=[pl.BlockSpec((8, LANES), lambda b, h, i: (0, 0)),
                  pl.BlockSpec((1, HEAD_W), lambda b, h, i: (0, 0)),
                  pl.BlockSpec((None, tq, HEAD_W), lambda b, h, i: (b, i, h)),
                  pl.BlockSpec((None, seq, HEAD_W), lambda b, h, i: (b, 0, DIFF_HEADS + h)),
                  pl.BlockSpec((None, seq, HEAD_W), lambda b, h, i: (b, 0, 2 * DIFF_HEADS + h))],
        out_specs=pl.BlockSpec((None, tq, HEAD_W), lambda b, h, i: (b, i, h)),
        out_shape=jax.ShapeDtypeStruct((batch, seq, DIFF_WIDTH), BF16),
        compiler_params=_cparams("parallel", "parallel", "parallel"),
        name="diff",
    )(lamv, g_sub, d, d, d)
    return out.reshape(batch * seq, DIFF_WIDTH)


def _first_max(vals):
    mx = functools.reduce(jnp.maximum, vals)
    taken = None
    masks = []
    for v in vals:
        hit = v == mx
        if taken is None:
            masks.append(hit)
            taken = hit
        else:
            masks.append(jnp.logical_and(hit, jnp.logical_not(taken)))
            taken = jnp.logical_or(taken, hit)
    return masks, mx


def _outproj_kernel(om_ref, od_ref, x_ref, woa_ref, wob_ref, g_ref, b_ref, wrt_ref, br_ref,
                    x1_ref, gates_ref):
    tm = x_ref.shape[0]
    mix = (jnp.dot(om_ref[...], woa_ref[...], preferred_element_type=F32)
           + jnp.dot(od_ref[...], wob_ref[...], preferred_element_type=F32))
    x1 = _layer_norm(ALPHA * x_ref[...] + mix, g_ref[...], b_ref[...])
    x1_ref[...] = x1

    logits = lax.dot_general(wrt_ref[...], x1, (((1,), (1,)), ((), ())),
                             precision=lax.Precision.HIGHEST, preferred_element_type=F32)
    scores = jax.nn.sigmoid(logits)
    biased = scores + br_ref[...]
    member = [biased[k * N_GROUPS:(k + 1) * N_GROUPS, :] for k in range(EXPERTS_PER_GROUP)]
    score_m = [scores[k * N_GROUPS:(k + 1) * N_GROUPS, :] for k in range(EXPERTS_PER_GROUP)]
    first, top1 = _first_max(member)
    rest = [jnp.where(f, -jnp.inf, v) for f, v in zip(first, member)]
    second, top2 = _first_max(rest)
    grp_score = top1 + top2
    grp_rows = [grp_score[g:g + 1, :] for g in range(N_GROUPS)]
    grp_sel, _ = _first_max(grp_rows)
    in_grp = jnp.concatenate([jnp.where(s, 1.0, 0.0) for s in grp_sel], axis=0) > 0.5
    picked = [jnp.logical_and(in_grp, jnp.logical_or(f, s)) for f, s in zip(first, second)]
    w = [jnp.where(p, sc, 0.0) for p, sc in zip(picked, score_m)]
    denom = jnp.sum(functools.reduce(jnp.add, w), axis=0, keepdims=True)
    gates_t = jnp.concatenate(w + [jnp.zeros((LANES - N_EXPERTS, tm), F32)], axis=0) / denom
    gates_ref[...] = gates_t.T


def _outproj_call(om, od, x, lw, wrt, br, tm):
    t = x.shape[0]
    const = lambda i: (0, 0)
    row = lambda i: (i, 0)
    return pl.pallas_call(
        _outproj_kernel,
        grid=(t // tm,),
        in_specs=[pl.BlockSpec((tm, MLA_SLAB), row),
                  pl.BlockSpec((tm, DIFF_WIDTH), row),
                  pl.BlockSpec((tm, D_MODEL), row),
                  pl.BlockSpec((MLA_SLAB, D_MODEL), const),
                  pl.BlockSpec((DIFF_WIDTH, D_MODEL), const),
                  pl.BlockSpec((1, D_MODEL), const),
                  pl.BlockSpec((1, D_MODEL), const),
                  pl.BlockSpec((N_EXPERTS, D_MODEL), const),
                  pl.BlockSpec((N_EXPERTS, 1), const)],
        out_specs=[pl.BlockSpec((tm, D_MODEL), row),
                   pl.BlockSpec((tm, LANES), row)],
        out_shape=[jax.ShapeDtypeStruct((t, D_MODEL), F32),
                   jax.ShapeDtypeStruct((t, LANES), F32)],
        compiler_params=_cparams("parallel"),
        name="outproj",
    )(om, od, x, lw["woa"], lw["wob"], lw["ln1_g"], lw["ln1_b"], wrt, br)


def _moe_kernel(x_ref, gates_ref, wg_ref, wu_ref, wd_ref, g_ref, b_ref, o_ref, xb_ref, acc_ref):
    e = pl.program_id(1)

    @pl.when(e == 0)
    def _():
        xb_ref[...] = x_ref[...].astype(BF16)
        acc_ref[...] = jnp.zeros_like(acc_ref)

    xb = xb_ref[...]
    hg = jnp.dot(xb, wg_ref[...], preferred_element_type=F32)
    hu = jnp.dot(xb, wu_ref[...], preferred_element_type=F32)
    lane_e = (e % EXPERTS_PER_GROUP) * N_GROUPS + e // EXPERTS_PER_GROUP
    lane = lax.broadcasted_iota(jnp.int32, (1, LANES), 1)
    gate = jnp.sum(jnp.where(lane == lane_e, gates_ref[...], 0.0), axis=-1, keepdims=True)
    hh = (hg * jax.nn.sigmoid(hg) * hu * gate).astype(BF16)
    acc_ref[...] += jnp.dot(hh, wd_ref[...], preferred_element_type=F32)

    @pl.when(e == N_EXPERTS - 1)
    def _():
        o_ref[...] = _layer_norm(ALPHA * x_ref[...] + acc_ref[...], g_ref[...], b_ref[...])


def _moe_call(x1, gates, lw, tm):
    t = x1.shape[0]
    return pl.pallas_call(
        _moe_kernel,
        grid=(t // tm, N_EXPERTS),
        in_specs=[pl.BlockSpec((tm, D_MODEL), lambda i, e: (i, 0)),
                  pl.BlockSpec((tm, LANES), lambda i, e: (i, 0)),
                  pl.BlockSpec((None, D_MODEL, D_FF_EXPERT), lambda i, e: (e, 0, 0)),
                  pl.BlockSpec((None, D_MODEL, D_FF_EXPERT), lambda i, e: (e, 0, 0)),
                  pl.BlockSpec((None, D_FF_EXPERT, D_MODEL), lambda i, e: (e, 0, 0)),
                  pl.BlockSpec((1, D_MODEL), lambda i, e: (0, 0)),
                  pl.BlockSpec((1, D_MODEL), lambda i, e: (0, 0))],
        out_specs=pl.BlockSpec((tm, D_MODEL), lambda i, e: (i, 0)),
        out_shape=jax.ShapeDtypeStruct((t, D_MODEL), F32),
        scratch_shapes=[pltpu.VMEM((tm, D_MODEL), BF16), pltpu.VMEM((tm, D_MODEL), F32)],
        compiler_params=_cparams("parallel", "arbitrary"),
        name="moe",
    )(x1, gates, lw["w_gate"], lw["w_up"], lw["w_down"], lw["ln2_g"], lw["ln2_b"])


def _rotate_half_cols(w):
    return jnp.concatenate([-w[..., ROPE_HALF:], w[..., :ROPE_HALF]], axis=-1)


def _prep_layer(l, w_in, g_q, g_kv, w_uq, w_ukv, lam_q1, lam_k1, lam_q2, lam_k2, g_sub, w_o,
                ln1_g, ln1_b, w_gate, w_up, w_down, ln2_g, ln2_b):
    d = D_MODEL
    z = lambda *s: jnp.zeros(s, F32)
    wi = w_in[l]
    c_kr = Q_LORA_RANK + KV_LORA_RANK
    c_dq = c_kr + MLA_ROPE_DIM
    w_kr = wi[:, c_kr:c_dq]
    pad_l, pad_r = z(d, MLA_NOPE_DIM), z(d, HEAD_W - MLA_NOPE_DIM - MLA_ROPE_DIM)
    w_in_wide = jnp.concatenate([
        wi[:, :c_kr],
        pad_l, w_kr, pad_r,
        pad_l, _rotate_half_cols(w_kr), pad_r,
        wi[:, c_dq:c_dq + DIFF_WIDTH] * DIFF_HEAD_DIM ** -0.5,
        wi[:, c_dq + DIFF_WIDTH:]], axis=1).astype(BF16)

    r = Q_LORA_RANK
    wq = w_uq[l].reshape(r, MLA_HEADS, MLA_NOPE_DIM + MLA_ROPE_DIM)
    wq_rope = wq[..., MLA_NOPE_DIM:]
    tail = z(r, MLA_HEADS, HEAD_W - MLA_NOPE_DIM - MLA_ROPE_DIM)
    wqa = jnp.concatenate([wq, tail], axis=-1).reshape(r, MLA_SLAB).astype(BF16)
    wqb = jnp.concatenate([z(r, MLA_HEADS, MLA_NOPE_DIM), _rotate_half_cols(wq_rope), tail],
                          axis=-1).reshape(r, MLA_SLAB).astype(BF16)
    r = KV_LORA_RANK
    wkv = w_ukv[l].reshape(r, MLA_HEADS, MLA_NOPE_DIM + MLA_V_DIM)
    wk = jnp.concatenate([wkv[..., :MLA_NOPE_DIM], z(r, MLA_HEADS, HEAD_W - MLA_NOPE_DIM)],
                         axis=-1).reshape(r, MLA_SLAB).astype(BF16)
    wv = jnp.concatenate([wkv[..., MLA_NOPE_DIM:], z(r, MLA_HEADS, HEAD_W - MLA_V_DIM)],
                         axis=-1).reshape(r, MLA_SLAB).astype(BF16)

    wo_m = w_o[l][:MLA_WIDTH].reshape(MLA_HEADS, MLA_V_DIM, d)
    woa = jnp.concatenate([wo_m, z(MLA_HEADS, HEAD_W - MLA_V_DIM, d)], axis=1).reshape(MLA_SLAB, d).astype(BF16)
    wob = w_o[l][MLA_WIDTH:].astype(BF16)

    lamv = jnp.zeros((8, LANES), F32).at[:4, :DIFF_HEAD_DIM].set(
        jnp.stack([lam_q1[l], lam_k1[l], lam_q2[l], lam_k2[l]]).astype(F32))
    return dict(
        w_in=w_in_wide, g_q=g_q[l][None], g_kv=g_kv[l][None], wqa=wqa, wqb=wqb, wk=wk, wv=wv,
        lamv=lamv, g_sub=g_sub[l][None], woa=woa, wob=wob, ln1_g=ln1_g[l][None], ln1_b=ln1_b[l][None],
        w_gate=w_gate[l].astype(BF16), w_up=w_up[l].astype(BF16), w_down=w_down[l].astype(BF16),
        ln2_g=ln2_g[l][None], ln2_b=ln2_b[l][None])


def _rope_table(seq):
    inv = 1.0 / (ROPE_THETA ** (jnp.arange(0, MLA_ROPE_DIM, 2, dtype=F32) / MLA_ROPE_DIM))
    ang = jnp.arange(seq, dtype=F32)[:, None] * inv[None, :]
    cos, sin = jnp.cos(ang), jnp.sin(ang)
    scale = (MLA_NOPE_DIM + MLA_ROPE_DIM) ** -0.5
    ones = jnp.ones((seq, MLA_NOPE_DIM), F32)
    zl = jnp.zeros((seq, MLA_NOPE_DIM), F32)
    zr = jnp.zeros((seq, HEAD_W - MLA_NOPE_DIM - MLA_ROPE_DIM), F32)
    ct_q = jnp.concatenate([ones, cos, cos, zr], axis=1) * scale
    st_q = jnp.concatenate([zl, sin, sin, zr], axis=1) * scale
    ct_k = jnp.concatenate([zl, cos, cos, zr], axis=1)
    st_k = jnp.concatenate([zl, sin, sin, zr], axis=1)
    return jnp.concatenate([ct_q, st_q, ct_k, st_k], axis=1)


def _pick(n, pref):
    while n % pref:
        pref //= 2
    return pref


def _trunk(x, ln0_g, ln0_b, layers, wrt, br):
    batch, seq, _ = x.shape
    t = batch * seq
    tm_proj = _pick(seq, 256)
    tm_tok = _pick(t, 512)
    tm_moe = _pick(t, 1024)
    tq = _pick(seq, 256)
    tk = _pick(seq, 512)
    tab = _rope_table(seq)
    x = _ln_call(x.reshape(t, D_MODEL), ln0_g[None], ln0_b[None], tm_tok)
    for l, lw in enumerate(layers):
        lam_init = 0.8 - 0.6 * math.exp(-0.3 * l)
        qm, km, vm, dqkv = _proj_call(x, tab, lw, seq, tm_proj)
        om = _mla_call(qm, km, vm, batch, seq, tq, tk)
        od = _diff_call(dqkv, lw["lamv"], lw["g_sub"], lam_init, batch, seq, tq, tk)
        x1, gates = _outproj_call(om, od, x, lw, wrt, br, tm_tok)
        x = _moe_call(x1, gates, lw, tm_moe)
    return x.reshape(batch, seq, D_MODEL)


def kernel(x_prompt, x_sample, ln0_g, ln0_b, w_in, g_q, g_kv, w_uq, w_ukv, lam_q1, lam_k1, lam_q2, lam_k2,
           g_sub, w_o, ln1_g, ln1_b, w_router, b_router, w_gate, w_up, w_down, ln2_g, ln2_b):
    layers = [_prep_layer(l, w_in, g_q, g_kv, w_uq, w_ukv, lam_q1, lam_k1, lam_q2, lam_k2, g_sub, w_o,
                          ln1_g, ln1_b, w_gate, w_up, w_down, ln2_g, ln2_b) for l in range(DEPTH)]
    perm = jnp.arange(N_EXPERTS).reshape(N_GROUPS, EXPERTS_PER_GROUP).T.reshape(-1)
    wrt = w_router.T[perm].astype(F32)
    br = b_router[perm].astype(F32)[:, None]
    y_prompt = _trunk(x_prompt, ln0_g, ln0_b, layers, wrt, br)
    y_sample = _trunk(x_sample, ln0_g, ln0_b, layers, wrt, br)
    return (y_prompt, y_sample)
```

```python
import functools
import math

import jax
import jax.numpy as jnp
from jax import lax
from jax.experimental import pallas as pl
from jax.experimental.pallas import tpu as pltpu

F32 = jnp.float32
BF16 = jnp.bfloat16

D_MODEL = 1024
DEPTH = 4
MLA_HEADS = 8
MLA_NOPE_DIM = 64
MLA_ROPE_DIM = 32
MLA_V_DIM = 64
Q_LORA_RANK = 256
KV_LORA_RANK = 128
ROPE_THETA = 10000.0
DIFF_HEADS = 4
DIFF_HEAD_DIM = 64
MLA_WIDTH = MLA_HEADS * MLA_V_DIM
DIFF_WIDTH = DIFF_HEADS * 2 * DIFF_HEAD_DIM
N_EXPERTS = 16
N_GROUPS = 4
EXPERTS_PER_GROUP = N_EXPERTS // N_GROUPS
D_FF_EXPERT = 512
LN_EPS = 1e-5
RMS_EPS = 1e-6
ALPHA = (2 * DEPTH) ** 0.25

LANES = 128
HEAD_W = LANES
MLA_SLAB = MLA_HEADS * HEAD_W
ROPE_HALF = MLA_ROPE_DIM // 2
C_Q = 0
C_KV = C_Q + Q_LORA_RANK
C_KR_A = C_KV + KV_LORA_RANK
C_KR_B = C_KR_A + HEAD_W
C_D = C_KR_B + HEAD_W
IN_W = C_D + 2 * DIFF_WIDTH
BF16_SUBLANES = 16
MLA_VT_ROWS = MLA_V_DIM + BF16_SUBLANES
DIFF_VT_ROWS = 2 * DIFF_HEAD_DIM + BF16_SUBLANES
LOG2E = math.log2(math.e)

VMEM_LIMIT_BYTES = 48 * 1024 * 1024


def _cparams(*sem):
    return pltpu.CompilerParams(dimension_semantics=sem, vmem_limit_bytes=VMEM_LIMIT_BYTES)


def _layer_norm(x, g, b):
    mu = jnp.mean(x, axis=-1, keepdims=True)
    xc = x - mu
    var = jnp.mean(xc * xc, axis=-1, keepdims=True)
    return xc * lax.rsqrt(var + LN_EPS) * g + b


def _rms(x, g, eps):
    return x * lax.rsqrt(jnp.mean(x * x, axis=-1, keepdims=True) + eps) * g


def _ln_kernel(x_ref, g_ref, b_ref, o_ref):
    o_ref[...] = _layer_norm(x_ref[...], g_ref[...], b_ref[...])


def _ln_call(x, g, b, tm):
    t = x.shape[0]
    return pl.pallas_call(
        _ln_kernel,
        grid=(t // tm,),
        in_specs=[pl.BlockSpec((tm, D_MODEL), lambda i: (i, 0)),
                  pl.BlockSpec((1, D_MODEL), lambda i: (0, 0)),
                  pl.BlockSpec((1, D_MODEL), lambda i: (0, 0))],
        out_specs=pl.BlockSpec((tm, D_MODEL), lambda i: (i, 0)),
        out_shape=jax.ShapeDtypeStruct((t, D_MODEL), F32),
        compiler_params=_cparams("parallel"),
        name="ln0",
    )(x, g, b)


def _ones_rows(n):
    row = lax.broadcasted_iota(jnp.int32, (BF16_SUBLANES, n), 0)
    return jnp.where(row == 0, 1.0, 0.0).astype(BF16)


def _nt_dot(a, b):
    return lax.dot_general(a, b, (((1,), (1,)), ((), ())), preferred_element_type=F32)


def _proj_kernel(x_ref, tab_ref, win_ref, gq_ref, gkv_ref, wqa_ref, wqb_ref, wk_ref, wvt_ref, wdvt_ref,
                 qm_ref, km_ref, vt_ref, d_ref, dvt_ref):
    tm = x_ref.shape[0]
    xb = x_ref[...].astype(BF16)
    h = jnp.dot(xb, win_ref[...], preferred_element_type=F32)
    ct_q = tab_ref[:, 0 * HEAD_W:1 * HEAD_W]
    st_q = tab_ref[:, 1 * HEAD_W:2 * HEAD_W]
    ct_k = tab_ref[:, 2 * HEAD_W:3 * HEAD_W]
    st_k = tab_ref[:, 3 * HEAD_W:4 * HEAD_W]

    cq = _rms(h[:, C_Q:C_KV], gq_ref[...], RMS_EPS).astype(BF16)
    qa = jnp.dot(cq, wqa_ref[...], preferred_element_type=F32)
    qb = jnp.dot(cq, wqb_ref[...], preferred_element_type=F32)
    ckv = _rms(h[:, C_KV:C_KR_A], gkv_ref[...], RMS_EPS).astype(BF16)
    kn = jnp.dot(ckv, wk_ref[...], preferred_element_type=F32)
    kr = h[:, C_KR_A:C_KR_B] * ct_k + h[:, C_KR_B:C_D] * st_k
    ones = _ones_rows(tm)
    vt = _nt_dot(wvt_ref[...], ckv).astype(BF16)
    for hd in range(MLA_HEADS):
        sl = slice(hd * HEAD_W, (hd + 1) * HEAD_W)
        qm_ref[:, sl] = (qa[:, sl] * ct_q + qb[:, sl] * st_q).astype(BF16)
        km_ref[:, sl] = (kn[:, sl] + kr).astype(BF16)
        r0 = hd * MLA_VT_ROWS
        vt_ref[r0:r0 + MLA_V_DIM, :] = vt[hd * MLA_V_DIM:(hd + 1) * MLA_V_DIM, :]
        vt_ref[r0 + MLA_V_DIM:r0 + MLA_VT_ROWS, :] = ones
    d_ref[:, :DIFF_WIDTH] = (h[:, C_D:C_D + DIFF_WIDTH] * (LOG2E * DIFF_HEAD_DIM ** -0.5)).astype(BF16)
    d_ref[:, DIFF_WIDTH:] = h[:, C_D + DIFF_WIDTH:].astype(BF16)
    dvt = _nt_dot(wdvt_ref[...], xb).astype(BF16)
    for hd in range(DIFF_HEADS):
        r0 = hd * DIFF_VT_ROWS
        dvt_ref[r0:r0 + HEAD_W, :] = dvt[hd * HEAD_W:(hd + 1) * HEAD_W, :]
        dvt_ref[r0 + HEAD_W:r0 + DIFF_VT_ROWS, :] = ones


def _proj_call(x, tab, lw, seq, tm):
    t = x.shape[0]
    batch = t // seq
    nblk = seq // tm
    const = lambda i: (0, 0)
    row = lambda i: (i, 0)
    tcol = lambda i: (i // nblk, 0, i % nblk)
    return pl.pallas_call(
        _proj_kernel,
        grid=(t // tm,),
        in_specs=[pl.BlockSpec((tm, D_MODEL), row),
                  pl.BlockSpec((tm, 4 * HEAD_W), lambda i: (i % nblk, 0)),
                  pl.BlockSpec((D_MODEL, IN_W), const),
                  pl.BlockSpec((1, Q_LORA_RANK), const),
                  pl.BlockSpec((1, KV_LORA_RANK), const),
                  pl.BlockSpec((Q_LORA_RANK, MLA_SLAB), const),
                  pl.BlockSpec((Q_LORA_RANK, MLA_SLAB), const),
                  pl.BlockSpec((KV_LORA_RANK, MLA_SLAB), const),
                  pl.BlockSpec((MLA_WIDTH, KV_LORA_RANK), const),
                  pl.BlockSpec((DIFF_WIDTH, D_MODEL), const)],
        out_specs=[pl.BlockSpec((tm, MLA_SLAB), row),
                   pl.BlockSpec((tm, MLA_SLAB), row),
                   pl.BlockSpec((None, MLA_HEADS * MLA_VT_ROWS, tm), tcol),
                   pl.BlockSpec((tm, 2 * DIFF_WIDTH), row),
                   pl.BlockSpec((None, DIFF_HEADS * DIFF_VT_ROWS, tm), tcol)],
        out_shape=[jax.ShapeDtypeStruct((t, MLA_SLAB), BF16),
                   jax.ShapeDtypeStruct((t, MLA_SLAB), BF16),
                   jax.ShapeDtypeStruct((batch, MLA_HEADS * MLA_VT_ROWS, seq), BF16),
                   jax.ShapeDtypeStruct((t, 2 * DIFF_WIDTH), BF16),
                   jax.ShapeDtypeStruct((batch, DIFF_HEADS * DIFF_VT_ROWS, seq), BF16)],
        compiler_params=_cparams("parallel"),
        name="proj",
    )(x, tab, lw["w_in"], lw["g_q"], lw["g_kv"], lw["wqa"], lw["wqb"], lw["wk"], lw["wvt"], lw["wdvt"])


def _flash_loop(score_fn, vt_ref, s_ref, rows):
    _, nstream, tk, tq = s_ref.shape
    nk = vt_ref.shape[1] // tk

    def produce(j, slot):
        mx = []
        for i, st in enumerate(score_fn(pl.multiple_of(j * tk, tk))):
            s_ref[slot, i] = st
            mx.append(jnp.max(st, axis=0, keepdims=True))
        return tuple(mx)

    def consume(j, slot, mx, carry):
        vt = vt_ref[:, pl.ds(pl.multiple_of(j * tk, tk), tk)]
        out = []
        for i, (m, acc) in enumerate(carry):
            m_new = jnp.maximum(m, mx[i])
            p = jnp.exp2(s_ref[slot, i] - m_new).astype(BF16)
            out.append((m_new, jnp.exp2(m - m_new) * acc + jnp.dot(vt, p, preferred_element_type=F32)))
        return tuple(out)

    def pair(jj, state):
        mx_even, carry = state
        j = 2 * jj
        mx_odd = produce(j + 1, 1)
        carry = consume(j, 0, mx_even, carry)
        mx_even = produce(j + 2, 0)
        carry = consume(j + 1, 1, mx_odd, carry)
        return mx_even, carry

    init = tuple((jnp.full((1, tq), -jnp.inf, F32), jnp.zeros((rows, tq), F32)) for _ in range(nstream))
    mx_even, carry = lax.fori_loop(0, nk // 2 - 1, pair, (produce(0, 0), init))
    mx_odd = produce(nk - 1, 1)
    carry = consume(nk - 2, 0, mx_even, carry)
    carry = consume(nk - 1, 1, mx_odd, carry)
    return [acc for _, acc in carry]


def _finish(acc, rows):
    return acc[:rows, :] * (1.0 / acc[rows:rows + 1, :])


def _mla_kernel(q_ref, k_ref, vt_ref, o_ref, s_ref):
    _, nq, tk, tq = s_ref.shape
    qs = [q_ref[i * tq:(i + 1) * tq, :] for i in range(nq)]

    def scores(off):
        ks = k_ref[pl.ds(off, tk), :]
        return [_nt_dot(ks, q) for q in qs]

    accs = _flash_loop(scores, vt_ref, s_ref, MLA_VT_ROWS)
    pad = jnp.zeros((HEAD_W - MLA_V_DIM, tq), F32)
    for i, acc in enumerate(accs):
        ot = jnp.concatenate([_finish(acc, MLA_V_DIM), pad], axis=0)
        o_ref[i * tq:(i + 1) * tq, :] = ot.T.astype(BF16)


def _mla_call(qm, km, vt, batch, seq, tq, tk, nq):
    qm, km = (a.reshape(batch, seq, MLA_SLAB) for a in (qm, km))
    bq = tq * nq
    out = pl.pallas_call(
        _mla_kernel,
        grid=(batch, MLA_HEADS, seq // bq),
        scratch_shapes=[pltpu.VMEM((2, nq, tk, tq), F32)],
        in_specs=[pl.BlockSpec((None, bq, HEAD_W), lambda b, h, i: (b, i, h)),
                  pl.BlockSpec((None, seq, HEAD_W), lambda b, h, i: (b, 0, h)),
                  pl.BlockSpec((None, MLA_VT_ROWS, seq), lambda b, h, i: (b, h, 0))],
        out_specs=pl.BlockSpec((None, bq, HEAD_W), lambda b, h, i: (b, i, h)),
        out_shape=jax.ShapeDtypeStruct((batch, seq, MLA_SLAB), BF16),
        compiler_params=_cparams("parallel", "parallel", "parallel"),
        name="mla",
    )(qm, km, vt)
    return out.reshape(batch * seq, MLA_SLAB)


def _diff_kernel(lam_ref, gsub_ref, q_ref, k_ref, vt_ref, o_ref, s_ref, *, lam_init):
    _, _, tk, tq = s_ref.shape
    hd = pl.program_id(1)
    q0 = pl.program_id(2) * tq
    slope = LOG2E * jnp.exp2(-2.0 * (hd + 1).astype(F32) * jnp.ones((1, 1), F32))
    lane = lax.broadcasted_iota(jnp.int32, (1, HEAD_W), 1)
    q = q_ref[...]
    q1 = jnp.where(lane < DIFF_HEAD_DIM, q, jnp.zeros_like(q))
    q2 = jnp.where(lane >= DIFF_HEAD_DIM, q, jnp.zeros_like(q))
    rel = (lax.broadcasted_iota(jnp.int32, (tk, tq), 0) - lax.broadcasted_iota(jnp.int32, (tk, tq), 1)).astype(F32)

    def scores(off):
        ks = k_ref[pl.ds(off, tk), :]
        bias = jnp.abs(rel + (off - q0).astype(F32)) * (-slope)
        return [_nt_dot(ks, q1) + bias, _nt_dot(ks, q2) + bias]

    acc1, acc2 = _flash_loop(scores, vt_ref, s_ref, DIFF_VT_ROWS)

    lam = (jnp.exp(jnp.sum(lam_ref[0:1, :] * lam_ref[1:2, :], axis=-1, keepdims=True))
           - jnp.exp(jnp.sum(lam_ref[2:3, :] * lam_ref[3:4, :], axis=-1, keepdims=True)) + lam_init)
    o = (_finish(acc1, HEAD_W) - lam * _finish(acc2, HEAD_W)).T
    o_ref[...] = (_rms(o, gsub_ref[...], LN_EPS) * (1.0 - lam_init)).astype(BF16)


def _diff_call(d, dvt, lamv, g_sub, lam_init, batch, seq, tq, tk):
    d = d.reshape(batch, seq, 2 * DIFF_WIDTH)
    out = pl.pallas_call(
        functools.partial(_diff_kernel, lam_init=lam_init),
        grid=(batch, DIFF_HEADS, seq // tq),
        scratch_shapes=[pltpu.VMEM((2, 2, tk, tq), F32)],
        in_specs=[pl.BlockSpec((8, LANES), lambda b, h, i: (0, 0)),
                  pl.BlockSpec((1, HEAD_W), lambda b, h, i: (0, 0)),
                  pl.BlockSpec((None, tq, HEAD_W), lambda b, h, i: (b, i, h)),
                  pl.BlockSpec((None, seq, HEAD_W), lambda b, h, i: (b, 0, DIFF_HEADS + h)),
                  pl.BlockSpec((None, DIFF_VT_ROWS, seq), lambda b, h, i: (b, h, 0))],
        out_specs=pl.BlockSpec((None, tq, HEAD_W), lambda b, h, i: (b, i, h)),
        out_shape=jax.ShapeDtypeStruct((batch, seq, DIFF_WIDTH), BF16),
        compiler_params=_cparams("parallel", "parallel", "parallel"),
        name="diff",
    )(lamv, g_sub, d, d, dvt)
    return out.reshape(batch * seq, DIFF_WIDTH)


def _first_max(vals):
    mx = functools.reduce(jnp.maximum, vals)
    taken = None
    masks = []
    for v in vals:
        hit = v == mx
        if taken is None:
            masks.append(hit)
            taken = hit
        else:
            masks.append(jnp.logical_and(hit, jnp.logical_not(taken)))
            taken = jnp.logical_or(taken, hit)
    return masks, mx


def _outproj_kernel(om_ref, od_ref, x_ref, woa_ref, wob_ref, g_ref, b_ref, wrt_ref, br_ref,
                    x1_ref, gates_ref):
    tm = x_ref.shape[0]
    mix = (jnp.dot(om_ref[...], woa_ref[...], preferred_element_type=F32)
           + jnp.dot(od_ref[...], wob_ref[...], preferred_element_type=F32))
    x1 = _layer_norm(ALPHA * x_ref[...] + mix, g_ref[...], b_ref[...])
    x1_ref[...] = x1

    logits = lax.dot_general(wrt_ref[...], x1, (((1,), (1,)), ((), ())),
                             precision=lax.Precision.HIGHEST, preferred_element_type=F32)
    scores = jax.nn.sigmoid(logits)
    biased = scores + br_ref[...]
    member = [biased[k * N_GROUPS:(k + 1) * N_GROUPS, :] for k in range(EXPERTS_PER_GROUP)]
    score_m = [scores[k * N_GROUPS:(k + 1) * N_GROUPS, :] for k in range(EXPERTS_PER_GROUP)]
    first, top1 = _first_max(member)
    rest = [jnp.where(f, -jnp.inf, v) for f, v in zip(first, member)]
    second, top2 = _first_max(rest)
    grp_score = top1 + top2
    grp_rows = [grp_score[g:g + 1, :] for g in range(N_GROUPS)]
    grp_sel, _ = _first_max(grp_rows)
    in_grp = jnp.concatenate([jnp.where(s, 1.0, 0.0) for s in grp_sel], axis=0) > 0.5
    picked = [jnp.logical_and(in_grp, jnp.logical_or(f, s)) for f, s in zip(first, second)]
    w = [jnp.where(p, sc, 0.0) for p, sc in zip(picked, score_m)]
    denom = jnp.sum(functools.reduce(jnp.add, w), axis=0, keepdims=True)
    gates_t = jnp.concatenate(w + [jnp.zeros((LANES - N_EXPERTS, tm), F32)], axis=0) / denom
    gates_ref[...] = gates_t.T


def _outproj_call(om, od, x, lw, wrt, br, tm):
    t = x.shape[0]
    const = lambda i: (0, 0)
    row = lambda i: (i, 0)
    return pl.pallas_call(
        _outproj_kernel,
        grid=(t // tm,),
        in_specs=[pl.BlockSpec((tm, MLA_SLAB), row),
                  pl.BlockSpec((tm, DIFF_WIDTH), row),
                  pl.BlockSpec((tm, D_MODEL), row),
                  pl.BlockSpec((MLA_SLAB, D_MODEL), const),
                  pl.BlockSpec((DIFF_WIDTH, D_MODEL), const),
                  pl.BlockSpec((1, D_MODEL), const),
                  pl.BlockSpec((1, D_MODEL), const),
                  pl.BlockSpec((N_EXPERTS, D_MODEL), const),
                  pl.BlockSpec((N_EXPERTS, 1), const)],
        out_specs=[pl.BlockSpec((tm, D_MODEL), row),
                   pl.BlockSpec((tm, LANES), row)],
        out_shape=[jax.ShapeDtypeStruct((t, D_MODEL), F32),
                   jax.ShapeDtypeStruct((t, LANES), F32)],
        compiler_params=_cparams("parallel"),
        name="outproj",
    )(om, od, x, lw["woa"], lw["wob"], lw["ln1_g"], lw["ln1_b"], wrt, br)


def _moe_kernel(x_ref, gates_ref, wg_ref, wu_ref, wd_ref, g_ref, b_ref, o_ref, xb_ref, acc_ref):
    e = pl.program_id(1)

    @pl.when(e == 0)
    def _():
        xb_ref[...] = x_ref[...].astype(BF16)
        acc_ref[...] = jnp.zeros_like(acc_ref)

    xb = xb_ref[...]
    hg = jnp.dot(xb, wg_ref[...], preferred_element_type=F32)
    hu = jnp.dot(xb, wu_ref[...], preferred_element_type=F32)
    lane_e = (e % EXPERTS_PER_GROUP) * N_GROUPS + e // EXPERTS_PER_GROUP
    lane = lax.broadcasted_iota(jnp.int32, (1, LANES), 1)
    gate = jnp.sum(jnp.where(lane == lane_e, gates_ref[...], 0.0), axis=-1, keepdims=True)
    hh = (hg * jax.nn.sigmoid(hg) * hu * gate).astype(BF16)
    acc_ref[...] += jnp.dot(hh, wd_ref[...], preferred_element_type=F32)

    @pl.when(e == N_EXPERTS - 1)
    def _():
        o_ref[...] = _layer_norm(ALPHA * x_ref[...] + acc_ref[...], g_ref[...], b_ref[...])


def _moe_call(x1, gates, lw, tm):
    t = x1.shape[0]
    return pl.pallas_call(
        _moe_kernel,
        grid=(t // tm, N_EXPERTS),
        in_specs=[pl.BlockSpec((tm, D_MODEL), lambda i, e: (i, 0)),
                  pl.BlockSpec((tm, LANES), lambda i, e: (i, 0)),
                  pl.BlockSpec((None, D_MODEL, D_FF_EXPERT), lambda i, e: (e, 0, 0)),
                  pl.BlockSpec((None, D_MODEL, D_FF_EXPERT), lambda i, e: (e, 0, 0)),
                  pl.BlockSpec((None, D_FF_EXPERT, D_MODEL), lambda i, e: (e, 0, 0)),
                  pl.BlockSpec((1, D_MODEL), lambda i, e: (0, 0)),
                  pl.BlockSpec((1, D_MODEL), lambda i, e: (0, 0))],
        out_specs=pl.BlockSpec((tm, D_MODEL), lambda i, e: (i, 0)),
        out_shape=jax.ShapeDtypeStruct((t, D_MODEL), F32),
        scratch_shapes=[pltpu.VMEM((tm, D_MODEL), BF16), pltpu.VMEM((tm, D_MODEL), F32)],
        compiler_params=_cparams("parallel", "arbitrary"),
        name="moe",
    )(x1, gates, lw["w_gate"], lw["w_up"], lw["w_down"], lw["ln2_g"], lw["ln2_b"])


def _rotate_half_cols(w):
    return jnp.concatenate([-w[..., ROPE_HALF:], w[..., :ROPE_HALF]], axis=-1)


def _prep_layer(l, w_in, g_q, g_kv, w_uq, w_ukv, lam_q1, lam_k1, lam_q2, lam_k2, g_sub, w_o,
                ln1_g, ln1_b, w_gate, w_up, w_down, ln2_g, ln2_b):
    d = D_MODEL
    z = lambda *s: jnp.zeros(s, F32)
    wi = w_in[l]
    c_kr = Q_LORA_RANK + KV_LORA_RANK
    c_dq = c_kr + MLA_ROPE_DIM
    w_kr = wi[:, c_kr:c_dq]
    pad_l, pad_r = z(d, MLA_NOPE_DIM), z(d, HEAD_W - MLA_NOPE_DIM - MLA_ROPE_DIM)
    w_in_wide = jnp.concatenate([
        wi[:, :c_kr],
        pad_l, w_kr, pad_r,
        pad_l, _rotate_half_cols(w_kr), pad_r,
        wi[:, c_dq:c_dq + 2 * DIFF_WIDTH]], axis=1).astype(BF16)
    wdvt = wi[:, c_dq + 2 * DIFF_WIDTH:].T.astype(BF16)

    r = Q_LORA_RANK
    wq = w_uq[l].reshape(r, MLA_HEADS, MLA_NOPE_DIM + MLA_ROPE_DIM)
    wq_rope = wq[..., MLA_NOPE_DIM:]
    tail = z(r, MLA_HEADS, HEAD_W - MLA_NOPE_DIM - MLA_ROPE_DIM)
    wqa = jnp.concatenate([wq, tail], axis=-1).reshape(r, MLA_SLAB).astype(BF16)
    wqb = jnp.concatenate([z(r, MLA_HEADS, MLA_NOPE_DIM), _rotate_half_cols(wq_rope), tail],
                          axis=-1).reshape(r, MLA_SLAB).astype(BF16)
    r = KV_LORA_RANK
    wkv = w_ukv[l].reshape(r, MLA_HEADS, MLA_NOPE_DIM + MLA_V_DIM)
    wk = jnp.concatenate([wkv[..., :MLA_NOPE_DIM], z(r, MLA_HEADS, HEAD_W - MLA_NOPE_DIM)],
                         axis=-1).reshape(r, MLA_SLAB).astype(BF16)
    wvt = wkv[..., MLA_NOPE_DIM:].reshape(r, MLA_WIDTH).T.astype(BF16)

    wo_m = w_o[l][:MLA_WIDTH].reshape(MLA_HEADS, MLA_V_DIM, d)
    woa = jnp.concatenate([wo_m, z(MLA_HEADS, HEAD_W - MLA_V_DIM, d)], axis=1).reshape(MLA_SLAB, d).astype(BF16)
    wob = w_o[l][MLA_WIDTH:].astype(BF16)

    lamv = jnp.zeros((8, LANES), F32).at[:4, :DIFF_HEAD_DIM].set(
        jnp.stack([lam_q1[l], lam_k1[l], lam_q2[l], lam_k2[l]]).astype(F32))
    return dict(
        w_in=w_in_wide, g_q=g_q[l][None], g_kv=g_kv[l][None], wqa=wqa, wqb=wqb, wk=wk, wvt=wvt, wdvt=wdvt,
        lamv=lamv, g_sub=g_sub[l][None], woa=woa, wob=wob, ln1_g=ln1_g[l][None], ln1_b=ln1_b[l][None],
        w_gate=w_gate[l].astype(BF16), w_up=w_up[l].astype(BF16), w_down=w_down[l].astype(BF16),
        ln2_g=ln2_g[l][None], ln2_b=ln2_b[l][None])


def _rope_table(seq):
    inv = 1.0 / (ROPE_THETA ** (jnp.arange(0, MLA_ROPE_DIM, 2, dtype=F32) / MLA_ROPE_DIM))
    ang = jnp.arange(seq, dtype=F32)[:, None] * inv[None, :]
    cos, sin = jnp.cos(ang), jnp.sin(ang)
    scale = LOG2E * (MLA_NOPE_DIM + MLA_ROPE_DIM) ** -0.5
    ones = jnp.ones((seq, MLA_NOPE_DIM), F32)
    zl = jnp.zeros((seq, MLA_NOPE_DIM), F32)
    zr = jnp.zeros((seq, HEAD_W - MLA_NOPE_DIM - MLA_ROPE_DIM), F32)
    ct_q = jnp.concatenate([ones, cos, cos, zr], axis=1) * scale
    st_q = jnp.concatenate([zl, sin, sin, zr], axis=1) * scale
    ct_k = jnp.concatenate([zl, cos, cos, zr], axis=1)
    st_k = jnp.concatenate([zl, sin, sin, zr], axis=1)
    return jnp.concatenate([ct_q, st_q, ct_k, st_k], axis=1)


def _pick(n, pref):
    while n % pref:
        pref //= 2
    return pref


def _trunk(x, ln0_g, ln0_b, layers, wrt, br):
    batch, seq, _ = x.shape
    t = batch * seq
    tm_proj = _pick(seq, 256)
    tm_tok = _pick(t, 512)
    tm_moe = _pick(t, 1024)
    tq = _pick(seq, 256)
    tk = _pick(seq // 2, 512)
    nq = 2 if seq % (2 * tq) == 0 else 1
    tab = _rope_table(seq)
    x = _ln_call(x.reshape(t, D_MODEL), ln0_g[None], ln0_b[None], tm_tok)
    for l, lw in enumerate(layers):
        lam_init = 0.8 - 0.6 * math.exp(-0.3 * l)
        qm, km, vt, d, dvt = _proj_call(x, tab, lw, seq, tm_proj)
        om = _mla_call(qm, km, vt, batch, seq, tq, tk, nq)
        od = _diff_call(d, dvt, lw["lamv"], lw["g_sub"], lam_init, batch, seq, tq, tk)
        x1, gates = _outproj_call(om, od, x, lw, wrt, br, tm_tok)
        x = _moe_call(x1, gates, lw, tm_moe)
    return x.reshape(batch, seq, D_MODEL)


def kernel(x_prompt, x_sample, ln0_g, ln0_b, w_in, g_q, g_kv, w_uq, w_ukv, lam_q1, lam_k1, lam_q2, lam_k2,
           g_sub, w_o, ln1_g, ln1_b, w_router, b_router, w_gate, w_up, w_down, ln2_g, ln2_b):
    layers = [_prep_layer(l, w_in, g_q, g_kv, w_uq, w_ukv, lam_q1, lam_k1, lam_q2, lam_k2, g_sub, w_o,
                          ln1_g, ln1_b, w_gate, w_up, w_down, ln2_g, ln2_b) for l in range(DEPTH)]
    perm = jnp.arange(N_EXPERTS).reshape(N_GROUPS, EXPERTS_PER_GROUP).T.reshape(-1)
    wrt = w_router.T[perm].astype(F32)
    br = b_router[perm].astype(F32)[:, None]
    y_prompt = _trunk(x_prompt, ln0_g, ln0_b, layers, wrt, br)
    y_sample = _trunk(x_sample, ln0_g, ln0_b, layers, wrt, br)
    return (y_prompt, y_sample)
```

```python
import functools
import math

import jax
import jax.numpy as jnp
from jax import lax
from jax.experimental import pallas as pl
from jax.experimental.pallas import tpu as pltpu

F32 = jnp.float32
BF16 = jnp.bfloat16

D_MODEL = 1024
DEPTH = 4
MLA_HEADS = 8
MLA_NOPE_DIM = 64
MLA_ROPE_DIM = 32
MLA_V_DIM = 64
Q_LORA_RANK = 256
KV_LORA_RANK = 128
ROPE_THETA = 10000.0
DIFF_HEADS = 4
DIFF_HEAD_DIM = 64
MLA_WIDTH = MLA_HEADS * MLA_V_DIM
DIFF_WIDTH = DIFF_HEADS * 2 * DIFF_HEAD_DIM
N_EXPERTS = 16
N_GROUPS = 4
EXPERTS_PER_GROUP = N_EXPERTS // N_GROUPS
D_FF_EXPERT = 512
LN_EPS = 1e-5
RMS_EPS = 1e-6
ALPHA = (2 * DEPTH) ** 0.25

LANES = 128
HEAD_W = LANES
MLA_SLAB = MLA_HEADS * HEAD_W
ROPE_HALF = MLA_ROPE_DIM // 2
C_Q = 0
C_KV = C_Q + Q_LORA_RANK
C_KR_A = C_KV + KV_LORA_RANK
C_KR_B = C_KR_A + HEAD_W
C_D = C_KR_B + HEAD_W
IN_W = C_D + 2 * DIFF_WIDTH
BF16_SUBLANES = 16
MLA_VT_ROWS = MLA_V_DIM + BF16_SUBLANES
DIFF_VT_ROWS = 2 * DIFF_HEAD_DIM + BF16_SUBLANES
LOG2E = math.log2(math.e)
XG_W = D_MODEL + LANES
ROUTE_GROUP_LANE = N_EXPERTS
ROUTE_RANK_LANE = N_EXPERTS + 1

VMEM_LIMIT_BYTES = 48 * 1024 * 1024


def _cparams(*sem):
    return pltpu.CompilerParams(dimension_semantics=sem, vmem_limit_bytes=VMEM_LIMIT_BYTES)


def _layer_norm(x, g, b):
    mu = jnp.mean(x, axis=-1, keepdims=True)
    xc = x - mu
    var = jnp.mean(xc * xc, axis=-1, keepdims=True)
    return xc * lax.rsqrt(var + LN_EPS) * g + b


def _rms(x, g, eps):
    return x * lax.rsqrt(jnp.mean(x * x, axis=-1, keepdims=True) + eps) * g


def _ln_kernel(x_ref, g_ref, b_ref, o_ref):
    o_ref[...] = _layer_norm(x_ref[...], g_ref[...], b_ref[...])


def _ln_call(x, g, b, tm):
    t = x.shape[0]
    return pl.pallas_call(
        _ln_kernel,
        grid=(t // tm,),
        in_specs=[pl.BlockSpec((tm, D_MODEL), lambda i: (i, 0)),
                  pl.BlockSpec((1, D_MODEL), lambda i: (0, 0)),
                  pl.BlockSpec((1, D_MODEL), lambda i: (0, 0))],
        out_specs=pl.BlockSpec((tm, D_MODEL), lambda i: (i, 0)),
        out_shape=jax.ShapeDtypeStruct((t, D_MODEL), F32),
        compiler_params=_cparams("parallel"),
        name="ln0",
    )(x, g, b)


def _ones_rows(n):
    row = lax.broadcasted_iota(jnp.int32, (BF16_SUBLANES, n), 0)
    return jnp.where(row == 0, 1.0, 0.0).astype(BF16)


def _nt_dot(a, b):
    return lax.dot_general(a, b, (((1,), (1,)), ((), ())), preferred_element_type=F32)


def _proj_kernel(x_ref, tab_ref, win_ref, gq_ref, gkv_ref, wqa_ref, wqb_ref, wk_ref, wvt_ref, wdvt_ref,
                 qm_ref, km_ref, vt_ref, d_ref, dvt_ref):
    tm = x_ref.shape[0]
    xb = x_ref[...].astype(BF16)
    h = jnp.dot(xb, win_ref[...], preferred_element_type=F32)
    ct_q = tab_ref[:, 0 * HEAD_W:1 * HEAD_W]
    st_q = tab_ref[:, 1 * HEAD_W:2 * HEAD_W]
    ct_k = tab_ref[:, 2 * HEAD_W:3 * HEAD_W]
    st_k = tab_ref[:, 3 * HEAD_W:4 * HEAD_W]

    cq = _rms(h[:, C_Q:C_KV], gq_ref[...], RMS_EPS).astype(BF16)
    qa = jnp.dot(cq, wqa_ref[...], preferred_element_type=F32)
    qb = jnp.dot(cq, wqb_ref[...], preferred_element_type=F32)
    ckv = _rms(h[:, C_KV:C_KR_A], gkv_ref[...], RMS_EPS).astype(BF16)
    kn = jnp.dot(ckv, wk_ref[...], preferred_element_type=F32)
    kr = h[:, C_KR_A:C_KR_B] * ct_k + h[:, C_KR_B:C_D] * st_k
    ones = _ones_rows(tm)
    vt = _nt_dot(wvt_ref[...], ckv).astype(BF16)
    for hd in range(MLA_HEADS):
        sl = slice(hd * HEAD_W, (hd + 1) * HEAD_W)
        qm_ref[:, sl] = (qa[:, sl] * ct_q + qb[:, sl] * st_q).astype(BF16)
        km_ref[:, sl] = (kn[:, sl] + kr).astype(BF16)
        r0 = hd * MLA_VT_ROWS
        vt_ref[r0:r0 + MLA_V_DIM, :] = vt[hd * MLA_V_DIM:(hd + 1) * MLA_V_DIM, :]
        vt_ref[r0 + MLA_V_DIM:r0 + MLA_VT_ROWS, :] = ones
    d_ref[:, :DIFF_WIDTH] = (h[:, C_D:C_D + DIFF_WIDTH] * (LOG2E * DIFF_HEAD_DIM ** -0.5)).astype(BF16)
    d_ref[:, DIFF_WIDTH:] = h[:, C_D + DIFF_WIDTH:].astype(BF16)
    dvt = _nt_dot(wdvt_ref[...], xb).astype(BF16)
    for hd in range(DIFF_HEADS):
        r0 = hd * DIFF_VT_ROWS
        dvt_ref[r0:r0 + HEAD_W, :] = dvt[hd * HEAD_W:(hd + 1) * HEAD_W, :]
        dvt_ref[r0 + HEAD_W:r0 + DIFF_VT_ROWS, :] = ones


def _proj_call(x, tab, lw, seq, tm):
    t = x.shape[0]
    batch = t // seq
    nblk = seq // tm
    const = lambda i: (0, 0)
    row = lambda i: (i, 0)
    tcol = lambda i: (i // nblk, 0, i % nblk)
    return pl.pallas_call(
        _proj_kernel,
        grid=(t // tm,),
        in_specs=[pl.BlockSpec((tm, D_MODEL), row),
                  pl.BlockSpec((tm, 4 * HEAD_W), lambda i: (i % nblk, 0)),
                  pl.BlockSpec((D_MODEL, IN_W), const),
                  pl.BlockSpec((1, Q_LORA_RANK), const),
                  pl.BlockSpec((1, KV_LORA_RANK), const),
                  pl.BlockSpec((Q_LORA_RANK, MLA_SLAB), const),
                  pl.BlockSpec((Q_LORA_RANK, MLA_SLAB), const),
                  pl.BlockSpec((KV_LORA_RANK, MLA_SLAB), const),
                  pl.BlockSpec((MLA_WIDTH, KV_LORA_RANK), const),
                  pl.BlockSpec((DIFF_WIDTH, D_MODEL), const)],
        out_specs=[pl.BlockSpec((tm, MLA_SLAB), row),
                   pl.BlockSpec((tm, MLA_SLAB), row),
                   pl.BlockSpec((None, MLA_HEADS * MLA_VT_ROWS, tm), tcol),
                   pl.BlockSpec((tm, 2 * DIFF_WIDTH), row),
                   pl.BlockSpec((None, DIFF_HEADS * DIFF_VT_ROWS, tm), tcol)],
        out_shape=[jax.ShapeDtypeStruct((t, MLA_SLAB), BF16),
                   jax.ShapeDtypeStruct((t, MLA_SLAB), BF16),
                   jax.ShapeDtypeStruct((batch, MLA_HEADS * MLA_VT_ROWS, seq), BF16),
                   jax.ShapeDtypeStruct((t, 2 * DIFF_WIDTH), BF16),
                   jax.ShapeDtypeStruct((batch, DIFF_HEADS * DIFF_VT_ROWS, seq), BF16)],
        compiler_params=_cparams("parallel"),
        name="proj",
    )(x, tab, lw["w_in"], lw["g_q"], lw["g_kv"], lw["wqa"], lw["wqb"], lw["wk"], lw["wvt"], lw["wdvt"])


def _flash_loop(score_fn, vt_ref, s_ref, rows):
    _, nstream, tk, tq = s_ref.shape
    nk = vt_ref.shape[1] // tk

    def produce(j, slot):
        mx = []
        for i, st in enumerate(score_fn(pl.multiple_of(j * tk, tk))):
            s_ref[slot, i] = st
            mx.append(jnp.max(st, axis=0, keepdims=True))
        return tuple(mx)

    def consume(j, slot, mx, carry):
        vt = vt_ref[:, pl.ds(pl.multiple_of(j * tk, tk), tk)]
        out = []
        for i, (m, acc) in enumerate(carry):
            m_new = jnp.maximum(m, mx[i])
            p = jnp.exp2(s_ref[slot, i] - m_new).astype(BF16)
            out.append((m_new, jnp.exp2(m - m_new) * acc + jnp.dot(vt, p, preferred_element_type=F32)))
        return tuple(out)

    def pair(jj, state):
        mx_even, carry = state
        j = 2 * jj
        mx_odd = produce(j + 1, 1)
        carry = consume(j, 0, mx_even, carry)
        mx_even = produce(j + 2, 0)
        carry = consume(j + 1, 1, mx_odd, carry)
        return mx_even, carry

    init = tuple((jnp.full((1, tq), -jnp.inf, F32), jnp.zeros((rows, tq), F32)) for _ in range(nstream))
    mx_even, carry = lax.fori_loop(0, nk // 2 - 1, pair, (produce(0, 0), init))
    mx_odd = produce(nk - 1, 1)
    carry = consume(nk - 2, 0, mx_even, carry)
    carry = consume(nk - 1, 1, mx_odd, carry)
    return [acc for _, acc in carry]


def _finish(acc, rows):
    return acc[:rows, :] * (1.0 / acc[rows:rows + 1, :])


def _mla_kernel(q_ref, k_ref, vt_ref, o_ref, s_ref):
    _, nq, tk, tq = s_ref.shape
    qs = [q_ref[i * tq:(i + 1) * tq, :] for i in range(nq)]

    def scores(off):
        ks = k_ref[pl.ds(off, tk), :]
        return [_nt_dot(ks, q) for q in qs]

    accs = _flash_loop(scores, vt_ref, s_ref, MLA_VT_ROWS)
    pad = jnp.zeros((HEAD_W - MLA_V_DIM, tq), F32)
    for i, acc in enumerate(accs):
        ot = jnp.concatenate([_finish(acc, MLA_V_DIM), pad], axis=0)
        o_ref[i * tq:(i + 1) * tq, :] = ot.T.astype(BF16)


def _mla_call(qm, km, vt, batch, seq, tq, tk, nq):
    qm, km = (a.reshape(batch, seq, MLA_SLAB) for a in (qm, km))
    bq = tq * nq
    out = pl.pallas_call(
        _mla_kernel,
        grid=(batch, MLA_HEADS, seq // bq),
        scratch_shapes=[pltpu.VMEM((2, nq, tk, tq), F32)],
        in_specs=[pl.BlockSpec((None, bq, HEAD_W), lambda b, h, i: (b, i, h)),
                  pl.BlockSpec((None, seq, HEAD_W), lambda b, h, i: (b, 0, h)),
                  pl.BlockSpec((None, MLA_VT_ROWS, seq), lambda b, h, i: (b, h, 0))],
        out_specs=pl.BlockSpec((None, bq, HEAD_W), lambda b, h, i: (b, i, h)),
        out_shape=jax.ShapeDtypeStruct((batch, seq, MLA_SLAB), BF16),
        compiler_params=_cparams("parallel", "parallel", "parallel"),
        name="mla",
    )(qm, km, vt)
    return out.reshape(batch * seq, MLA_SLAB)


def _diff_kernel(lam_ref, gsub_ref, q_ref, k_ref, vt_ref, o_ref, s_ref, *, lam_init):
    _, _, tk, tq = s_ref.shape
    hd = pl.program_id(1)
    q0 = pl.program_id(2) * tq
    slope = LOG2E * jnp.exp2(-2.0 * (hd + 1).astype(F32) * jnp.ones((1, 1), F32))
    lane = lax.broadcasted_iota(jnp.int32, (1, HEAD_W), 1)
    q = q_ref[...]
    q1 = jnp.where(lane < DIFF_HEAD_DIM, q, jnp.zeros_like(q))
    q2 = jnp.where(lane >= DIFF_HEAD_DIM, q, jnp.zeros_like(q))
    rel = (lax.broadcasted_iota(jnp.int32, (tk, tq), 0) - lax.broadcasted_iota(jnp.int32, (tk, tq), 1)).astype(F32)

    def scores(off):
        ks = k_ref[pl.ds(off, tk), :]
        bias = jnp.abs(rel + (off - q0).astype(F32)) * (-slope)
        return [_nt_dot(ks, q1) + bias, _nt_dot(ks, q2) + bias]

    acc1, acc2 = _flash_loop(scores, vt_ref, s_ref, DIFF_VT_ROWS)

    lam = (jnp.exp(jnp.sum(lam_ref[0:1, :] * lam_ref[1:2, :], axis=-1, keepdims=True))
           - jnp.exp(jnp.sum(lam_ref[2:3, :] * lam_ref[3:4, :], axis=-1, keepdims=True)) + lam_init)
    o = (_finish(acc1, HEAD_W) - lam * _finish(acc2, HEAD_W)).T
    o_ref[...] = (_rms(o, gsub_ref[...], LN_EPS) * (1.0 - lam_init)).astype(BF16)


def _diff_call(d, dvt, lamv, g_sub, lam_init, batch, seq, tq, tk):
    d = d.reshape(batch, seq, 2 * DIFF_WIDTH)
    out = pl.pallas_call(
        functools.partial(_diff_kernel, lam_init=lam_init),
        grid=(batch, DIFF_HEADS, seq // tq),
        scratch_shapes=[pltpu.VMEM((2, 2, tk, tq), F32)],
        in_specs=[pl.BlockSpec((8, LANES), lambda b, h, i: (0, 0)),
                  pl.BlockSpec((1, HEAD_W), lambda b, h, i: (0, 0)),
                  pl.BlockSpec((None, tq, HEAD_W), lambda b, h, i: (b, i, h)),
                  pl.BlockSpec((None, seq, HEAD_W), lambda b, h, i: (b, 0, DIFF_HEADS + h)),
                  pl.BlockSpec((None, DIFF_VT_ROWS, seq), lambda b, h, i: (b, h, 0))],
        out_specs=pl.BlockSpec((None, tq, HEAD_W), lambda b, h, i: (b, i, h)),
        out_shape=jax.ShapeDtypeStruct((batch, seq, DIFF_WIDTH), BF16),
        compiler_params=_cparams("parallel", "parallel", "parallel"),
        name="diff",
    )(lamv, g_sub, d, d, dvt)
    return out.reshape(batch * seq, DIFF_WIDTH)


def _first_max(vals):
    mx = functools.reduce(jnp.maximum, vals)
    taken = None
    masks = []
    for v in vals:
        hit = v == mx
        if taken is None:
            masks.append(hit)
            taken = hit
        else:
            masks.append(jnp.logical_and(hit, jnp.logical_not(taken)))
            taken = jnp.logical_or(taken, hit)
    return masks, mx


def _outproj_kernel(om_ref, od_ref, x_ref, woa_ref, wob_ref, g_ref, b_ref, wrt_ref, br_ref,
                    xg_ref, cnt_ref):
    tm = x_ref.shape[0]

    @pl.when(pl.program_id(0) == 0)
    def _():
        cnt_ref[...] = jnp.zeros_like(cnt_ref)

    mix = (jnp.dot(om_ref[...], woa_ref[...], preferred_element_type=F32)
           + jnp.dot(od_ref[...], wob_ref[...], preferred_element_type=F32))
    x1 = _layer_norm(ALPHA * x_ref[...] + mix, g_ref[...], b_ref[...])
    xg_ref[:, :D_MODEL] = x1

    logits = lax.dot_general(wrt_ref[...], x1, (((1,), (1,)), ((), ())),
                             precision=lax.Precision.HIGHEST, preferred_element_type=F32)
    scores = jax.nn.sigmoid(logits)
    biased = scores + br_ref[...]
    member = [biased[k * N_GROUPS:(k + 1) * N_GROUPS, :] for k in range(EXPERTS_PER_GROUP)]
    score_m = [scores[k * N_GROUPS:(k + 1) * N_GROUPS, :] for k in range(EXPERTS_PER_GROUP)]
    first, top1 = _first_max(member)
    rest = [jnp.where(f, -jnp.inf, v) for f, v in zip(first, member)]
    second, top2 = _first_max(rest)
    grp_score = top1 + top2
    grp_rows = [grp_score[g:g + 1, :] for g in range(N_GROUPS)]
    grp_sel, _ = _first_max(grp_rows)
    in_grp_f = jnp.concatenate([jnp.where(s, 1.0, 0.0) for s in grp_sel], axis=0)
    in_grp = in_grp_f > 0.5
    picked = [jnp.logical_and(in_grp, jnp.logical_or(f, s)) for f, s in zip(first, second)]
    w = [jnp.where(p, sc, 0.0) for p, sc in zip(picked, score_m)]
    denom = jnp.sum(functools.reduce(jnp.add, w), axis=0, keepdims=True)
    gates = jnp.concatenate(w, axis=0) / denom

    nrow = cnt_ref.shape[0]
    grp_rows_f = jnp.concatenate([in_grp_f, jnp.zeros((BF16_SUBLANES - N_GROUPS, tm), F32)], axis=0)
    upper = jnp.where(lax.broadcasted_iota(jnp.int32, (tm, tm), 0) <= lax.broadcasted_iota(jnp.int32, (tm, tm), 1),
                      1.0, 0.0).astype(BF16)
    incl = jnp.dot(grp_rows_f.astype(BF16), upper, preferred_element_type=F32)[:nrow]
    onehot = grp_rows_f[:nrow]
    rank = jnp.sum(onehot * (incl - 1.0 + cnt_ref[:, 0:1]), axis=0, keepdims=True)
    row = lax.broadcasted_iota(jnp.int32, (nrow, tm), 0)
    gid = jnp.sum(onehot * row.astype(F32), axis=0, keepdims=True)
    cnt_ref[...] = cnt_ref[...] + incl[:, tm - 1:tm]
    meta = jnp.where(row == 0, gid, jnp.where(row == 1, rank, 0.0))
    slab_t = jnp.concatenate([gates, meta, jnp.zeros((LANES - N_EXPERTS - nrow, tm), F32)], axis=0)
    xg_ref[:, D_MODEL:] = slab_t.T


def _outproj_call(om, od, x, lw, wrt, br, tm):
    t = x.shape[0]
    const = lambda i: (0, 0)
    row = lambda i: (i, 0)
    return pl.pallas_call(
        _outproj_kernel,
        grid=(t // tm,),
        in_specs=[pl.BlockSpec((tm, MLA_SLAB), row),
                  pl.BlockSpec((tm, DIFF_WIDTH), row),
                  pl.BlockSpec((tm, D_MODEL), row),
                  pl.BlockSpec((MLA_SLAB, D_MODEL), const),
                  pl.BlockSpec((DIFF_WIDTH, D_MODEL), const),
                  pl.BlockSpec((1, D_MODEL), const),
                  pl.BlockSpec((1, D_MODEL), const),
                  pl.BlockSpec((N_EXPERTS, D_MODEL), const),
                  pl.BlockSpec((N_EXPERTS, 1), const)],
        out_specs=[pl.BlockSpec((tm, XG_W), row),
                   pl.BlockSpec((8, LANES), const)],
        out_shape=[jax.ShapeDtypeStruct((t, XG_W), F32),
                   jax.ShapeDtypeStruct((8, LANES), F32)],
        compiler_params=_cparams("arbitrary"),
        name="outproj",
    )(om, od, x, lw["woa"], lw["wob"], lw["ln1_g"], lw["ln1_b"], wrt, br)


def _row_dma_all(row_copy, n):
    def start(r, c):
        row_copy(r).start()
        return c

    def wait(r, c):
        row_copy(r).wait()
        return c

    lax.fori_loop(0, n, start, 0)
    lax.fori_loop(0, n, wait, 0)


def _scatter_kernel(pos_ref, xg_ref, init_ref, xs_ref, sem):
    del init_ref
    tm = xg_ref.shape[0]
    base = pl.program_id(0) * tm
    _row_dma_all(lambda r: pltpu.make_async_copy(xg_ref.at[pl.ds(r, 1), :],
                                                 xs_ref.at[pl.ds(pos_ref[base + r], 1), :], sem), tm)


def _scatter_call(pos, xg, rows, tm):
    t = xg.shape[0]
    return pl.pallas_call(
        _scatter_kernel,
        grid_spec=pltpu.PrefetchScalarGridSpec(
            num_scalar_prefetch=1,
            grid=(t // tm,),
            in_specs=[pl.BlockSpec((tm, XG_W), lambda i, pos: (i, 0)),
                      pl.BlockSpec(memory_space=pl.ANY)],
            out_specs=pl.BlockSpec(memory_space=pl.ANY),
            scratch_shapes=[pltpu.SemaphoreType.DMA]),
        out_shape=jax.ShapeDtypeStruct((rows, XG_W), F32),
        input_output_aliases={2: 0},
        compiler_params=_cparams("arbitrary"),
        name="moe_scatter",
    )(pos, xg, jnp.zeros((rows, XG_W), F32))


def _moe_kernel(tg_ref, nused_ref, xs_ref, wg_ref, wu_ref, wd_ref, ys_ref, xb_ref, acc_ref):
    del tg_ref
    k = pl.program_id(1)
    used = pl.program_id(0) < nused_ref[0]

    @pl.when(jnp.logical_and(used, k == 0))
    def _():
        xb_ref[...] = xs_ref[:, :D_MODEL].astype(BF16)
        acc_ref[...] = jnp.zeros_like(acc_ref)

    @pl.when(used)
    def _():
        xb = xb_ref[...]
        hg = jnp.dot(xb, wg_ref[...], preferred_element_type=F32)
        hu = jnp.dot(xb, wu_ref[...], preferred_element_type=F32)
        lane = lax.broadcasted_iota(jnp.int32, (1, LANES), 1)
        mine = jnp.logical_and(lane >= k * N_GROUPS, lane < (k + 1) * N_GROUPS)
        gate = jnp.sum(jnp.where(mine, xs_ref[:, D_MODEL:], 0.0), axis=-1, keepdims=True)
        hh = (hg * jax.nn.sigmoid(hg) * hu * gate).astype(BF16)
        acc_ref[...] += jnp.dot(hh, wd_ref[...], preferred_element_type=F32)

    @pl.when(jnp.logical_and(used, k == EXPERTS_PER_GROUP - 1))
    def _():
        ys_ref[...] = acc_ref[...]

    @pl.when(jnp.logical_and(jnp.logical_not(used), k == EXPERTS_PER_GROUP - 1))
    def _():
        ys_ref[...] = jnp.zeros_like(ys_ref)


def _moe_call(tile_group, nused, xs, lw, tm):
    rows = xs.shape[0]

    def wmap(i, k, tg, nu):
        return (tg[i] * EXPERTS_PER_GROUP + jnp.where(i < nu[0], k, EXPERTS_PER_GROUP - 1), 0, 0)

    return pl.pallas_call(
        _moe_kernel,
        grid_spec=pltpu.PrefetchScalarGridSpec(
            num_scalar_prefetch=2,
            grid=(rows // tm, EXPERTS_PER_GROUP),
            in_specs=[pl.BlockSpec((tm, XG_W), lambda i, k, tg, nu: (i, 0)),
                      pl.BlockSpec((None, D_MODEL, D_FF_EXPERT), wmap),
                      pl.BlockSpec((None, D_MODEL, D_FF_EXPERT), wmap),
                      pl.BlockSpec((None, D_FF_EXPERT, D_MODEL), wmap)],
            out_specs=pl.BlockSpec((tm, D_MODEL), lambda i, k, tg, nu: (i, 0)),
            scratch_shapes=[pltpu.VMEM((tm, D_MODEL), BF16), pltpu.VMEM((tm, D_MODEL), F32)]),
        out_shape=jax.ShapeDtypeStruct((rows, D_MODEL), F32),
        compiler_params=_cparams("parallel", "arbitrary"),
        name="moe",
    )(tile_group, nused, xs, lw["w_gate"], lw["w_up"], lw["w_down"])


def _unsort_ln_kernel(pos_ref, x1_ref, ys_ref, g_ref, b_ref, o_ref, buf_ref, sem):
    tm = x1_ref.shape[0]
    base = pl.program_id(0) * tm
    _row_dma_all(lambda r: pltpu.make_async_copy(ys_ref.at[pl.ds(pos_ref[base + r], 1), :],
                                                 buf_ref.at[pl.ds(r, 1), :], sem), tm)
    o_ref[...] = _layer_norm(ALPHA * x1_ref[...] + buf_ref[...], g_ref[...], b_ref[...])


def _unsort_ln_call(pos, xg, ys, lw, tm):
    t = xg.shape[0]
    return pl.pallas_call(
        _unsort_ln_kernel,
        grid_spec=pltpu.PrefetchScalarGridSpec(
            num_scalar_prefetch=1,
            grid=(t // tm,),
            in_specs=[pl.BlockSpec((tm, D_MODEL), lambda i, pos: (i, 0)),
                      pl.BlockSpec(memory_space=pl.ANY),
                      pl.BlockSpec((1, D_MODEL), lambda i, pos: (0, 0)),
                      pl.BlockSpec((1, D_MODEL), lambda i, pos: (0, 0))],
            out_specs=pl.BlockSpec((tm, D_MODEL), lambda i, pos: (i, 0)),
            scratch_shapes=[pltpu.VMEM((tm, D_MODEL), F32), pltpu.SemaphoreType.DMA]),
        out_shape=jax.ShapeDtypeStruct((t, D_MODEL), F32),
        compiler_params=_cparams("arbitrary"),
        name="moe_unsort_ln",
    )(pos, xg, ys, lw["ln2_g"], lw["ln2_b"])


def _moe_layer(xg, cnt, lw, tm_sorted, tm_tok):
    t = xg.shape[0]
    counts = cnt[:N_GROUPS, 0].astype(jnp.int32)
    padded = (counts + tm_sorted - 1) // tm_sorted * tm_sorted
    ends = jnp.cumsum(padded)
    starts = ends - padded
    grp = xg[:, D_MODEL + ROUTE_GROUP_LANE].astype(jnp.int32)
    rank = xg[:, D_MODEL + ROUTE_RANK_LANE].astype(jnp.int32)
    pos = rank + jnp.sum(jnp.where(grp[:, None] == jnp.arange(N_GROUPS)[None, :], starts[None, :], 0), axis=1)
    ntiles = t // tm_sorted + N_GROUPS
    tile_start = jnp.arange(ntiles, dtype=jnp.int32) * tm_sorted
    tile_group = jnp.minimum(jnp.sum((tile_start[:, None] >= ends[None, :]).astype(jnp.int32), axis=1), N_GROUPS - 1)
    nused = (ends[-1:] // tm_sorted).astype(jnp.int32)
    xs = _scatter_call(pos, xg, ntiles * tm_sorted, tm_tok)
    ys = _moe_call(tile_group, nused, xs, lw, tm_sorted)
    return _unsort_ln_call(pos, xg, ys, lw, tm_tok)


def _rotate_half_cols(w):
    return jnp.concatenate([-w[..., ROPE_HALF:], w[..., :ROPE_HALF]], axis=-1)


def _prep_layer(l, w_in, g_q, g_kv, w_uq, w_ukv, lam_q1, lam_k1, lam_q2, lam_k2, g_sub, w_o,
                ln1_g, ln1_b, w_gate, w_up, w_down, ln2_g, ln2_b):
    d = D_MODEL
    z = lambda *s: jnp.zeros(s, F32)
    wi = w_in[l]
    c_kr = Q_LORA_RANK + KV_LORA_RANK
    c_dq = c_kr + MLA_ROPE_DIM
    w_kr = wi[:, c_kr:c_dq]
    pad_l, pad_r = z(d, MLA_NOPE_DIM), z(d, HEAD_W - MLA_NOPE_DIM - MLA_ROPE_DIM)
    w_in_wide = jnp.concatenate([
        wi[:, :c_kr],
        pad_l, w_kr, pad_r,
        pad_l, _rotate_half_cols(w_kr), pad_r,
        wi[:, c_dq:c_dq + 2 * DIFF_WIDTH]], axis=1).astype(BF16)
    wdvt = wi[:, c_dq + 2 * DIFF_WIDTH:].T.astype(BF16)

    r = Q_LORA_RANK
    wq = w_uq[l].reshape(r, MLA_HEADS, MLA_NOPE_DIM + MLA_ROPE_DIM)
    wq_rope = wq[..., MLA_NOPE_DIM:]
    tail = z(r, MLA_HEADS, HEAD_W - MLA_NOPE_DIM - MLA_ROPE_DIM)
    wqa = jnp.concatenate([wq, tail], axis=-1).reshape(r, MLA_SLAB).astype(BF16)
    wqb = jnp.concatenate([z(r, MLA_HEADS, MLA_NOPE_DIM), _rotate_half_cols(wq_rope), tail],
                          axis=-1).reshape(r, MLA_SLAB).astype(BF16)
    r = KV_LORA_RANK
    wkv = w_ukv[l].reshape(r, MLA_HEADS, MLA_NOPE_DIM + MLA_V_DIM)
    wk = jnp.concatenate([wkv[..., :MLA_NOPE_DIM], z(r, MLA_HEADS, HEAD_W - MLA_NOPE_DIM)],
                         axis=-1).reshape(r, MLA_SLAB).astype(BF16)
    wvt = wkv[..., MLA_NOPE_DIM:].reshape(r, MLA_WIDTH).T.astype(BF16)

    wo_m = w_o[l][:MLA_WIDTH].reshape(MLA_HEADS, MLA_V_DIM, d)
    woa = jnp.concatenate([wo_m, z(MLA_HEADS, HEAD_W - MLA_V_DIM, d)], axis=1).reshape(MLA_SLAB, d).astype(BF16)
    wob = w_o[l][MLA_WIDTH:].astype(BF16)

    lamv = jnp.zeros((8, LANES), F32).at[:4, :DIFF_HEAD_DIM].set(
        jnp.stack([lam_q1[l], lam_k1[l], lam_q2[l], lam_k2[l]]).astype(F32))
    return dict(
        w_in=w_in_wide, g_q=g_q[l][None], g_kv=g_kv[l][None], wqa=wqa, wqb=wqb, wk=wk, wvt=wvt, wdvt=wdvt,
        lamv=lamv, g_sub=g_sub[l][None], woa=woa, wob=wob, ln1_g=ln1_g[l][None], ln1_b=ln1_b[l][None],
        w_gate=w_gate[l].astype(BF16), w_up=w_up[l].astype(BF16), w_down=w_down[l].astype(BF16),
        ln2_g=ln2_g[l][None], ln2_b=ln2_b[l][None])


def _rope_table(seq):
    inv = 1.0 / (ROPE_THETA ** (jnp.arange(0, MLA_ROPE_DIM, 2, dtype=F32) / MLA_ROPE_DIM))
    ang = jnp.arange(seq, dtype=F32)[:, None] * inv[None, :]
    cos, sin = jnp.cos(ang), jnp.sin(ang)
    scale = LOG2E * (MLA_NOPE_DIM + MLA_ROPE_DIM) ** -0.5
    ones = jnp.ones((seq, MLA_NOPE_DIM), F32)
    zl = jnp.zeros((seq, MLA_NOPE_DIM), F32)
    zr = jnp.zeros((seq, HEAD_W - MLA_NOPE_DIM - MLA_ROPE_DIM), F32)
    ct_q = jnp.concatenate([ones, cos, cos, zr], axis=1) * scale
    st_q = jnp.concatenate([zl, sin, sin, zr], axis=1) * scale
    ct_k = jnp.concatenate([zl, cos, cos, zr], axis=1)
    st_k = jnp.concatenate([zl, sin, sin, zr], axis=1)
    return jnp.concatenate([ct_q, st_q, ct_k, st_k], axis=1)


def _pick(n, pref):
    while n % pref:
        pref //= 2
    return pref


def _trunk(x, ln0_g, ln0_b, layers, wrt, br):
    batch, seq, _ = x.shape
    t = batch * seq
    tm_proj = _pick(seq, 256)
    tm_tok = _pick(t, 512)
    tm_moe = _pick(t, 512)
    tq = _pick(seq, 256)
    tk = _pick(seq // 2, 512)
    nq = 2 if seq % (2 * tq) == 0 else 1
    tab = _rope_table(seq)
    x = _ln_call(x.reshape(t, D_MODEL), ln0_g[None], ln0_b[None], tm_tok)
    for l, lw in enumerate(layers):
        lam_init = 0.8 - 0.6 * math.exp(-0.3 * l)
        qm, km, vt, d, dvt = _proj_call(x, tab, lw, seq, tm_proj)
        om = _mla_call(qm, km, vt, batch, seq, tq, tk, nq)
        od = _diff_call(d, dvt, lw["lamv"], lw["g_sub"], lam_init, batch, seq, tq, tk)
        xg, cnt = _outproj_call(om, od, x, lw, wrt, br, tm_tok)
        x = _moe_layer(xg, cnt, lw, tm_moe, tm_tok)
    return x.reshape(batch, seq, D_MODEL)


def kernel(x_prompt, x_sample, ln0_g, ln0_b, w_in, g_q, g_kv, w_uq, w_ukv, lam_q1, lam_k1, lam_q2, lam_k2,
           g_sub, w_o, ln1_g, ln1_b, w_router, b_router, w_gate, w_up, w_down, ln2_g, ln2_b):
    layers = [_prep_layer(l, w_in, g_q, g_kv, w_uq, w_ukv, lam_q1, lam_k1, lam_q2, lam_k2, g_sub, w_o,
                          ln1_g, ln1_b, w_gate, w_up, w_down, ln2_g, ln2_b) for l in range(DEPTH)]
    perm = jnp.arange(N_EXPERTS).reshape(N_GROUPS, EXPERTS_PER_GROUP).T.reshape(-1)
    wrt = w_router.T[perm].astype(F32)
    br = b_router[perm].astype(F32)[:, None]
    y_prompt = _trunk(x_prompt, ln0_g, ln0_b, layers, wrt, br)
    y_sample = _trunk(x_sample, ln0_g, ln0_b, layers, wrt, br)
    return (y_prompt, y_sample)
```

```python
import functools
import math

import jax
import jax.numpy as jnp
from jax import lax
from jax.experimental import pallas as pl
from jax.experimental.pallas import tpu as pltpu

F32 = jnp.float32
BF16 = jnp.bfloat16

D_MODEL = 1024
DEPTH = 4
MLA_HEADS = 8
MLA_NOPE_DIM = 64
MLA_ROPE_DIM = 32
MLA_V_DIM = 64
Q_LORA_RANK = 256
KV_LORA_RANK = 128
ROPE_THETA = 10000.0
DIFF_HEADS = 4
DIFF_HEAD_DIM = 64
MLA_WIDTH = MLA_HEADS * MLA_V_DIM
DIFF_WIDTH = DIFF_HEADS * 2 * DIFF_HEAD_DIM
N_EXPERTS = 16
N_GROUPS = 4
EXPERTS_PER_GROUP = N_EXPERTS // N_GROUPS
D_FF_EXPERT = 512
LN_EPS = 1e-5
RMS_EPS = 1e-6
ALPHA = (2 * DEPTH) ** 0.25

LANES = 128
HEAD_W = LANES
MLA_SLAB = MLA_HEADS * HEAD_W
ROPE_HALF = MLA_ROPE_DIM // 2
C_Q = 0
C_KV = C_Q + Q_LORA_RANK
C_KR_A = C_KV + KV_LORA_RANK
C_KR_B = C_KR_A + HEAD_W
C_D = C_KR_B + HEAD_W
IN_W = C_D + 2 * DIFF_WIDTH
BF16_SUBLANES = 16
MLA_VT_ROWS = MLA_V_DIM + BF16_SUBLANES
DIFF_VT_ROWS = 2 * DIFF_HEAD_DIM + BF16_SUBLANES
LOG2E = math.log2(math.e)
XG_W = D_MODEL + LANES
ROUTE_GROUP_LANE = N_EXPERTS
ROUTE_RANK_LANE = N_EXPERTS + 1

VMEM_LIMIT_BYTES = 48 * 1024 * 1024


def _cparams(*sem):
    return pltpu.CompilerParams(dimension_semantics=sem, vmem_limit_bytes=VMEM_LIMIT_BYTES)


def _layer_norm(x, g, b):
    mu = jnp.mean(x, axis=-1, keepdims=True)
    xc = x - mu
    var = jnp.mean(xc * xc, axis=-1, keepdims=True)
    return xc * lax.rsqrt(var + LN_EPS) * g + b


def _rms(x, g, eps):
    return x * lax.rsqrt(jnp.mean(x * x, axis=-1, keepdims=True) + eps) * g


def _ln_kernel(x_ref, g_ref, b_ref, o_ref):
    o_ref[...] = _layer_norm(x_ref[...], g_ref[...], b_ref[...])


def _ln_call(x, g, b, tm):
    t = x.shape[0]
    return pl.pallas_call(
        _ln_kernel,
        grid=(t // tm,),
        in_specs=[pl.BlockSpec((tm, D_MODEL), lambda i: (i, 0)),
                  pl.BlockSpec((1, D_MODEL), lambda i: (0, 0)),
                  pl.BlockSpec((1, D_MODEL), lambda i: (0, 0))],
        out_specs=pl.BlockSpec((tm, D_MODEL), lambda i: (i, 0)),
        out_shape=jax.ShapeDtypeStruct((t, D_MODEL), F32),
        compiler_params=_cparams("parallel"),
        name="ln0",
    )(x, g, b)


def _ones_rows(n):
    row = lax.broadcasted_iota(jnp.int32, (BF16_SUBLANES, n), 0)
    return jnp.where(row == 0, 1.0, 0.0).astype(BF16)


def _nt_dot(a, b):
    return lax.dot_general(a, b, (((1,), (1,)), ((), ())), preferred_element_type=F32)


def _proj_kernel(x_ref, tab_ref, win_ref, gq_ref, gkv_ref, wqa_ref, wqb_ref, wk_ref, wvt_ref, wdvt_ref,
                 qm_ref, km_ref, vt_ref, d_ref, dvt_ref):
    tm = x_ref.shape[0]
    xb = x_ref[...].astype(BF16)
    h = jnp.dot(xb, win_ref[...], preferred_element_type=F32)
    ct_q = tab_ref[:, 0 * HEAD_W:1 * HEAD_W]
    st_q = tab_ref[:, 1 * HEAD_W:2 * HEAD_W]
    ct_k = tab_ref[:, 2 * HEAD_W:3 * HEAD_W]
    st_k = tab_ref[:, 3 * HEAD_W:4 * HEAD_W]

    cq = _rms(h[:, C_Q:C_KV], gq_ref[...], RMS_EPS).astype(BF16)
    qa = jnp.dot(cq, wqa_ref[...], preferred_element_type=F32)
    qb = jnp.dot(cq, wqb_ref[...], preferred_element_type=F32)
    ckv = _rms(h[:, C_KV:C_KR_A], gkv_ref[...], RMS_EPS).astype(BF16)
    kn = jnp.dot(ckv, wk_ref[...], preferred_element_type=F32)
    kr = h[:, C_KR_A:C_KR_B] * ct_k + h[:, C_KR_B:C_D] * st_k
    ones = _ones_rows(tm)
    vt = _nt_dot(wvt_ref[...], ckv).astype(BF16)
    for hd in range(MLA_HEADS):
        sl = slice(hd * HEAD_W, (hd + 1) * HEAD_W)
        qm_ref[:, sl] = (qa[:, sl] * ct_q + qb[:, sl] * st_q).astype(BF16)
        km_ref[:, sl] = (kn[:, sl] + kr).astype(BF16)
        r0 = hd * MLA_VT_ROWS
        vt_ref[r0:r0 + MLA_V_DIM, :] = vt[hd * MLA_V_DIM:(hd + 1) * MLA_V_DIM, :]
        vt_ref[r0 + MLA_V_DIM:r0 + MLA_VT_ROWS, :] = ones
    d_ref[:, :DIFF_WIDTH] = (h[:, C_D:C_D + DIFF_WIDTH] * (LOG2E * DIFF_HEAD_DIM ** -0.5)).astype(BF16)
    d_ref[:, DIFF_WIDTH:] = h[:, C_D + DIFF_WIDTH:].astype(BF16)
    dvt = _nt_dot(wdvt_ref[...], xb).astype(BF16)
    for hd in range(DIFF_HEADS):
        r0 = hd * DIFF_VT_ROWS
        dvt_ref[r0:r0 + HEAD_W, :] = dvt[hd * HEAD_W:(hd + 1) * HEAD_W, :]
        dvt_ref[r0 + HEAD_W:r0 + DIFF_VT_ROWS, :] = ones


def _proj_call(x, tab, lw, seq, tm):
    t = x.shape[0]
    batch = t // seq
    nblk = seq // tm
    const = lambda i: (0, 0)
    row = lambda i: (i, 0)
    tcol = lambda i: (i // nblk, 0, i % nblk)
    return pl.pallas_call(
        _proj_kernel,
        grid=(t // tm,),
        in_specs=[pl.BlockSpec((tm, D_MODEL), row),
                  pl.BlockSpec((tm, 4 * HEAD_W), lambda i: (i % nblk, 0)),
                  pl.BlockSpec((D_MODEL, IN_W), const),
                  pl.BlockSpec((1, Q_LORA_RANK), const),
                  pl.BlockSpec((1, KV_LORA_RANK), const),
                  pl.BlockSpec((Q_LORA_RANK, MLA_SLAB), const),
                  pl.BlockSpec((Q_LORA_RANK, MLA_SLAB), const),
                  pl.BlockSpec((KV_LORA_RANK, MLA_SLAB), const),
                  pl.BlockSpec((MLA_WIDTH, KV_LORA_RANK), const),
                  pl.BlockSpec((DIFF_WIDTH, D_MODEL), const)],
        out_specs=[pl.BlockSpec((tm, MLA_SLAB), row),
                   pl.BlockSpec((tm, MLA_SLAB), row),
                   pl.BlockSpec((None, MLA_HEADS * MLA_VT_ROWS, tm), tcol),
                   pl.BlockSpec((tm, 2 * DIFF_WIDTH), row),
                   pl.BlockSpec((None, DIFF_HEADS * DIFF_VT_ROWS, tm), tcol)],
        out_shape=[jax.ShapeDtypeStruct((t, MLA_SLAB), BF16),
                   jax.ShapeDtypeStruct((t, MLA_SLAB), BF16),
                   jax.ShapeDtypeStruct((batch, MLA_HEADS * MLA_VT_ROWS, seq), BF16),
                   jax.ShapeDtypeStruct((t, 2 * DIFF_WIDTH), BF16),
                   jax.ShapeDtypeStruct((batch, DIFF_HEADS * DIFF_VT_ROWS, seq), BF16)],
        compiler_params=_cparams("parallel"),
        name="proj",
    )(x, tab, lw["w_in"], lw["g_q"], lw["g_kv"], lw["wqa"], lw["wqb"], lw["wk"], lw["wvt"], lw["wdvt"])


def _flash_loop(score_fn, vt_ref, s_ref, rows):
    _, nstream, tk, tq = s_ref.shape
    nk = vt_ref.shape[1] // tk

    def produce(j, slot):
        mx = []
        for i, st in enumerate(score_fn(pl.multiple_of(j * tk, tk))):
            s_ref[slot, i] = st
            mx.append(jnp.max(st, axis=0, keepdims=True))
        return tuple(mx)

    def consume(j, slot, mx, carry):
        vt = vt_ref[:, pl.ds(pl.multiple_of(j * tk, tk), tk)]
        out = []
        for i, (m, acc) in enumerate(carry):
            m_new = jnp.maximum(m, mx[i])
            p = jnp.exp2(s_ref[slot, i] - m_new).astype(BF16)
            out.append((m_new, jnp.exp2(m - m_new) * acc + jnp.dot(vt, p, preferred_element_type=F32)))
        return tuple(out)

    def pair(jj, state):
        mx_even, carry = state
        j = 2 * jj
        mx_odd = produce(j + 1, 1)
        carry = consume(j, 0, mx_even, carry)
        mx_even = produce(j + 2, 0)
        carry = consume(j + 1, 1, mx_odd, carry)
        return mx_even, carry

    init = tuple((jnp.full((1, tq), -jnp.inf, F32), jnp.zeros((rows, tq), F32)) for _ in range(nstream))
    mx_even, carry = lax.fori_loop(0, nk // 2 - 1, pair, (produce(0, 0), init))
    mx_odd = produce(nk - 1, 1)
    carry = consume(nk - 2, 0, mx_even, carry)
    carry = consume(nk - 1, 1, mx_odd, carry)
    return [acc for _, acc in carry]


def _finish(acc, rows):
    return acc[:rows, :] * (1.0 / acc[rows:rows + 1, :])


def _mla_kernel(q_ref, k_ref, vt_ref, o_ref, s_ref):
    _, nq, tk, tq = s_ref.shape
    qs = [q_ref[i * tq:(i + 1) * tq, :] for i in range(nq)]

    def scores(off):
        ks = k_ref[pl.ds(off, tk), :]
        return [_nt_dot(ks, q) for q in qs]

    accs = _flash_loop(scores, vt_ref, s_ref, MLA_VT_ROWS)
    pad = jnp.zeros((HEAD_W - MLA_V_DIM, tq), F32)
    for i, acc in enumerate(accs):
        ot = jnp.concatenate([_finish(acc, MLA_V_DIM), pad], axis=0)
        o_ref[i * tq:(i + 1) * tq, :] = ot.T.astype(BF16)


def _mla_call(qm, km, vt, batch, seq, tq, tk, nq):
    qm, km = (a.reshape(batch, seq, MLA_SLAB) for a in (qm, km))
    bq = tq * nq
    out = pl.pallas_call(
        _mla_kernel,
        grid=(batch, MLA_HEADS, seq // bq),
        scratch_shapes=[pltpu.VMEM((2, nq, tk, tq), F32)],
        in_specs=[pl.BlockSpec((None, bq, HEAD_W), lambda b, h, i: (b, i, h)),
                  pl.BlockSpec((None, seq, HEAD_W), lambda b, h, i: (b, 0, h)),
                  pl.BlockSpec((None, MLA_VT_ROWS, seq), lambda b, h, i: (b, h, 0))],
        out_specs=pl.BlockSpec((None, bq, HEAD_W), lambda b, h, i: (b, i, h)),
        out_shape=jax.ShapeDtypeStruct((batch, seq, MLA_SLAB), BF16),
        compiler_params=_cparams("parallel", "parallel", "parallel"),
        name="mla",
    )(qm, km, vt)
    return out.reshape(batch * seq, MLA_SLAB)


def _diff_kernel(lam_ref, gsub_ref, q_ref, k_ref, vt_ref, o_ref, s_ref, *, lam_init):
    _, _, tk, tq = s_ref.shape
    hd = pl.program_id(1)
    q0 = pl.program_id(2) * q_ref.shape[0]
    slope = LOG2E * jnp.exp2(-2.0 * (hd + 1).astype(F32) * jnp.ones((1, 1), F32))
    lane = lax.broadcasted_iota(jnp.int32, (1, HEAD_W), 1)
    nq = q_ref.shape[0] // tq
    qs = []
    for i in range(nq):
        q = q_ref[i * tq:(i + 1) * tq, :]
        qs.append((jnp.where(lane < DIFF_HEAD_DIM, q, jnp.zeros_like(q)),
                   jnp.where(lane >= DIFF_HEAD_DIM, q, jnp.zeros_like(q))))
    rel = (lax.broadcasted_iota(jnp.int32, (tk, tq), 0) - lax.broadcasted_iota(jnp.int32, (tk, tq), 1)).astype(F32)

    def scores(off):
        ks = k_ref[pl.ds(off, tk), :]
        out = []
        for i, (q1, q2) in enumerate(qs):
            bias = jnp.abs(rel + (off - q0 - i * tq).astype(F32)) * (-slope)
            out += [_nt_dot(ks, q1) + bias, _nt_dot(ks, q2) + bias]
        return out

    accs = _flash_loop(scores, vt_ref, s_ref, DIFF_VT_ROWS)

    lam = (jnp.exp(jnp.sum(lam_ref[0:1, :] * lam_ref[1:2, :], axis=-1, keepdims=True))
           - jnp.exp(jnp.sum(lam_ref[2:3, :] * lam_ref[3:4, :], axis=-1, keepdims=True)) + lam_init)
    for i in range(nq):
        o = (_finish(accs[2 * i], HEAD_W) - lam * _finish(accs[2 * i + 1], HEAD_W)).T
        o_ref[i * tq:(i + 1) * tq, :] = (_rms(o, gsub_ref[...], LN_EPS) * (1.0 - lam_init)).astype(BF16)


def _diff_call(d, dvt, lamv, g_sub, lam_init, batch, seq, tq, tk, nq):
    d = d.reshape(batch, seq, 2 * DIFF_WIDTH)
    bq = tq * nq
    out = pl.pallas_call(
        functools.partial(_diff_kernel, lam_init=lam_init),
        grid=(batch, DIFF_HEADS, seq // bq),
        scratch_shapes=[pltpu.VMEM((2, 2 * nq, tk, tq), F32)],
        in_specs=[pl.BlockSpec((8, LANES), lambda b, h, i: (0, 0)),
                  pl.BlockSpec((1, HEAD_W), lambda b, h, i: (0, 0)),
                  pl.BlockSpec((None, bq, HEAD_W), lambda b, h, i: (b, i, h)),
                  pl.BlockSpec((None, seq, HEAD_W), lambda b, h, i: (b, 0, DIFF_HEADS + h)),
                  pl.BlockSpec((None, DIFF_VT_ROWS, seq), lambda b, h, i: (b, h, 0))],
        out_specs=pl.BlockSpec((None, bq, HEAD_W), lambda b, h, i: (b, i, h)),
        out_shape=jax.ShapeDtypeStruct((batch, seq, DIFF_WIDTH), BF16),
        compiler_params=_cparams("parallel", "parallel", "parallel"),
        name="diff",
    )(lamv, g_sub, d, d, dvt)
    return out.reshape(batch * seq, DIFF_WIDTH)


def _first_max(vals):
    mx = functools.reduce(jnp.maximum, vals)
    taken = None
    masks = []
    for v in vals:
        hit = v == mx
        if taken is None:
            masks.append(hit)
            taken = hit
        else:
            masks.append(jnp.logical_and(hit, jnp.logical_not(taken)))
            taken = jnp.logical_or(taken, hit)
    return masks, mx


def _outproj_kernel(om_ref, od_ref, x_ref, woa_ref, wob_ref, g_ref, b_ref, wrt_ref, br_ref,
                    xg_ref, cnt_ref):
    tm = x_ref.shape[0]

    @pl.when(pl.program_id(0) == 0)
    def _():
        cnt_ref[...] = jnp.zeros_like(cnt_ref)

    mix = (jnp.dot(om_ref[...], woa_ref[...], preferred_element_type=F32)
           + jnp.dot(od_ref[...], wob_ref[...], preferred_element_type=F32))
    x1 = _layer_norm(ALPHA * x_ref[...] + mix, g_ref[...], b_ref[...])
    xg_ref[:, :D_MODEL] = x1

    logits = lax.dot_general(wrt_ref[...], x1, (((1,), (1,)), ((), ())),
                             precision=lax.Precision.HIGHEST, preferred_element_type=F32)
    scores = jax.nn.sigmoid(logits)
    biased = scores + br_ref[...]
    member = [biased[k * N_GROUPS:(k + 1) * N_GROUPS, :] for k in range(EXPERTS_PER_GROUP)]
    score_m = [scores[k * N_GROUPS:(k + 1) * N_GROUPS, :] for k in range(EXPERTS_PER_GROUP)]
    first, top1 = _first_max(member)
    rest = [jnp.where(f, -jnp.inf, v) for f, v in zip(first, member)]
    second, top2 = _first_max(rest)
    grp_score = top1 + top2
    grp_rows = [grp_score[g:g + 1, :] for g in range(N_GROUPS)]
    grp_sel, _ = _first_max(grp_rows)
    in_grp_f = jnp.concatenate([jnp.where(s, 1.0, 0.0) for s in grp_sel], axis=0)
    in_grp = in_grp_f > 0.5
    picked = [jnp.logical_and(in_grp, jnp.logical_or(f, s)) for f, s in zip(first, second)]
    w = [jnp.where(p, sc, 0.0) for p, sc in zip(picked, score_m)]
    denom = jnp.sum(functools.reduce(jnp.add, w), axis=0, keepdims=True)
    gates = jnp.concatenate(w, axis=0) / denom

    nrow = cnt_ref.shape[0]
    grp_rows_f = jnp.concatenate([in_grp_f, jnp.zeros((BF16_SUBLANES - N_GROUPS, tm), F32)], axis=0)
    upper = jnp.where(lax.broadcasted_iota(jnp.int32, (tm, tm), 0) <= lax.broadcasted_iota(jnp.int32, (tm, tm), 1),
                      1.0, 0.0).astype(BF16)
    incl = jnp.dot(grp_rows_f.astype(BF16), upper, preferred_element_type=F32)[:nrow]
    onehot = grp_rows_f[:nrow]
    rank = jnp.sum(onehot * (incl - 1.0 + cnt_ref[:, 0:1]), axis=0, keepdims=True)
    row = lax.broadcasted_iota(jnp.int32, (nrow, tm), 0)
    gid = jnp.sum(onehot * row.astype(F32), axis=0, keepdims=True)
    cnt_ref[...] = cnt_ref[...] + incl[:, tm - 1:tm]
    meta = jnp.where(row == 0, gid, jnp.where(row == 1, rank, 0.0))
    slab_t = jnp.concatenate([gates, meta, jnp.zeros((LANES - N_EXPERTS - nrow, tm), F32)], axis=0)
    xg_ref[:, D_MODEL:] = slab_t.T


def _outproj_call(om, od, x, lw, wrt, br, tm):
    t = x.shape[0]
    const = lambda i: (0, 0)
    row = lambda i: (i, 0)
    return pl.pallas_call(
        _outproj_kernel,
        grid=(t // tm,),
        in_specs=[pl.BlockSpec((tm, MLA_SLAB), row),
                  pl.BlockSpec((tm, DIFF_WIDTH), row),
                  pl.BlockSpec((tm, D_MODEL), row),
                  pl.BlockSpec((MLA_SLAB, D_MODEL), const),
                  pl.BlockSpec((DIFF_WIDTH, D_MODEL), const),
                  pl.BlockSpec((1, D_MODEL), const),
                  pl.BlockSpec((1, D_MODEL), const),
                  pl.BlockSpec((N_EXPERTS, D_MODEL), const),
                  pl.BlockSpec((N_EXPERTS, 1), const)],
        out_specs=[pl.BlockSpec((tm, XG_W), row),
                   pl.BlockSpec((8, LANES), const)],
        out_shape=[jax.ShapeDtypeStruct((t, XG_W), F32),
                   jax.ShapeDtypeStruct((8, LANES), F32)],
        compiler_params=_cparams("arbitrary"),
        name="outproj",
    )(om, od, x, lw["woa"], lw["wob"], lw["ln1_g"], lw["ln1_b"], wrt, br)


ROW_DMA_UNROLL = 8


def _row_dma_all(row_copy, n):
    def start(r, c):
        row_copy(r).start()
        return c

    def wait(r, c):
        row_copy(r).wait()
        return c

    lax.fori_loop(0, n, start, 0, unroll=ROW_DMA_UNROLL)
    lax.fori_loop(0, n, wait, 0, unroll=ROW_DMA_UNROLL)


def _scatter_kernel(pos_ref, xg_ref, init_ref, xs_ref, sem):
    del init_ref
    tm = xg_ref.shape[0]
    base = pl.program_id(0) * tm
    _row_dma_all(lambda r: pltpu.make_async_copy(xg_ref.at[pl.ds(r, 1), :],
                                                 xs_ref.at[pl.ds(pos_ref[base + r], 1), :], sem), tm)


def _scatter_call(pos, xg, rows, tm):
    t = xg.shape[0]
    return pl.pallas_call(
        _scatter_kernel,
        grid_spec=pltpu.PrefetchScalarGridSpec(
            num_scalar_prefetch=1,
            grid=(t // tm,),
            in_specs=[pl.BlockSpec((tm, XG_W), lambda i, pos: (i, 0)),
                      pl.BlockSpec(memory_space=pl.ANY)],
            out_specs=pl.BlockSpec(memory_space=pl.ANY),
            scratch_shapes=[pltpu.SemaphoreType.DMA]),
        out_shape=jax.ShapeDtypeStruct((rows, XG_W), F32),
        input_output_aliases={2: 0},
        compiler_params=_cparams("arbitrary"),
        name="moe_scatter",
    )(pos, xg, jnp.zeros((rows, XG_W), F32))


def _moe_kernel(tg_ref, nused_ref, xs_ref, wg_ref, wu_ref, wd_ref, ys_ref, xb_ref, acc_ref):
    del tg_ref
    k = pl.program_id(1)
    used = pl.program_id(0) < nused_ref[0]

    @pl.when(jnp.logical_and(used, k == 0))
    def _():
        xb_ref[...] = xs_ref[:, :D_MODEL].astype(BF16)
        acc_ref[...] = jnp.zeros_like(acc_ref)

    @pl.when(used)
    def _():
        xb = xb_ref[...]
        hg = jnp.dot(xb, wg_ref[...], preferred_element_type=F32)
        hu = jnp.dot(xb, wu_ref[...], preferred_element_type=F32)
        lane = lax.broadcasted_iota(jnp.int32, (1, LANES), 1)
        mine = jnp.logical_and(lane >= k * N_GROUPS, lane < (k + 1) * N_GROUPS)
        gate = jnp.sum(jnp.where(mine, xs_ref[:, D_MODEL:], 0.0), axis=-1, keepdims=True)
        hh = (hg * jax.nn.sigmoid(hg) * hu * gate).astype(BF16)
        acc_ref[...] += jnp.dot(hh, wd_ref[...], preferred_element_type=F32)

    @pl.when(jnp.logical_and(used, k == EXPERTS_PER_GROUP - 1))
    def _():
        ys_ref[...] = acc_ref[...]

    @pl.when(jnp.logical_and(jnp.logical_not(used), k == EXPERTS_PER_GROUP - 1))
    def _():
        ys_ref[...] = jnp.zeros_like(ys_ref)


def _moe_call(tile_group, nused, xs, lw, tm):
    rows = xs.shape[0]

    def wmap(i, k, tg, nu):
        return (tg[i] * EXPERTS_PER_GROUP + jnp.where(i < nu[0], k, EXPERTS_PER_GROUP - 1), 0, 0)

    return pl.pallas_call(
        _moe_kernel,
        grid_spec=pltpu.PrefetchScalarGridSpec(
            num_scalar_prefetch=2,
            grid=(rows // tm, EXPERTS_PER_GROUP),
            in_specs=[pl.BlockSpec((tm, XG_W), lambda i, k, tg, nu: (i, 0)),
                      pl.BlockSpec((None, D_MODEL, D_FF_EXPERT), wmap),
                      pl.BlockSpec((None, D_MODEL, D_FF_EXPERT), wmap),
                      pl.BlockSpec((None, D_FF_EXPERT, D_MODEL), wmap)],
            out_specs=pl.BlockSpec((tm, D_MODEL), lambda i, k, tg, nu: (i, 0)),
            scratch_shapes=[pltpu.VMEM((tm, D_MODEL), BF16), pltpu.VMEM((tm, D_MODEL), F32)]),
        out_shape=jax.ShapeDtypeStruct((rows, D_MODEL), F32),
        compiler_params=_cparams("parallel", "arbitrary"),
        name="moe",
    )(tile_group, nused, xs, lw["w_gate"], lw["w_up"], lw["w_down"])


def _unsort_ln_kernel(pos_ref, x1_ref, ys_ref, g_ref, b_ref, o_ref, buf_ref, sem):
    tm = x1_ref.shape[0]
    base = pl.program_id(0) * tm
    _row_dma_all(lambda r: pltpu.make_async_copy(ys_ref.at[pl.ds(pos_ref[base + r], 1), :],
                                                 buf_ref.at[pl.ds(r, 1), :], sem), tm)
    o_ref[...] = _layer_norm(ALPHA * x1_ref[...] + buf_ref[...], g_ref[...], b_ref[...])


def _unsort_ln_call(pos, xg, ys, lw, tm):
    t = xg.shape[0]
    return pl.pallas_call(
        _unsort_ln_kernel,
        grid_spec=pltpu.PrefetchScalarGridSpec(
            num_scalar_prefetch=1,
            grid=(t // tm,),
            in_specs=[pl.BlockSpec((tm, D_MODEL), lambda i, pos: (i, 0)),
                      pl.BlockSpec(memory_space=pl.ANY),
                      pl.BlockSpec((1, D_MODEL), lambda i, pos: (0, 0)),
                      pl.BlockSpec((1, D_MODEL), lambda i, pos: (0, 0))],
            out_specs=pl.BlockSpec((tm, D_MODEL), lambda i, pos: (i, 0)),
            scratch_shapes=[pltpu.VMEM((tm, D_MODEL), F32), pltpu.SemaphoreType.DMA]),
        out_shape=jax.ShapeDtypeStruct((t, D_MODEL), F32),
        compiler_params=_cparams("arbitrary"),
        name="moe_unsort_ln",
    )(pos, xg, ys, lw["ln2_g"], lw["ln2_b"])


def _moe_layer(xg, cnt, lw, tm_sorted, tm_tok):
    t = xg.shape[0]
    counts = cnt[:N_GROUPS, 0].astype(jnp.int32)
    padded = (counts + tm_sorted - 1) // tm_sorted * tm_sorted
    ends = jnp.cumsum(padded)
    starts = ends - padded
    grp = xg[:, D_MODEL + ROUTE_GROUP_LANE].astype(jnp.int32)
    rank = xg[:, D_MODEL + ROUTE_RANK_LANE].astype(jnp.int32)
    pos = rank + jnp.sum(jnp.where(grp[:, None] == jnp.arange(N_GROUPS)[None, :], starts[None, :], 0), axis=1)
    ntiles = t // tm_sorted + N_GROUPS
    tile_start = jnp.arange(ntiles, dtype=jnp.int32) * tm_sorted
    tile_group = jnp.minimum(jnp.sum((tile_start[:, None] >= ends[None, :]).astype(jnp.int32), axis=1), N_GROUPS - 1)
    nused = (ends[-1:] // tm_sorted).astype(jnp.int32)
    xs = _scatter_call(pos, xg, ntiles * tm_sorted, tm_tok)
    ys = _moe_call(tile_group, nused, xs, lw, tm_sorted)
    return _unsort_ln_call(pos, xg, ys, lw, tm_tok)


def _rotate_half_cols(w):
    return jnp.concatenate([-w[..., ROPE_HALF:], w[..., :ROPE_HALF]], axis=-1)


def _prep_layer(l, w_in, g_q, g_kv, w_uq, w_ukv, lam_q1, lam_k1, lam_q2, lam_k2, g_sub, w_o,
                ln1_g, ln1_b, w_gate, w_up, w_down, ln2_g, ln2_b):
    d = D_MODEL
    z = lambda *s: jnp.zeros(s, F32)
    wi = w_in[l]
    c_kr = Q_LORA_RANK + KV_LORA_RANK
    c_dq = c_kr + MLA_ROPE_DIM
    w_kr = wi[:, c_kr:c_dq]
    pad_l, pad_r = z(d, MLA_NOPE_DIM), z(d, HEAD_W - MLA_NOPE_DIM - MLA_ROPE_DIM)
    w_in_wide = jnp.concatenate([
        wi[:, :c_kr],
        pad_l, w_kr, pad_r,
        pad_l, _rotate_half_cols(w_kr), pad_r,
        wi[:, c_dq:c_dq + 2 * DIFF_WIDTH]], axis=1).astype(BF16)
    wdvt = wi[:, c_dq + 2 * DIFF_WIDTH:].T.astype(BF16)

    r = Q_LORA_RANK
    wq = w_uq[l].reshape(r, MLA_HEADS, MLA_NOPE_DIM + MLA_ROPE_DIM)
    wq_rope = wq[..., MLA_NOPE_DIM:]
    tail = z(r, MLA_HEADS, HEAD_W - MLA_NOPE_DIM - MLA_ROPE_DIM)
    wqa = jnp.concatenate([wq, tail], axis=-1).reshape(r, MLA_SLAB).astype(BF16)
    wqb = jnp.concatenate([z(r, MLA_HEADS, MLA_NOPE_DIM), _rotate_half_cols(wq_rope), tail],
                          axis=-1).reshape(r, MLA_SLAB).astype(BF16)
    r = KV_LORA_RANK
    wkv = w_ukv[l].reshape(r, MLA_HEADS, MLA_NOPE_DIM + MLA_V_DIM)
    wk = jnp.concatenate([wkv[..., :MLA_NOPE_DIM], z(r, MLA_HEADS, HEAD_W - MLA_NOPE_DIM)],
                         axis=-1).reshape(r, MLA_SLAB).astype(BF16)
    wvt = wkv[..., MLA_NOPE_DIM:].reshape(r, MLA_WIDTH).T.astype(BF16)

    wo_m = w_o[l][:MLA_WIDTH].reshape(MLA_HEADS, MLA_V_DIM, d)
    woa = jnp.concatenate([wo_m, z(MLA_HEADS, HEAD_W - MLA_V_DIM, d)], axis=1).reshape(MLA_SLAB, d).astype(BF16)
    wob = w_o[l][MLA_WIDTH:].astype(BF16)

    lamv = jnp.zeros((8, LANES), F32).at[:4, :DIFF_HEAD_DIM].set(
        jnp.stack([lam_q1[l], lam_k1[l], lam_q2[l], lam_k2[l]]).astype(F32))
    return dict(
        w_in=w_in_wide, g_q=g_q[l][None], g_kv=g_kv[l][None], wqa=wqa, wqb=wqb, wk=wk, wvt=wvt, wdvt=wdvt,
        lamv=lamv, g_sub=g_sub[l][None], woa=woa, wob=wob, ln1_g=ln1_g[l][None], ln1_b=ln1_b[l][None],
        w_gate=w_gate[l].astype(BF16), w_up=w_up[l].astype(BF16), w_down=w_down[l].astype(BF16),
        ln2_g=ln2_g[l][None], ln2_b=ln2_b[l][None])


def _rope_table(seq):
    inv = 1.0 / (ROPE_THETA ** (jnp.arange(0, MLA_ROPE_DIM, 2, dtype=F32) / MLA_ROPE_DIM))
    ang = jnp.arange(seq, dtype=F32)[:, None] * inv[None, :]
    cos, sin = jnp.cos(ang), jnp.sin(ang)
    scale = LOG2E * (MLA_NOPE_DIM + MLA_ROPE_DIM) ** -0.5
    ones = jnp.ones((seq, MLA_NOPE_DIM), F32)
    zl = jnp.zeros((seq, MLA_NOPE_DIM), F32)
    zr = jnp.zeros((seq, HEAD_W - MLA_NOPE_DIM - MLA_ROPE_DIM), F32)
    ct_q = jnp.concatenate([ones, cos, cos, zr], axis=1) * scale
    st_q = jnp.concatenate([zl, sin, sin, zr], axis=1) * scale
    ct_k = jnp.concatenate([zl, cos, cos, zr], axis=1)
    st_k = jnp.concatenate([zl, sin, sin, zr], axis=1)
    return jnp.concatenate([ct_q, st_q, ct_k, st_k], axis=1)


def _pick(n, pref):
    while n % pref:
        pref //= 2
    return pref


def _trunk(x, ln0_g, ln0_b, layers, wrt, br):
    batch, seq, _ = x.shape
    t = batch * seq
    tm_proj = _pick(seq, 256)
    tm_tok = _pick(t, 512)
    tm_moe = _pick(t, 512)
    tq = _pick(seq, 256)
    tk = _pick(seq // 2, 512)
    nq = _pick(seq // tq, 4)
    nq_diff = _pick(seq // tq, 2)
    tab = _rope_table(seq)
    x = _ln_call(x.reshape(t, D_MODEL), ln0_g[None], ln0_b[None], tm_tok)
    for l, lw in enumerate(layers):
        lam_init = 0.8 - 0.6 * math.exp(-0.3 * l)
        qm, km, vt, d, dvt = _proj_call(x, tab, lw, seq, tm_proj)
        om = _mla_call(qm, km, vt, batch, seq, tq, tk, nq)
        od = _diff_call(d, dvt, lw["lamv"], lw["g_sub"], lam_init, batch, seq, tq, tk, nq_diff)
        xg, cnt = _outproj_call(om, od, x, lw, wrt, br, tm_tok)
        x = _moe_layer(xg, cnt, lw, tm_moe, tm_tok)
    return x.reshape(batch, seq, D_MODEL)


def kernel(x_prompt, x_sample, ln0_g, ln0_b, w_in, g_q, g_kv, w_uq, w_ukv, lam_q1, lam_k1, lam_q2, lam_k2,
           g_sub, w_o, ln1_g, ln1_b, w_router, b_router, w_gate, w_up, w_down, ln2_g, ln2_b):
    layers = [_prep_layer(l, w_in, g_q, g_kv, w_uq, w_ukv, lam_q1, lam_k1, lam_q2, lam_k2, g_sub, w_o,
                          ln1_g, ln1_b, w_gate, w_up, w_down, ln2_g, ln2_b) for l in range(DEPTH)]
    perm = jnp.arange(N_EXPERTS).reshape(N_GROUPS, EXPERTS_PER_GROUP).T.reshape(-1)
    wrt = w_router.T[perm].astype(F32)
    br = b_router[perm].astype(F32)[:, None]
    y_prompt = _trunk(x_prompt, ln0_g, ln0_b, layers, wrt, br)
    y_sample = _trunk(x_sample, ln0_g, ln0_b, layers, wrt, br)
    return (y_prompt, y_sample)
```

```python
import functools
import math

import jax
import jax.numpy as jnp
from jax import lax
from jax.experimental import pallas as pl
from jax.experimental.pallas import tpu as pltpu

F32 = jnp.float32
BF16 = jnp.bfloat16

D_MODEL = 1024
DEPTH = 4
MLA_HEADS = 8
MLA_NOPE_DIM = 64
MLA_ROPE_DIM = 32
MLA_V_DIM = 64
Q_LORA_RANK = 256
KV_LORA_RANK = 128
ROPE_THETA = 10000.0
DIFF_HEADS = 4
DIFF_HEAD_DIM = 64
MLA_WIDTH = MLA_HEADS * MLA_V_DIM
DIFF_WIDTH = DIFF_HEADS * 2 * DIFF_HEAD_DIM
N_EXPERTS = 16
N_GROUPS = 4
EXPERTS_PER_GROUP = N_EXPERTS // N_GROUPS
D_FF_EXPERT = 512
LN_EPS = 1e-5
RMS_EPS = 1e-6
ALPHA = (2 * DEPTH) ** 0.25

LANES = 128
HEAD_W = LANES
MLA_SLAB = MLA_HEADS * HEAD_W
ROPE_HALF = MLA_ROPE_DIM // 2
C_Q = 0
C_KV = C_Q + Q_LORA_RANK
C_KR_A = C_KV + KV_LORA_RANK
C_KR_B = C_KR_A + HEAD_W
C_D = C_KR_B + HEAD_W
IN_W = C_D + 2 * DIFF_WIDTH
BF16_SUBLANES = 16
MLA_VT_ROWS = MLA_V_DIM + BF16_SUBLANES
DIFF_VT_ROWS = 2 * DIFF_HEAD_DIM + BF16_SUBLANES
LOG2E = math.log2(math.e)
XG_W = D_MODEL + LANES
ROUTE_GROUP_LANE = N_EXPERTS
ROUTE_RANK_LANE = N_EXPERTS + 1

VMEM_LIMIT_BYTES = 48 * 1024 * 1024


def _cparams(*sem):
    return pltpu.CompilerParams(dimension_semantics=sem, vmem_limit_bytes=VMEM_LIMIT_BYTES)


def _layer_norm(x, g, b):
    mu = jnp.mean(x, axis=-1, keepdims=True)
    xc = x - mu
    var = jnp.mean(xc * xc, axis=-1, keepdims=True)
    return xc * lax.rsqrt(var + LN_EPS) * g + b


def _rms(x, g, eps):
    return x * lax.rsqrt(jnp.mean(x * x, axis=-1, keepdims=True) + eps) * g


def _ln_kernel(x_ref, g_ref, b_ref, o_ref):
    o_ref[...] = _layer_norm(x_ref[...], g_ref[...], b_ref[...])


def _ln_call(x, g, b, tm):
    t = x.shape[0]
    return pl.pallas_call(
        _ln_kernel,
        grid=(t // tm,),
        in_specs=[pl.BlockSpec((tm, D_MODEL), lambda i: (i, 0)),
                  pl.BlockSpec((1, D_MODEL), lambda i: (0, 0)),
                  pl.BlockSpec((1, D_MODEL), lambda i: (0, 0))],
        out_specs=pl.BlockSpec((tm, D_MODEL), lambda i: (i, 0)),
        out_shape=jax.ShapeDtypeStruct((t, D_MODEL), F32),
        compiler_params=_cparams("parallel"),
        name="ln0",
    )(x, g, b)


def _ones_rows(n):
    row = lax.broadcasted_iota(jnp.int32, (BF16_SUBLANES, n), 0)
    return jnp.where(row == 0, 1.0, 0.0).astype(BF16)


def _nt_dot(a, b):
    return lax.dot_general(a, b, (((1,), (1,)), ((), ())), preferred_element_type=F32)


def _proj_kernel(x_ref, tab_ref, win_ref, gq_ref, gkv_ref, wqa_ref, wqb_ref, wk_ref, wvt_ref, wdvt_ref,
                 qm_ref, km_ref, vt_ref, d_ref, dvt_ref):
    tm = x_ref.shape[0]
    xb = x_ref[...].astype(BF16)
    h = jnp.dot(xb, win_ref[...], preferred_element_type=F32)
    ct_q = tab_ref[:, 0 * HEAD_W:1 * HEAD_W]
    st_q = tab_ref[:, 1 * HEAD_W:2 * HEAD_W]
    ct_k = tab_ref[:, 2 * HEAD_W:3 * HEAD_W]
    st_k = tab_ref[:, 3 * HEAD_W:4 * HEAD_W]

    cq = _rms(h[:, C_Q:C_KV], gq_ref[...], RMS_EPS).astype(BF16)
    qa = jnp.dot(cq, wqa_ref[...], preferred_element_type=F32)
    qb = jnp.dot(cq, wqb_ref[...], preferred_element_type=F32)
    ckv = _rms(h[:, C_KV:C_KR_A], gkv_ref[...], RMS_EPS).astype(BF16)
    kn = jnp.dot(ckv, wk_ref[...], preferred_element_type=F32)
    kr = h[:, C_KR_A:C_KR_B] * ct_k + h[:, C_KR_B:C_D] * st_k
    ones = _ones_rows(tm)
    vt = _nt_dot(wvt_ref[...], ckv).astype(BF16)
    for hd in range(MLA_HEADS):
        sl = slice(hd * HEAD_W, (hd + 1) * HEAD_W)
        qm_ref[:, sl] = (qa[:, sl] * ct_q + qb[:, sl] * st_q).astype(BF16)
        km_ref[:, sl] = (kn[:, sl] + kr).astype(BF16)
        r0 = hd * MLA_VT_ROWS
        vt_ref[r0:r0 + MLA_V_DIM, :] = vt[hd * MLA_V_DIM:(hd + 1) * MLA_V_DIM, :]
        vt_ref[r0 + MLA_V_DIM:r0 + MLA_VT_ROWS, :] = ones
    d_ref[:, :DIFF_WIDTH] = (h[:, C_D:C_D + DIFF_WIDTH] * (LOG2E * DIFF_HEAD_DIM ** -0.5)).astype(BF16)
    d_ref[:, DIFF_WIDTH:] = h[:, C_D + DIFF_WIDTH:].astype(BF16)
    dvt = _nt_dot(wdvt_ref[...], xb).astype(BF16)
    for hd in range(DIFF_HEADS):
        r0 = hd * DIFF_VT_ROWS
        dvt_ref[r0:r0 + HEAD_W, :] = dvt[hd * HEAD_W:(hd + 1) * HEAD_W, :]
        dvt_ref[r0 + HEAD_W:r0 + DIFF_VT_ROWS, :] = ones


def _proj_call(x, tab, lw, seq, tm):
    t = x.shape[0]
    batch = t // seq
    nblk = seq // tm
    const = lambda i: (0, 0)
    row = lambda i: (i, 0)
    tcol = lambda i: (i // nblk, 0, i % nblk)
    return pl.pallas_call(
        _proj_kernel,
        grid=(t // tm,),
        in_specs=[pl.BlockSpec((tm, D_MODEL), row),
                  pl.BlockSpec((tm, 4 * HEAD_W), lambda i: (i % nblk, 0)),
                  pl.BlockSpec((D_MODEL, IN_W), const),
                  pl.BlockSpec((1, Q_LORA_RANK), const),
                  pl.BlockSpec((1, KV_LORA_RANK), const),
                  pl.BlockSpec((Q_LORA_RANK, MLA_SLAB), const),
                  pl.BlockSpec((Q_LORA_RANK, MLA_SLAB), const),
                  pl.BlockSpec((KV_LORA_RANK, MLA_SLAB), const),
                  pl.BlockSpec((MLA_WIDTH, KV_LORA_RANK), const),
                  pl.BlockSpec((DIFF_WIDTH, D_MODEL), const)],
        out_specs=[pl.BlockSpec((tm, MLA_SLAB), row),
                   pl.BlockSpec((tm, MLA_SLAB), row),
                   pl.BlockSpec((None, MLA_HEADS * MLA_VT_ROWS, tm), tcol),
                   pl.BlockSpec((tm, 2 * DIFF_WIDTH), row),
                   pl.BlockSpec((None, DIFF_HEADS * DIFF_VT_ROWS, tm), tcol)],
        out_shape=[jax.ShapeDtypeStruct((t, MLA_SLAB), BF16),
                   jax.ShapeDtypeStruct((t, MLA_SLAB), BF16),
                   jax.ShapeDtypeStruct((batch, MLA_HEADS * MLA_VT_ROWS, seq), BF16),
                   jax.ShapeDtypeStruct((t, 2 * DIFF_WIDTH), BF16),
                   jax.ShapeDtypeStruct((batch, DIFF_HEADS * DIFF_VT_ROWS, seq), BF16)],
        compiler_params=_cparams("parallel"),
        name="proj",
    )(x, tab, lw["w_in"], lw["g_q"], lw["g_kv"], lw["wqa"], lw["wqb"], lw["wk"], lw["wvt"], lw["wdvt"])


def _flash_loop(score_fn, vt_ref, s_ref, rows, unroll):
    _, nstream, tk, tq = s_ref.shape
    nk = vt_ref.shape[1] // tk

    def key_offset(j):
        return j * tk if isinstance(j, int) else pl.multiple_of(j * tk, tk)

    def produce_one(fn, slot, i):
        st = fn()
        s_ref[slot, i] = st
        return jnp.max(st, axis=0, keepdims=True)

    def consume_one(vt, slot, i, mx_i, m, acc):
        m_new = jnp.maximum(m, mx_i)
        p = jnp.exp2(s_ref[slot, i] - m_new).astype(BF16)
        return m_new, jnp.exp2(m - m_new) * acc + jnp.dot(vt, p, preferred_element_type=F32)

    def produce(j, slot):
        return tuple(produce_one(fn, slot, i) for i, fn in enumerate(score_fn(key_offset(j))))

    def consume(j, slot, mx, carry):
        vt = vt_ref[:, pl.ds(key_offset(j), tk)]
        return tuple(consume_one(vt, slot, i, mx[i], m, acc) for i, (m, acc) in enumerate(carry))

    def step(jp, slot_p, jc, slot_c, mx_c, carry):
        fns = score_fn(key_offset(jp))
        vt = vt_ref[:, pl.ds(key_offset(jc), tk)]
        mx_p, out = [], []
        for i, (m, acc) in enumerate(carry):
            mx_p.append(produce_one(fns[i], slot_p, i))
            out.append(consume_one(vt, slot_c, i, mx_c[i], m, acc))
        return tuple(mx_p), tuple(out)

    carry = tuple((jnp.full((1, tq), -jnp.inf, F32), jnp.zeros((rows, tq), F32)) for _ in range(nstream))
    mx = produce(0, 0)
    if unroll:
        for j in range(nk - 2):
            mx, carry = step(j + 1, (j + 1) % 2, j, j % 2, mx, carry)
    else:
        def pair(jj, state):
            mx_even, carry = state
            j = 2 * jj
            mx_odd, carry = step(j + 1, 1, j, 0, mx_even, carry)
            return step(j + 2, 0, j + 1, 1, mx_odd, carry)

        mx, carry = lax.fori_loop(0, nk // 2 - 1, pair, (mx, carry))
    mx, carry = step(nk - 1, 1, nk - 2, 0, mx, carry)
    carry = consume(nk - 1, 1, mx, carry)
    return [acc for _, acc in carry]


def _finish(acc, rows):
    return acc[:rows, :] * (1.0 / acc[rows:rows + 1, :])


def _mla_kernel(q_ref, k_ref, vt_ref, o_ref, s_ref):
    _, nq, tk, tq = s_ref.shape
    qs = [q_ref[i * tq:(i + 1) * tq, :] for i in range(nq)]

    def scores(off):
        ks = k_ref[pl.ds(off, tk), :]
        return [functools.partial(_nt_dot, ks, q) for q in qs]

    accs = _flash_loop(scores, vt_ref, s_ref, MLA_VT_ROWS, unroll=True)
    pad = jnp.zeros((HEAD_W - MLA_V_DIM, tq), F32)
    for i, acc in enumerate(accs):
        ot = jnp.concatenate([_finish(acc, MLA_V_DIM), pad], axis=0)
        o_ref[i * tq:(i + 1) * tq, :] = ot.T.astype(BF16)


def _mla_call(qm, km, vt, batch, seq, tq, tk, nq):
    qm, km = (a.reshape(batch, seq, MLA_SLAB) for a in (qm, km))
    bq = tq * nq
    out = pl.pallas_call(
        _mla_kernel,
        grid=(batch, MLA_HEADS, seq // bq),
        scratch_shapes=[pltpu.VMEM((2, nq, tk, tq), F32)],
        in_specs=[pl.BlockSpec((None, bq, HEAD_W), lambda b, h, i: (b, i, h)),
                  pl.BlockSpec((None, seq, HEAD_W), lambda b, h, i: (b, 0, h)),
                  pl.BlockSpec((None, MLA_VT_ROWS, seq), lambda b, h, i: (b, h, 0))],
        out_specs=pl.BlockSpec((None, bq, HEAD_W), lambda b, h, i: (b, i, h)),
        out_shape=jax.ShapeDtypeStruct((batch, seq, MLA_SLAB), BF16),
        compiler_params=_cparams("parallel", "parallel", "parallel"),
        name="mla",
    )(qm, km, vt)
    return out.reshape(batch * seq, MLA_SLAB)


def _diff_kernel(lam_ref, gsub_ref, q_ref, k_ref, vt_ref, o_ref, s_ref, *, lam_init):
    _, _, tk, tq = s_ref.shape
    hd = pl.program_id(1)
    q0 = pl.program_id(2) * q_ref.shape[0]
    slope = LOG2E * jnp.exp2(-2.0 * (hd + 1).astype(F32) * jnp.ones((1, 1), F32))
    lane = lax.broadcasted_iota(jnp.int32, (1, HEAD_W), 1)
    nq = q_ref.shape[0] // tq
    qs = []
    for i in range(nq):
        q = q_ref[i * tq:(i + 1) * tq, :]
        qs.append((jnp.where(lane < DIFF_HEAD_DIM, q, jnp.zeros_like(q)),
                   jnp.where(lane >= DIFF_HEAD_DIM, q, jnp.zeros_like(q))))
    rel = (lax.broadcasted_iota(jnp.int32, (tk, tq), 0) - lax.broadcasted_iota(jnp.int32, (tk, tq), 1)).astype(F32)

    def scores(off):
        ks = k_ref[pl.ds(off, tk), :]
        out = []
        for i, (q1, q2) in enumerate(qs):
            bias = jnp.abs(rel + (off - q0 - i * tq).astype(F32)) * (-slope)
            out += [lambda q=q1, b=bias: _nt_dot(ks, q) + b, lambda q=q2, b=bias: _nt_dot(ks, q) + b]
        return out

    accs = _flash_loop(scores, vt_ref, s_ref, DIFF_VT_ROWS, unroll=False)

    lam = (jnp.exp(jnp.sum(lam_ref[0:1, :] * lam_ref[1:2, :], axis=-1, keepdims=True))
           - jnp.exp(jnp.sum(lam_ref[2:3, :] * lam_ref[3:4, :], axis=-1, keepdims=True)) + lam_init)
    for i in range(nq):
        o = (_finish(accs[2 * i], HEAD_W) - lam * _finish(accs[2 * i + 1], HEAD_W)).T
        o_ref[i * tq:(i + 1) * tq, :] = (_rms(o, gsub_ref[...], LN_EPS) * (1.0 - lam_init)).astype(BF16)


def _diff_call(d, dvt, lamv, g_sub, lam_init, batch, seq, tq, tk, nq):
    d = d.reshape(batch, seq, 2 * DIFF_WIDTH)
    bq = tq * nq
    out = pl.pallas_call(
        functools.partial(_diff_kernel, lam_init=lam_init),
        grid=(batch, DIFF_HEADS, seq // bq),
        scratch_shapes=[pltpu.VMEM((2, 2 * nq, tk, tq), F32)],
        in_specs=[pl.BlockSpec((8, LANES), lambda b, h, i: (0, 0)),
                  pl.BlockSpec((1, HEAD_W), lambda b, h, i: (0, 0)),
                  pl.BlockSpec((None, bq, HEAD_W), lambda b, h, i: (b, i, h)),
                  pl.BlockSpec((None, seq, HEAD_W), lambda b, h, i: (b, 0, DIFF_HEADS + h)),
                  pl.BlockSpec((None, DIFF_VT_ROWS, seq), lambda b, h, i: (b, h, 0))],
        out_specs=pl.BlockSpec((None, bq, HEAD_W), lambda b, h, i: (b, i, h)),
        out_shape=jax.ShapeDtypeStruct((batch, seq, DIFF_WIDTH), BF16),
        compiler_params=_cparams("parallel", "parallel", "parallel"),
        name="diff",
    )(lamv, g_sub, d, d, dvt)
    return out.reshape(batch * seq, DIFF_WIDTH)


def _first_max(vals):
    mx = functools.reduce(jnp.maximum, vals)
    taken = None
    masks = []
    for v in vals:
        hit = v == mx
        if taken is None:
            masks.append(hit)
            taken = hit
        else:
            masks.append(jnp.logical_and(hit, jnp.logical_not(taken)))
            taken = jnp.logical_or(taken, hit)
    return masks, mx


def _outproj_kernel(om_ref, od_ref, x_ref, woa_ref, wob_ref, g_ref, b_ref, wrt_ref, br_ref,
                    xg_ref, cnt_ref):
    tm = x_ref.shape[0]

    @pl.when(pl.program_id(0) == 0)
    def _():
        cnt_ref[...] = jnp.zeros_like(cnt_ref)

    mix = (jnp.dot(om_ref[...], woa_ref[...], preferred_element_type=F32)
           + jnp.dot(od_ref[...], wob_ref[...], preferred_element_type=F32))
    x1 = _layer_norm(ALPHA * x_ref[...] + mix, g_ref[...], b_ref[...])
    xg_ref[:, :D_MODEL] = x1

    logits = lax.dot_general(wrt_ref[...], x1, (((1,), (1,)), ((), ())),
                             precision=lax.Precision.HIGHEST, preferred_element_type=F32)
    scores = jax.nn.sigmoid(logits)
    biased = scores + br_ref[...]
    member = [biased[k * N_GROUPS:(k + 1) * N_GROUPS, :] for k in range(EXPERTS_PER_GROUP)]
    score_m = [scores[k * N_GROUPS:(k + 1) * N_GROUPS, :] for k in range(EXPERTS_PER_GROUP)]
    first, top1 = _first_max(member)
    rest = [jnp.where(f, -jnp.inf, v) for f, v in zip(first, member)]
    second, top2 = _first_max(rest)
    grp_score = top1 + top2
    grp_rows = [grp_score[g:g + 1, :] for g in range(N_GROUPS)]
    grp_sel, _ = _first_max(grp_rows)
    in_grp_f = jnp.concatenate([jnp.where(s, 1.0, 0.0) for s in grp_sel], axis=0)
    in_grp = in_grp_f > 0.5
    picked = [jnp.logical_and(in_grp, jnp.logical_or(f, s)) for f, s in zip(first, second)]
    w = [jnp.where(p, sc, 0.0) for p, sc in zip(picked, score_m)]
    denom = jnp.sum(functools.reduce(jnp.add, w), axis=0, keepdims=True)
    gates = jnp.concatenate(w, axis=0) / denom

    nrow = cnt_ref.shape[0]
    grp_rows_f = jnp.concatenate([in_grp_f, jnp.zeros((BF16_SUBLANES - N_GROUPS, tm), F32)], axis=0)
    upper = jnp.where(lax.broadcasted_iota(jnp.int32, (tm, tm), 0) <= lax.broadcasted_iota(jnp.int32, (tm, tm), 1),
                      1.0, 0.0).astype(BF16)
    incl = jnp.dot(grp_rows_f.astype(BF16), upper, preferred_element_type=F32)[:nrow]
    onehot = grp_rows_f[:nrow]
    rank = jnp.sum(onehot * (incl - 1.0 + cnt_ref[:, 0:1]), axis=0, keepdims=True)
    row = lax.broadcasted_iota(jnp.int32, (nrow, tm), 0)
    gid = jnp.sum(onehot * row.astype(F32), axis=0, keepdims=True)
    cnt_ref[...] = cnt_ref[...] + incl[:, tm - 1:tm]
    meta = jnp.where(row == 0, gid, jnp.where(row == 1, rank, 0.0))
    slab_t = jnp.concatenate([gates, meta, jnp.zeros((LANES - N_EXPERTS - nrow, tm), F32)], axis=0)
    xg_ref[:, D_MODEL:] = slab_t.T


def _outproj_call(om, od, x, lw, wrt, br, tm):
    t = x.shape[0]
    const = lambda i: (0, 0)
    row = lambda i: (i, 0)
    return pl.pallas_call(
        _outproj_kernel,
        grid=(t // tm,),
        in_specs=[pl.BlockSpec((tm, MLA_SLAB), row),
                  pl.BlockSpec((tm, DIFF_WIDTH), row),
                  pl.BlockSpec((tm, D_MODEL), row),
                  pl.BlockSpec((MLA_SLAB, D_MODEL), const),
                  pl.BlockSpec((DIFF_WIDTH, D_MODEL), const),
                  pl.BlockSpec((1, D_MODEL), const),
                  pl.BlockSpec((1, D_MODEL), const),
                  pl.BlockSpec((N_EXPERTS, D_MODEL), const),
                  pl.BlockSpec((N_EXPERTS, 1), const)],
        out_specs=[pl.BlockSpec((tm, XG_W), row),
                   pl.BlockSpec((8, LANES), const)],
        out_shape=[jax.ShapeDtypeStruct((t, XG_W), F32),
                   jax.ShapeDtypeStruct((8, LANES), F32)],
        compiler_params=_cparams("arbitrary"),
        name="outproj",
    )(om, od, x, lw["woa"], lw["wob"], lw["ln1_g"], lw["ln1_b"], wrt, br)


ROW_DMA_UNROLL = 8


def _row_dma_all(row_copy, n):
    def start(r, c):
        row_copy(r).start()
        return c

    def wait(r, c):
        row_copy(r).wait()
        return c

    lax.fori_loop(0, n, start, 0, unroll=ROW_DMA_UNROLL)
    lax.fori_loop(0, n, wait, 0, unroll=ROW_DMA_UNROLL)


def _scatter_kernel(pos_ref, xg_ref, init_ref, xs_ref, sem):
    del init_ref
    tm = xg_ref.shape[0]
    base = pl.program_id(0) * tm
    _row_dma_all(lambda r: pltpu.make_async_copy(xg_ref.at[pl.ds(r, 1), :],
                                                 xs_ref.at[pl.ds(pos_ref[base + r], 1), :], sem), tm)


def _scatter_call(pos, xg, rows, tm):
    t = xg.shape[0]
    return pl.pallas_call(
        _scatter_kernel,
        grid_spec=pltpu.PrefetchScalarGridSpec(
            num_scalar_prefetch=1,
            grid=(t // tm,),
            in_specs=[pl.BlockSpec((tm, XG_W), lambda i, pos: (i, 0)),
                      pl.BlockSpec(memory_space=pl.ANY)],
            out_specs=pl.BlockSpec(memory_space=pl.ANY),
            scratch_shapes=[pltpu.SemaphoreType.DMA]),
        out_shape=jax.ShapeDtypeStruct((rows, XG_W), F32),
        input_output_aliases={2: 0},
        compiler_params=_cparams("arbitrary"),
        name="moe_scatter",
    )(pos, xg, jnp.zeros((rows, XG_W), F32))


def _moe_kernel(tg_ref, nused_ref, xs_ref, wg_ref, wu_ref, wd_ref, ys_ref, xb_ref, acc_ref):
    del tg_ref
    k = pl.program_id(1)
    used = pl.program_id(0) < nused_ref[0]

    @pl.when(jnp.logical_and(used, k == 0))
    def _():
        xb_ref[...] = xs_ref[:, :D_MODEL].astype(BF16)
        acc_ref[...] = jnp.zeros_like(acc_ref)

    @pl.when(used)
    def _():
        xb = xb_ref[...]
        hg = jnp.dot(xb, wg_ref[...], preferred_element_type=F32)
        hu = jnp.dot(xb, wu_ref[...], preferred_element_type=F32)
        lane = lax.broadcasted_iota(jnp.int32, (1, LANES), 1)
        mine = jnp.logical_and(lane >= k * N_GROUPS, lane < (k + 1) * N_GROUPS)
        gate = jnp.sum(jnp.where(mine, xs_ref[:, D_MODEL:], 0.0), axis=-1, keepdims=True)
        hh = (hg * jax.nn.sigmoid(hg) * hu * gate).astype(BF16)
        acc_ref[...] += jnp.dot(hh, wd_ref[...], preferred_element_type=F32)

    @pl.when(jnp.logical_and(used, k == EXPERTS_PER_GROUP - 1))
    def _():
        ys_ref[...] = acc_ref[...]

    @pl.when(jnp.logical_and(jnp.logical_not(used), k == EXPERTS_PER_GROUP - 1))
    def _():
        ys_ref[...] = jnp.zeros_like(ys_ref)


def _moe_call(tile_group, nused, xs, lw, tm):
    rows = xs.shape[0]

    def wmap(i, k, tg, nu):
        return (tg[i] * EXPERTS_PER_GROUP + jnp.where(i < nu[0], k, EXPERTS_PER_GROUP - 1), 0, 0)

    return pl.pallas_call(
        _moe_kernel,
        grid_spec=pltpu.PrefetchScalarGridSpec(
            num_scalar_prefetch=2,
            grid=(rows // tm, EXPERTS_PER_GROUP),
            in_specs=[pl.BlockSpec((tm, XG_W), lambda i, k, tg, nu: (i, 0)),
                      pl.BlockSpec((None, D_MODEL, D_FF_EXPERT), wmap),
                      pl.BlockSpec((None, D_MODEL, D_FF_EXPERT), wmap),
                      pl.BlockSpec((None, D_FF_EXPERT, D_MODEL), wmap)],
            out_specs=pl.BlockSpec((tm, D_MODEL), lambda i, k, tg, nu: (i, 0)),
            scratch_shapes=[pltpu.VMEM((tm, D_MODEL), BF16), pltpu.VMEM((tm, D_MODEL), F32)]),
        out_shape=jax.ShapeDtypeStruct((rows, D_MODEL), F32),
        compiler_params=_cparams("parallel", "arbitrary"),
        name="moe",
    )(tile_group, nused, xs, lw["w_gate"], lw["w_up"], lw["w_down"])


def _unsort_ln_kernel(pos_ref, x1_ref, ys_ref, g_ref, b_ref, o_ref, buf_ref, sem):
    tm = x1_ref.shape[0]
    base = pl.program_id(0) * tm
    _row_dma_all(lambda r: pltpu.make_async_copy(ys_ref.at[pl.ds(pos_ref[base + r], 1), :],
                                                 buf_ref.at[pl.ds(r, 1), :], sem), tm)
    o_ref[...] = _layer_norm(ALPHA * x1_ref[...] + buf_ref[...], g_ref[...], b_ref[...])


def _unsort_ln_call(pos, xg, ys, lw, tm):
    t = xg.shape[0]
    return pl.pallas_call(
        _unsort_ln_kernel,
        grid_spec=pltpu.PrefetchScalarGridSpec(
            num_scalar_prefetch=1,
            grid=(t // tm,),
            in_specs=[pl.BlockSpec((tm, D_MODEL), lambda i, pos: (i, 0)),
                      pl.BlockSpec(memory_space=pl.ANY),
                      pl.BlockSpec((1, D_MODEL), lambda i, pos: (0, 0)),
                      pl.BlockSpec((1, D_MODEL), lambda i, pos: (0, 0))],
            out_specs=pl.BlockSpec((tm, D_MODEL), lambda i, pos: (i, 0)),
            scratch_shapes=[pltpu.VMEM((tm, D_MODEL), F32), pltpu.SemaphoreType.DMA]),
        out_shape=jax.ShapeDtypeStruct((t, D_MODEL), F32),
        compiler_params=_cparams("arbitrary"),
        name="moe_unsort_ln",
    )(pos, xg, ys, lw["ln2_g"], lw["ln2_b"])


def _moe_layer(xg, cnt, lw, tm_sorted, tm_tok):
    t = xg.shape[0]
    counts = cnt[:N_GROUPS, 0].astype(jnp.int32)
    padded = (counts + tm_sorted - 1) // tm_sorted * tm_sorted
    ends = jnp.cumsum(padded)
    starts = ends - padded
    grp = xg[:, D_MODEL + ROUTE_GROUP_LANE].astype(jnp.int32)
    rank = xg[:, D_MODEL + ROUTE_RANK_LANE].astype(jnp.int32)
    pos = rank + jnp.sum(jnp.where(grp[:, None] == jnp.arange(N_GROUPS)[None, :], starts[None, :], 0), axis=1)
    ntiles = t // tm_sorted + N_GROUPS
    tile_start = jnp.arange(ntiles, dtype=jnp.int32) * tm_sorted
    tile_group = jnp.minimum(jnp.sum((tile_start[:, None] >= ends[None, :]).astype(jnp.int32), axis=1), N_GROUPS - 1)
    nused = (ends[-1:] // tm_sorted).astype(jnp.int32)
    xs = _scatter_call(pos, xg, ntiles * tm_sorted, tm_tok)
    ys = _moe_call(tile_group, nused, xs, lw, tm_sorted)
    return _unsort_ln_call(pos, xg, ys, lw, tm_tok)


def _rotate_half_cols(w):
    return jnp.concatenate([-w[..., ROPE_HALF:], w[..., :ROPE_HALF]], axis=-1)


def _prep_layer(l, w_in, g_q, g_kv, w_uq, w_ukv, lam_q1, lam_k1, lam_q2, lam_k2, g_sub, w_o,
                ln1_g, ln1_b, w_gate, w_up, w_down, ln2_g, ln2_b):
    d = D_MODEL
    z = lambda *s: jnp.zeros(s, F32)
    wi = w_in[l]
    c_kr = Q_LORA_RANK + KV_LORA_RANK
    c_dq = c_kr + MLA_ROPE_DIM
    w_kr = wi[:, c_kr:c_dq]
    pad_l, pad_r = z(d, MLA_NOPE_DIM), z(d, HEAD_W - MLA_NOPE_DIM - MLA_ROPE_DIM)
    w_in_wide = jnp.concatenate([
        wi[:, :c_kr],
        pad_l, w_kr, pad_r,
        pad_l, _rotate_half_cols(w_kr), pad_r,
        wi[:, c_dq:c_dq + 2 * DIFF_WIDTH]], axis=1).astype(BF16)
    wdvt = wi[:, c_dq + 2 * DIFF_WIDTH:].T.astype(BF16)

    r = Q_LORA_RANK
    wq = w_uq[l].reshape(r, MLA_HEADS, MLA_NOPE_DIM + MLA_ROPE_DIM)
    wq_rope = wq[..., MLA_NOPE_DIM:]
    tail = z(r, MLA_HEADS, HEAD_W - MLA_NOPE_DIM - MLA_ROPE_DIM)
    wqa = jnp.concatenate([wq, tail], axis=-1).reshape(r, MLA_SLAB).astype(BF16)
    wqb = jnp.concatenate([z(r, MLA_HEADS, MLA_NOPE_DIM), _rotate_half_cols(wq_rope), tail],
                          axis=-1).reshape(r, MLA_SLAB).astype(BF16)
    r = KV_LORA_RANK
    wkv = w_ukv[l].reshape(r, MLA_HEADS, MLA_NOPE_DIM + MLA_V_DIM)
    wk = jnp.concatenate([wkv[..., :MLA_NOPE_DIM], z(r, MLA_HEADS, HEAD_W - MLA_NOPE_DIM)],
                         axis=-1).reshape(r, MLA_SLAB).astype(BF16)
    wvt = wkv[..., MLA_NOPE_DIM:].reshape(r, MLA_WIDTH).T.astype(BF16)

    wo_m = w_o[l][:MLA_WIDTH].reshape(MLA_HEADS, MLA_V_DIM, d)
    woa = jnp.concatenate([wo_m, z(MLA_HEADS, HEAD_W - MLA_V_DIM, d)], axis=1).reshape(MLA_SLAB, d).astype(BF16)
    wob = w_o[l][MLA_WIDTH:].astype(BF16)

    lamv = jnp.zeros((8, LANES), F32).at[:4, :DIFF_HEAD_DIM].set(
        jnp.stack([lam_q1[l], lam_k1[l], lam_q2[l], lam_k2[l]]).astype(F32))
    return dict(
        w_in=w_in_wide, g_q=g_q[l][None], g_kv=g_kv[l][None], wqa=wqa, wqb=wqb, wk=wk, wvt=wvt, wdvt=wdvt,
        lamv=lamv, g_sub=g_sub[l][None], woa=woa, wob=wob, ln1_g=ln1_g[l][None], ln1_b=ln1_b[l][None],
        w_gate=w_gate[l].astype(BF16), w_up=w_up[l].astype(BF16), w_down=w_down[l].astype(BF16),
        ln2_g=ln2_g[l][None], ln2_b=ln2_b[l][None])


def _rope_table(seq):
    inv = 1.0 / (ROPE_THETA ** (jnp.arange(0, MLA_ROPE_DIM, 2, dtype=F32) / MLA_ROPE_DIM))
    ang = jnp.arange(seq, dtype=F32)[:, None] * inv[None, :]
    cos, sin = jnp.cos(ang), jnp.sin(ang)
    scale = LOG2E * (MLA_NOPE_DIM + MLA_ROPE_DIM) ** -0.5
    ones = jnp.ones((seq, MLA_NOPE_DIM), F32)
    zl = jnp.zeros((seq, MLA_NOPE_DIM), F32)
    zr = jnp.zeros((seq, HEAD_W - MLA_NOPE_DIM - MLA_ROPE_DIM), F32)
    ct_q = jnp.concatenate([ones, cos, cos, zr], axis=1) * scale
    st_q = jnp.concatenate([zl, sin, sin, zr], axis=1) * scale
    ct_k = jnp.concatenate([zl, cos, cos, zr], axis=1)
    st_k = jnp.concatenate([zl, sin, sin, zr], axis=1)
    return jnp.concatenate([ct_q, st_q, ct_k, st_k], axis=1)


def _pick(n, pref):
    while n % pref:
        pref //= 2
    return pref


def _trunk(x, ln0_g, ln0_b, layers, wrt, br):
    batch, seq, _ = x.shape
    t = batch * seq
    tm_proj = _pick(seq, 256)
    tm_tok = _pick(t, 1024)
    tm_moe = _pick(t, 512)
    tq = _pick(seq, 256)
    tk = _pick(seq // 2, 512)
    nq = _pick(seq // tq, 4)
    nq_diff = _pick(seq // tq, 2)
    tab = _rope_table(seq)
    x = _ln_call(x.reshape(t, D_MODEL), ln0_g[None], ln0_b[None], tm_tok)
    for l, lw in enumerate(layers):
        lam_init = 0.8 - 0.6 * math.exp(-0.3 * l)
        qm, km, vt, d, dvt = _proj_call(x, tab, lw, seq, tm_proj)
        om = _mla_call(qm, km, vt, batch, seq, tq, tk, nq)
        od = _diff_call(d, dvt, lw["lamv"], lw["g_sub"], lam_init, batch, seq, tq, tk, nq_diff)
        xg, cnt = _outproj_call(om, od, x, lw, wrt, br, tm_tok)
        x = _moe_layer(xg, cnt, lw, tm_moe, tm_tok)
    return x.reshape(batch, seq, D_MODEL)


def kernel(x_prompt, x_sample, ln0_g, ln0_b, w_in, g_q, g_kv, w_uq, w_ukv, lam_q1, lam_k1, lam_q2, lam_k2,
           g_sub, w_o, ln1_g, ln1_b, w_router, b_router, w_gate, w_up, w_down, ln2_g, ln2_b):
    layers = [_prep_layer(l, w_in, g_q, g_kv, w_uq, w_ukv, lam_q1, lam_k1, lam_q2, lam_k2, g_sub, w_o,
                          ln1_g, ln1_b, w_gate, w_up, w_down, ln2_g, ln2_b) for l in range(DEPTH)]
    perm = jnp.arange(N_EXPERTS).reshape(N_GROUPS, EXPERTS_PER_GROUP).T.reshape(-1)
    wrt = w_router.T[perm].astype(F32)
    br = b_router[perm].astype(F32)[:, None]
    y_prompt = _trunk(x_prompt, ln0_g, ln0_b, layers, wrt, br)
    y_sample = _trunk(x_sample, ln0_g, ln0_b, layers, wrt, br)
    return (y_prompt, y_sample)
```

```python
import functools
import math

import jax
import jax.numpy as jnp
from jax import lax
from jax.experimental import pallas as pl
from jax.experimental.pallas import tpu as pltpu

F32 = jnp.float32
BF16 = jnp.bfloat16

D_MODEL = 1024
DEPTH = 4
MLA_HEADS = 8
MLA_NOPE_DIM = 64
MLA_ROPE_DIM = 32
MLA_V_DIM = 64
Q_LORA_RANK = 256
KV_LORA_RANK = 128
ROPE_THETA = 10000.0
DIFF_HEADS = 4
DIFF_HEAD_DIM = 64
MLA_WIDTH = MLA_HEADS * MLA_V_DIM
DIFF_WIDTH = DIFF_HEADS * 2 * DIFF_HEAD_DIM
N_EXPERTS = 16
N_GROUPS = 4
EXPERTS_PER_GROUP = N_EXPERTS // N_GROUPS
D_FF_EXPERT = 512
LN_EPS = 1e-5
RMS_EPS = 1e-6
ALPHA = (2 * DEPTH) ** 0.25

LANES = 128
HEAD_W = LANES
MLA_SLAB = MLA_HEADS * HEAD_W
ROPE_HALF = MLA_ROPE_DIM // 2
C_Q = 0
C_KV = C_Q + Q_LORA_RANK
C_KR_A = C_KV + KV_LORA_RANK
C_KR_B = C_KR_A + HEAD_W
C_D = C_KR_B + HEAD_W
IN_W = C_D + 2 * DIFF_WIDTH
BF16_SUBLANES = 16
MLA_VT_ROWS = MLA_V_DIM + BF16_SUBLANES
DIFF_VT_ROWS = 2 * DIFF_HEAD_DIM + BF16_SUBLANES
LOG2E = math.log2(math.e)
XG_W = D_MODEL + LANES
ROUTE_GROUP_LANE = N_EXPERTS
ROUTE_RANK_LANE = N_EXPERTS + 1

VMEM_LIMIT_BYTES = 48 * 1024 * 1024


def _cparams(*sem):
    return pltpu.CompilerParams(dimension_semantics=sem, vmem_limit_bytes=VMEM_LIMIT_BYTES)


def _layer_norm(x, g, b):
    mu = jnp.mean(x, axis=-1, keepdims=True)
    xc = x - mu
    var = jnp.mean(xc * xc, axis=-1, keepdims=True)
    return xc * lax.rsqrt(var + LN_EPS) * g + b


def _rms(x, g, eps):
    return x * lax.rsqrt(jnp.mean(x * x, axis=-1, keepdims=True) + eps) * g


def _ln_kernel(x_ref, g_ref, b_ref, o_ref):
    o_ref[...] = _layer_norm(x_ref[...], g_ref[...], b_ref[...])


def _ln_call(x, g, b, tm):
    t = x.shape[0]
    return pl.pallas_call(
        _ln_kernel,
        grid=(t // tm,),
        in_specs=[pl.BlockSpec((tm, D_MODEL), lambda i: (i, 0)),
                  pl.BlockSpec((1, D_MODEL), lambda i: (0, 0)),
                  pl.BlockSpec((1, D_MODEL), lambda i: (0, 0))],
        out_specs=pl.BlockSpec((tm, D_MODEL), lambda i: (i, 0)),
        out_shape=jax.ShapeDtypeStruct((t, D_MODEL), F32),
        compiler_params=_cparams("parallel"),
        name="ln0",
    )(x, g, b)


def _ones_rows(n):
    row = lax.broadcasted_iota(jnp.int32, (BF16_SUBLANES, n), 0)
    return jnp.where(row == 0, 1.0, 0.0).astype(BF16)


def _nt_dot(a, b):
    return lax.dot_general(a, b, (((1,), (1,)), ((), ())), preferred_element_type=F32)


def _proj_kernel(x_ref, tab_ref, win_ref, gq_ref, gkv_ref, wqa_ref, wqb_ref, wk_ref, wvt_ref, wdvt_ref,
                 qm_ref, km_ref, vt_ref, d_ref, dvt_ref):
    tm = x_ref.shape[0]
    xb = x_ref[...].astype(BF16)
    h = jnp.dot(xb, win_ref[...], preferred_element_type=F32)
    ct_q = tab_ref[:, 0 * HEAD_W:1 * HEAD_W]
    st_q = tab_ref[:, 1 * HEAD_W:2 * HEAD_W]
    ct_k = tab_ref[:, 2 * HEAD_W:3 * HEAD_W]
    st_k = tab_ref[:, 3 * HEAD_W:4 * HEAD_W]

    cq = _rms(h[:, C_Q:C_KV], gq_ref[...], RMS_EPS).astype(BF16)
    qa = jnp.dot(cq, wqa_ref[...], preferred_element_type=F32)
    qb = jnp.dot(cq, wqb_ref[...], preferred_element_type=F32)
    ckv = _rms(h[:, C_KV:C_KR_A], gkv_ref[...], RMS_EPS).astype(BF16)
    kn = jnp.dot(ckv, wk_ref[...], preferred_element_type=F32)
    kr = h[:, C_KR_A:C_KR_B] * ct_k + h[:, C_KR_B:C_D] * st_k
    ones = _ones_rows(tm)
    vt = _nt_dot(wvt_ref[...], ckv).astype(BF16)
    for hd in range(MLA_HEADS):
        sl = slice(hd * HEAD_W, (hd + 1) * HEAD_W)
        qm_ref[:, sl] = (qa[:, sl] * ct_q + qb[:, sl] * st_q).astype(BF16)
        km_ref[:, sl] = (kn[:, sl] + kr).astype(BF16)
        r0 = hd * MLA_VT_ROWS
        vt_ref[r0:r0 + MLA_V_DIM, :] = vt[hd * MLA_V_DIM:(hd + 1) * MLA_V_DIM, :]
        vt_ref[r0 + MLA_V_DIM:r0 + MLA_VT_ROWS, :] = ones
    d_ref[:, :DIFF_WIDTH] = (h[:, C_D:C_D + DIFF_WIDTH] * (LOG2E * DIFF_HEAD_DIM ** -0.5)).astype(BF16)
    d_ref[:, DIFF_WIDTH:] = h[:, C_D + DIFF_WIDTH:].astype(BF16)
    dvt = _nt_dot(wdvt_ref[...], xb).astype(BF16)
    for hd in range(DIFF_HEADS):
        r0 = hd * DIFF_VT_ROWS
        dvt_ref[r0:r0 + HEAD_W, :] = dvt[hd * HEAD_W:(hd + 1) * HEAD_W, :]
        dvt_ref[r0 + HEAD_W:r0 + DIFF_VT_ROWS, :] = ones


def _proj_call(x, tab, lw, seq, tm):
    t = x.shape[0]
    batch = t // seq
    nblk = seq // tm
    const = lambda i: (0, 0)
    row = lambda i: (i, 0)
    tcol = lambda i: (i // nblk, 0, i % nblk)
    return pl.pallas_call(
        _proj_kernel,
        grid=(t // tm,),
        in_specs=[pl.BlockSpec((tm, D_MODEL), row),
                  pl.BlockSpec((tm, 4 * HEAD_W), lambda i: (i % nblk, 0)),
                  pl.BlockSpec((D_MODEL, IN_W), const),
                  pl.BlockSpec((1, Q_LORA_RANK), const),
                  pl.BlockSpec((1, KV_LORA_RANK), const),
                  pl.BlockSpec((Q_LORA_RANK, MLA_SLAB), const),
                  pl.BlockSpec((Q_LORA_RANK, MLA_SLAB), const),
                  pl.BlockSpec((KV_LORA_RANK, MLA_SLAB), const),
                  pl.BlockSpec((MLA_WIDTH, KV_LORA_RANK), const),
                  pl.BlockSpec((DIFF_WIDTH, D_MODEL), const)],
        out_specs=[pl.BlockSpec((tm, MLA_SLAB), row),
                   pl.BlockSpec((tm, MLA_SLAB), row),
                   pl.BlockSpec((None, MLA_HEADS * MLA_VT_ROWS, tm), tcol),
                   pl.BlockSpec((tm, 2 * DIFF_WIDTH), row),
                   pl.BlockSpec((None, DIFF_HEADS * DIFF_VT_ROWS, tm), tcol)],
        out_shape=[jax.ShapeDtypeStruct((t, MLA_SLAB), BF16),
                   jax.ShapeDtypeStruct((t, MLA_SLAB), BF16),
                   jax.ShapeDtypeStruct((batch, MLA_HEADS * MLA_VT_ROWS, seq), BF16),
                   jax.ShapeDtypeStruct((t, 2 * DIFF_WIDTH), BF16),
                   jax.ShapeDtypeStruct((batch, DIFF_HEADS * DIFF_VT_ROWS, seq), BF16)],
        compiler_params=_cparams("parallel"),
        name="proj",
    )(x, tab, lw["w_in"], lw["g_q"], lw["g_kv"], lw["wqa"], lw["wqb"], lw["wk"], lw["wvt"], lw["wdvt"])


def _flash_loop(score_fn, vt_ref, s_ref, rows, unroll):
    _, nstream, tk, tq = s_ref.shape
    nk = vt_ref.shape[1] // tk

    def key_offset(j):
        return j * tk if isinstance(j, int) else pl.multiple_of(j * tk, tk)

    def produce_one(fn, slot, i):
        st = fn()
        s_ref[slot, i] = st
        return jnp.max(st, axis=0, keepdims=True)

    def consume_one(vt, slot, i, mx_i, m, acc):
        m_new = jnp.maximum(m, mx_i)
        p = jnp.exp2(s_ref[slot, i] - m_new).astype(BF16)
        return m_new, jnp.exp2(m - m_new) * acc + jnp.dot(vt, p, preferred_element_type=F32)

    def produce(j, slot):
        return tuple(produce_one(fn, slot, i) for i, fn in enumerate(score_fn(key_offset(j))))

    def consume(j, slot, mx, carry):
        vt = vt_ref[:, pl.ds(key_offset(j), tk)]
        return tuple(consume_one(vt, slot, i, mx[i], m, acc) for i, (m, acc) in enumerate(carry))

    def step(jp, slot_p, jc, slot_c, mx_c, carry):
        fns = score_fn(key_offset(jp))
        vt = vt_ref[:, pl.ds(key_offset(jc), tk)]
        mx_p, out = [], []
        for i, (m, acc) in enumerate(carry):
            mx_p.append(produce_one(fns[i], slot_p, i))
            out.append(consume_one(vt, slot_c, i, mx_c[i], m, acc))
        return tuple(mx_p), tuple(out)

    carry = tuple((jnp.full((1, tq), -jnp.inf, F32), jnp.zeros((rows, tq), F32)) for _ in range(nstream))
    mx = produce(0, 0)
    if unroll:
        for j in range(nk - 2):
            mx, carry = step(j + 1, (j + 1) % 2, j, j % 2, mx, carry)
    else:
        def pair(jj, state):
            mx_even, carry = state
            j = 2 * jj
            mx_odd, carry = step(j + 1, 1, j, 0, mx_even, carry)
            return step(j + 2, 0, j + 1, 1, mx_odd, carry)

        mx, carry = lax.fori_loop(0, nk // 2 - 1, pair, (mx, carry))
    mx, carry = step(nk - 1, 1, nk - 2, 0, mx, carry)
    carry = consume(nk - 1, 1, mx, carry)
    return [acc for _, acc in carry]


def _finish(acc, rows):
    return acc[:rows, :] * (1.0 / acc[rows:rows + 1, :])


def _mla_kernel(q_ref, k_ref, vt_ref, o_ref, s_ref):
    _, nq, tk, tq = s_ref.shape
    qs = [q_ref[i * tq:(i + 1) * tq, :] for i in range(nq)]

    def scores(off):
        ks = k_ref[pl.ds(off, tk), :]
        return [functools.partial(_nt_dot, ks, q) for q in qs]

    accs = _flash_loop(scores, vt_ref, s_ref, MLA_VT_ROWS, unroll=True)
    pad = jnp.zeros((HEAD_W - MLA_V_DIM, tq), F32)
    for i, acc in enumerate(accs):
        ot = jnp.concatenate([_finish(acc, MLA_V_DIM), pad], axis=0)
        o_ref[i * tq:(i + 1) * tq, :] = ot.T.astype(BF16)


def _mla_call(qm, km, vt, batch, seq, tq, tk, nq):
    qm, km = (a.reshape(batch, seq, MLA_SLAB) for a in (qm, km))
    bq = tq * nq
    out = pl.pallas_call(
        _mla_kernel,
        grid=(batch, MLA_HEADS, seq // bq),
        scratch_shapes=[pltpu.VMEM((2, nq, tk, tq), F32)],
        in_specs=[pl.BlockSpec((None, bq, HEAD_W), lambda b, h, i: (b, i, h)),
                  pl.BlockSpec((None, seq, HEAD_W), lambda b, h, i: (b, 0, h)),
                  pl.BlockSpec((None, MLA_VT_ROWS, seq), lambda b, h, i: (b, h, 0))],
        out_specs=pl.BlockSpec((None, bq, HEAD_W), lambda b, h, i: (b, i, h)),
        out_shape=jax.ShapeDtypeStruct((batch, seq, MLA_SLAB), BF16),
        compiler_params=_cparams("parallel", "parallel", "parallel"),
        name="mla",
    )(qm, km, vt)
    return out.reshape(batch * seq, MLA_SLAB)


def _diff_kernel(lam_ref, gsub_ref, q_ref, k_ref, vt_ref, o_ref, s_ref, *, lam_init):
    _, _, tk, tq = s_ref.shape
    hd = pl.program_id(1)
    q0 = pl.program_id(2) * q_ref.shape[0]
    slope = LOG2E * jnp.exp2(-2.0 * (hd + 1).astype(F32) * jnp.ones((1, 1), F32))
    lane = lax.broadcasted_iota(jnp.int32, (1, HEAD_W), 1)
    nq = q_ref.shape[0] // tq
    qs = []
    for i in range(nq):
        q = q_ref[i * tq:(i + 1) * tq, :]
        qs.append((jnp.where(lane < DIFF_HEAD_DIM, q, jnp.zeros_like(q)),
                   jnp.where(lane >= DIFF_HEAD_DIM, q, jnp.zeros_like(q))))
    rel = (lax.broadcasted_iota(jnp.int32, (tk, tq), 0) - lax.broadcasted_iota(jnp.int32, (tk, tq), 1)).astype(F32)

    def scores(off):
        ks = k_ref[pl.ds(off, tk), :]
        out = []
        for i, (q1, q2) in enumerate(qs):
            bias = jnp.abs(rel + (off - q0 - i * tq).astype(F32)) * (-slope)
            out += [lambda q=q1, b=bias: _nt_dot(ks, q) + b, lambda q=q2, b=bias: _nt_dot(ks, q) + b]
        return out

    accs = _flash_loop(scores, vt_ref, s_ref, DIFF_VT_ROWS, unroll=False)

    lam = (jnp.exp(jnp.sum(lam_ref[0:1, :] * lam_ref[1:2, :], axis=-1, keepdims=True))
           - jnp.exp(jnp.sum(lam_ref[2:3, :] * lam_ref[3:4, :], axis=-1, keepdims=True)) + lam_init)
    for i in range(nq):
        o = (_finish(accs[2 * i], HEAD_W) - lam * _finish(accs[2 * i + 1], HEAD_W)).T
        o_ref[i * tq:(i + 1) * tq, :] = (_rms(o, gsub_ref[...], LN_EPS) * (1.0 - lam_init)).astype(BF16)


def _diff_call(d, dvt, lamv, g_sub, lam_init, batch, seq, tq, tk, nq):
    d = d.reshape(batch, seq, 2 * DIFF_WIDTH)
    bq = tq * nq
    out = pl.pallas_call(
        functools.partial(_diff_kernel, lam_init=lam_init),
        grid=(batch, DIFF_HEADS, seq // bq),
        scratch_shapes=[pltpu.VMEM((2, 2 * nq, tk, tq), F32)],
        in_specs=[pl.BlockSpec((8, LANES), lambda b, h, i: (0, 0)),
                  pl.BlockSpec((1, HEAD_W), lambda b, h, i: (0, 0)),
                  pl.BlockSpec((None, bq, HEAD_W), lambda b, h, i: (b, i, h)),
                  pl.BlockSpec((None, seq, HEAD_W), lambda b, h, i: (b, 0, DIFF_HEADS + h)),
                  pl.BlockSpec((None, DIFF_VT_ROWS, seq), lambda b, h, i: (b, h, 0))],
        out_specs=pl.BlockSpec((None, bq, HEAD_W), lambda b, h, i: (b, i, h)),
        out_shape=jax.ShapeDtypeStruct((batch, seq, DIFF_WIDTH), BF16),
        compiler_params=_cparams("parallel", "parallel", "parallel"),
        name="diff",
    )(lamv, g_sub, d, d, dvt)
    return out.reshape(batch * seq, DIFF_WIDTH)


def _first_max(vals):
    mx = functools.reduce(jnp.maximum, vals)
    taken = None
    masks = []
    for v in vals:
        hit = v == mx
        if taken is None:
            masks.append(hit)
            taken = hit
        else:
            masks.append(jnp.logical_and(hit, jnp.logical_not(taken)))
            taken = jnp.logical_or(taken, hit)
    return masks, mx


def _outproj_kernel(om_ref, od_ref, x_ref, woa_ref, wob_ref, g_ref, b_ref, wrt_ref, br_ref,
                    xg_ref, cnt_ref, upper_ref):
    tm = x_ref.shape[0]

    @pl.when(pl.program_id(0) == 0)
    def _():
        cnt_ref[...] = jnp.zeros_like(cnt_ref)
        upper_ref[...] = jnp.where(
            lax.broadcasted_iota(jnp.int32, (tm, tm), 0) <= lax.broadcasted_iota(jnp.int32, (tm, tm), 1),
            1.0, 0.0).astype(BF16)

    mix = (jnp.dot(om_ref[...], woa_ref[...], preferred_element_type=F32)
           + jnp.dot(od_ref[...], wob_ref[...], preferred_element_type=F32))
    x1 = _layer_norm(ALPHA * x_ref[...] + mix, g_ref[...], b_ref[...])
    xg_ref[:, :D_MODEL] = x1

    logits = lax.dot_general(wrt_ref[...], x1, (((1,), (1,)), ((), ())),
                             precision=lax.Precision.HIGHEST, preferred_element_type=F32)
    scores = jax.nn.sigmoid(logits)
    biased = scores + br_ref[...]
    member = [biased[k * N_GROUPS:(k + 1) * N_GROUPS, :] for k in range(EXPERTS_PER_GROUP)]
    score_m = [scores[k * N_GROUPS:(k + 1) * N_GROUPS, :] for k in range(EXPERTS_PER_GROUP)]
    first, top1 = _first_max(member)
    rest = [jnp.where(f, -jnp.inf, v) for f, v in zip(first, member)]
    second, top2 = _first_max(rest)
    grp_score = top1 + top2
    grp_rows = [grp_score[g:g + 1, :] for g in range(N_GROUPS)]
    grp_sel, _ = _first_max(grp_rows)
    in_grp_f = jnp.concatenate([jnp.where(s, 1.0, 0.0) for s in grp_sel], axis=0)
    in_grp = in_grp_f > 0.5
    picked = [jnp.logical_and(in_grp, jnp.logical_or(f, s)) for f, s in zip(first, second)]
    w = [jnp.where(p, sc, 0.0) for p, sc in zip(picked, score_m)]
    denom = jnp.sum(functools.reduce(jnp.add, w), axis=0, keepdims=True)
    gates = jnp.concatenate(w, axis=0) / denom

    nrow = cnt_ref.shape[0]
    grp_rows_f = jnp.concatenate([in_grp_f, jnp.zeros((BF16_SUBLANES - N_GROUPS, tm), F32)], axis=0)
    incl = jnp.dot(grp_rows_f.astype(BF16), upper_ref[...], preferred_element_type=F32)[:nrow]
    onehot = grp_rows_f[:nrow]
    rank = jnp.sum(onehot * (incl - 1.0 + cnt_ref[:, 0:1]), axis=0, keepdims=True)
    row = lax.broadcasted_iota(jnp.int32, (nrow, tm), 0)
    gid = jnp.sum(onehot * row.astype(F32), axis=0, keepdims=True)
    cnt_ref[...] = cnt_ref[...] + incl[:, tm - 1:tm]
    meta = jnp.where(row == 0, gid, jnp.where(row == 1, rank, 0.0))
    slab_t = jnp.concatenate([gates, meta, jnp.zeros((LANES - N_EXPERTS - nrow, tm), F32)], axis=0)
    xg_ref[:, D_MODEL:] = slab_t.T


def _outproj_call(om, od, x, lw, wrt, br, tm):
    t = x.shape[0]
    const = lambda i: (0, 0)
    row = lambda i: (i, 0)
    return pl.pallas_call(
        _outproj_kernel,
        grid=(t // tm,),
        in_specs=[pl.BlockSpec((tm, MLA_SLAB), row),
                  pl.BlockSpec((tm, DIFF_WIDTH), row),
                  pl.BlockSpec((tm, D_MODEL), row),
                  pl.BlockSpec((MLA_SLAB, D_MODEL), const),
                  pl.BlockSpec((DIFF_WIDTH, D_MODEL), const),
                  pl.BlockSpec((1, D_MODEL), const),
                  pl.BlockSpec((1, D_MODEL), const),
                  pl.BlockSpec((N_EXPERTS, D_MODEL), const),
                  pl.BlockSpec((N_EXPERTS, 1), const)],
        out_specs=[pl.BlockSpec((tm, XG_W), row),
                   pl.BlockSpec((8, LANES), const)],
        out_shape=[jax.ShapeDtypeStruct((t, XG_W), F32),
                   jax.ShapeDtypeStruct((8, LANES), F32)],
        scratch_shapes=[pltpu.VMEM((tm, tm), BF16)],
        compiler_params=_cparams("arbitrary"),
        name="outproj",
    )(om, od, x, lw["woa"], lw["wob"], lw["ln1_g"], lw["ln1_b"], wrt, br)


ROW_DMA_UNROLL = 8


def _row_dma_all(row_copy, n):
    def start(r, c):
        row_copy(r).start()
        return c

    def wait(r, c):
        row_copy(r).wait()
        return c

    lax.fori_loop(0, n, start, 0, unroll=ROW_DMA_UNROLL)
    lax.fori_loop(0, n, wait, 0, unroll=ROW_DMA_UNROLL)


def _scatter_kernel(pos_ref, xg_ref, init_ref, xs_ref, sem):
    del init_ref
    tm = xg_ref.shape[0]
    base = pl.program_id(0) * tm
    _row_dma_all(lambda r: pltpu.make_async_copy(xg_ref.at[pl.ds(r, 1), :],
                                                 xs_ref.at[pl.ds(pos_ref[base + r], 1), :], sem), tm)


def _scatter_call(pos, xg, rows, tm):
    t = xg.shape[0]
    return pl.pallas_call(
        _scatter_kernel,
        grid_spec=pltpu.PrefetchScalarGridSpec(
            num_scalar_prefetch=1,
            grid=(t // tm,),
            in_specs=[pl.BlockSpec((tm, XG_W), lambda i, pos: (i, 0)),
                      pl.BlockSpec(memory_space=pl.ANY)],
            out_specs=pl.BlockSpec(memory_space=pl.ANY),
            scratch_shapes=[pltpu.SemaphoreType.DMA]),
        out_shape=jax.ShapeDtypeStruct((rows, XG_W), F32),
        input_output_aliases={2: 0},
        compiler_params=_cparams("arbitrary"),
        name="moe_scatter",
    )(pos, xg, jnp.zeros((rows, XG_W), F32))


def _moe_kernel(tg_ref, nused_ref, xs_ref, wg_ref, wu_ref, wd_ref, ys_ref, xb_ref, acc_ref):
    del tg_ref
    k = pl.program_id(1)
    used = pl.program_id(0) < nused_ref[0]

    @pl.when(jnp.logical_and(used, k == 0))
    def _():
        xb_ref[...] = xs_ref[:, :D_MODEL].astype(BF16)
        acc_ref[...] = jnp.zeros_like(acc_ref)

    @pl.when(used)
    def _():
        xb = xb_ref[...]
        hg = jnp.dot(xb, wg_ref[...], preferred_element_type=F32)
        hu = jnp.dot(xb, wu_ref[...], preferred_element_type=F32)
        lane = lax.broadcasted_iota(jnp.int32, (1, LANES), 1)
        mine = jnp.logical_and(lane >= k * N_GROUPS, lane < (k + 1) * N_GROUPS)
        gate = jnp.sum(jnp.where(mine, xs_ref[:, D_MODEL:], 0.0), axis=-1, keepdims=True)
        hh = (hg * jax.nn.sigmoid(hg) * hu * gate).astype(BF16)
        acc_ref[...] += jnp.dot(hh, wd_ref[...], preferred_element_type=F32)

    @pl.when(jnp.logical_and(used, k == EXPERTS_PER_GROUP - 1))
    def _():
        ys_ref[...] = acc_ref[...]

    @pl.when(jnp.logical_and(jnp.logical_not(used), k == EXPERTS_PER_GROUP - 1))
    def _():
        ys_ref[...] = jnp.zeros_like(ys_ref)


def _moe_call(tile_group, nused, xs, lw, tm):
    rows = xs.shape[0]

    def wmap(i, k, tg, nu):
        return (tg[i] * EXPERTS_PER_GROUP + jnp.where(i < nu[0], k, EXPERTS_PER_GROUP - 1), 0, 0)

    return pl.pallas_call(
        _moe_kernel,
        grid_spec=pltpu.PrefetchScalarGridSpec(
            num_scalar_prefetch=2,
            grid=(rows // tm, EXPERTS_PER_GROUP),
            in_specs=[pl.BlockSpec((tm, XG_W), lambda i, k, tg, nu: (i, 0)),
                      pl.BlockSpec((None, D_MODEL, D_FF_EXPERT), wmap),
                      pl.BlockSpec((None, D_MODEL, D_FF_EXPERT), wmap),
                      pl.BlockSpec((None, D_FF_EXPERT, D_MODEL), wmap)],
            out_specs=pl.BlockSpec((tm, D_MODEL), lambda i, k, tg, nu: (i, 0)),
            scratch_shapes=[pltpu.VMEM((tm, D_MODEL), BF16), pltpu.VMEM((tm, D_MODEL), F32)]),
        out_shape=jax.ShapeDtypeStruct((rows, D_MODEL), F32),
        compiler_params=_cparams("parallel", "arbitrary"),
        name="moe",
    )(tile_group, nused, xs, lw["w_gate"], lw["w_up"], lw["w_down"])


def _unsort_ln_kernel(pos_ref, x1_ref, ys_ref, g_ref, b_ref, o_ref, buf_ref, sem):
    tm = x1_ref.shape[0]
    base = pl.program_id(0) * tm
    _row_dma_all(lambda r: pltpu.make_async_copy(ys_ref.at[pl.ds(pos_ref[base + r], 1), :],
                                                 buf_ref.at[pl.ds(r, 1), :], sem), tm)
    o_ref[...] = _layer_norm(ALPHA * x1_ref[...] + buf_ref[...], g_ref[...], b_ref[...])


def _unsort_ln_call(pos, xg, ys, lw, tm):
    t = xg.shape[0]
    return pl.pallas_call(
        _unsort_ln_kernel,
        grid_spec=pltpu.PrefetchScalarGridSpec(
            num_scalar_prefetch=1,
            grid=(t // tm,),
            in_specs=[pl.BlockSpec((tm, D_MODEL), lambda i, pos: (i, 0)),
                      pl.BlockSpec(memory_space=pl.ANY),
                      pl.BlockSpec((1, D_MODEL), lambda i, pos: (0, 0)),
                      pl.BlockSpec((1, D_MODEL), lambda i, pos: (0, 0))],
            out_specs=pl.BlockSpec((tm, D_MODEL), lambda i, pos: (i, 0)),
            scratch_shapes=[pltpu.VMEM((tm, D_MODEL), F32), pltpu.SemaphoreType.DMA]),
        out_shape=jax.ShapeDtypeStruct((t, D_MODEL), F32),
        compiler_params=_cparams("arbitrary"),
        name="moe_unsort_ln",
    )(pos, xg, ys, lw["ln2_g"], lw["ln2_b"])


def _moe_layer(xg, cnt, lw, tm_sorted, tm_tok):
    t = xg.shape[0]
    counts = cnt[:N_GROUPS, 0].astype(jnp.int32)
    padded = (counts + tm_sorted - 1) // tm_sorted * tm_sorted
    ends = jnp.cumsum(padded)
    starts = ends - padded
    grp = xg[:, D_MODEL + ROUTE_GROUP_LANE].astype(jnp.int32)
    rank = xg[:, D_MODEL + ROUTE_RANK_LANE].astype(jnp.int32)
    pos = rank + jnp.sum(jnp.where(grp[:, None] == jnp.arange(N_GROUPS)[None, :], starts[None, :], 0), axis=1)
    ntiles = t // tm_sorted + N_GROUPS
    tile_start = jnp.arange(ntiles, dtype=jnp.int32) * tm_sorted
    tile_group = jnp.minimum(jnp.sum((tile_start[:, None] >= ends[None, :]).astype(jnp.int32), axis=1), N_GROUPS - 1)
    nused = (ends[-1:] // tm_sorted).astype(jnp.int32)
    xs = _scatter_call(pos, xg, ntiles * tm_sorted, tm_tok)
    ys = _moe_call(tile_group, nused, xs, lw, tm_sorted)
    return _unsort_ln_call(pos, xg, ys, lw, tm_tok)


def _rotate_half_cols(w):
    return jnp.concatenate([-w[..., ROPE_HALF:], w[..., :ROPE_HALF]], axis=-1)


def _prep_layer(l, w_in, g_q, g_kv, w_uq, w_ukv, lam_q1, lam_k1, lam_q2, lam_k2, g_sub, w_o,
                ln1_g, ln1_b, w_gate, w_up, w_down, ln2_g, ln2_b):
    d = D_MODEL
    z = lambda *s: jnp.zeros(s, F32)
    wi = w_in[l]
    c_kr = Q_LORA_RANK + KV_LORA_RANK
    c_dq = c_kr + MLA_ROPE_DIM
    w_kr = wi[:, c_kr:c_dq]
    pad_l, pad_r = z(d, MLA_NOPE_DIM), z(d, HEAD_W - MLA_NOPE_DIM - MLA_ROPE_DIM)
    w_in_wide = jnp.concatenate([
        wi[:, :c_kr],
        pad_l, w_kr, pad_r,
        pad_l, _rotate_half_cols(w_kr), pad_r,
        wi[:, c_dq:c_dq + 2 * DIFF_WIDTH]], axis=1).astype(BF16)
    wdvt = wi[:, c_dq + 2 * DIFF_WIDTH:].T.astype(BF16)

    r = Q_LORA_RANK
    wq = w_uq[l].reshape(r, MLA_HEADS, MLA_NOPE_DIM + MLA_ROPE_DIM)
    wq_rope = wq[..., MLA_NOPE_DIM:]
    tail = z(r, MLA_HEADS, HEAD_W - MLA_NOPE_DIM - MLA_ROPE_DIM)
    wqa = jnp.concatenate([wq, tail], axis=-1).reshape(r, MLA_SLAB).astype(BF16)
    wqb = jnp.concatenate([z(r, MLA_HEADS, MLA_NOPE_DIM), _rotate_half_cols(wq_rope), tail],
                          axis=-1).reshape(r, MLA_SLAB).astype(BF16)
    r = KV_LORA_RANK
    wkv = w_ukv[l].reshape(r, MLA_HEADS, MLA_NOPE_DIM + MLA_V_DIM)
    wk = jnp.concatenate([wkv[..., :MLA_NOPE_DIM], z(r, MLA_HEADS, HEAD_W - MLA_NOPE_DIM)],
                         axis=-1).reshape(r, MLA_SLAB).astype(BF16)
    wvt = wkv[..., MLA_NOPE_DIM:].reshape(r, MLA_WIDTH).T.astype(BF16)

    wo_m = w_o[l][:MLA_WIDTH].reshape(MLA_HEADS, MLA_V_DIM, d)
    woa = jnp.concatenate([wo_m, z(MLA_HEADS, HEAD_W - MLA_V_DIM, d)], axis=1).reshape(MLA_SLAB, d).astype(BF16)
    wob = w_o[l][MLA_WIDTH:].astype(BF16)

    lamv = jnp.zeros((8, LANES), F32).at[:4, :DIFF_HEAD_DIM].set(
        jnp.stack([lam_q1[l], lam_k1[l], lam_q2[l], lam_k2[l]]).astype(F32))
    return dict(
        w_in=w_in_wide, g_q=g_q[l][None], g_kv=g_kv[l][None], wqa=wqa, wqb=wqb, wk=wk, wvt=wvt, wdvt=wdvt,
        lamv=lamv, g_sub=g_sub[l][None], woa=woa, wob=wob, ln1_g=ln1_g[l][None], ln1_b=ln1_b[l][None],
        w_gate=w_gate[l].astype(BF16), w_up=w_up[l].astype(BF16), w_down=w_down[l].astype(BF16),
        ln2_g=ln2_g[l][None], ln2_b=ln2_b[l][None])


def _rope_table(seq):
    inv = 1.0 / (ROPE_THETA ** (jnp.arange(0, MLA_ROPE_DIM, 2, dtype=F32) / MLA_ROPE_DIM))
    ang = jnp.arange(seq, dtype=F32)[:, None] * inv[None, :]
    cos, sin = jnp.cos(ang), jnp.sin(ang)
    scale = LOG2E * (MLA_NOPE_DIM + MLA_ROPE_DIM) ** -0.5
    ones = jnp.ones((seq, MLA_NOPE_DIM), F32)
    zl = jnp.zeros((seq, MLA_NOPE_DIM), F32)
    zr = jnp.zeros((seq, HEAD_W - MLA_NOPE_DIM - MLA_ROPE_DIM), F32)
    ct_q = jnp.concatenate([ones, cos, cos, zr], axis=1) * scale
    st_q = jnp.concatenate([zl, sin, sin, zr], axis=1) * scale
    ct_k = jnp.concatenate([zl, cos, cos, zr], axis=1)
    st_k = jnp.concatenate([zl, sin, sin, zr], axis=1)
    return jnp.concatenate([ct_q, st_q, ct_k, st_k], axis=1)


def _pick(n, pref):
    while n % pref:
        pref //= 2
    return pref


def _trunk(x, ln0_g, ln0_b, layers, wrt, br):
    batch, seq, _ = x.shape
    t = batch * seq
    tm_proj = _pick(seq, 256)
    tm_tok = _pick(t, 1024)
    tm_moe = _pick(t, 512)
    tq = _pick(seq, 256)
    tk = _pick(seq // 2, 512)
    nq = _pick(seq // tq, 8)
    nq_diff = _pick(seq // tq, 4 if seq // tk >= 8 else 2)
    tab = _rope_table(seq)
    x = _ln_call(x.reshape(t, D_MODEL), ln0_g[None], ln0_b[None], tm_tok)
    for l, lw in enumerate(layers):
        lam_init = 0.8 - 0.6 * math.exp(-0.3 * l)
        qm, km, vt, d, dvt = _proj_call(x, tab, lw, seq, tm_proj)
        om = _mla_call(qm, km, vt, batch, seq, tq, tk, nq)
        od = _diff_call(d, dvt, lw["lamv"], lw["g_sub"], lam_init, batch, seq, tq, tk, nq_diff)
        xg, cnt = _outproj_call(om, od, x, lw, wrt, br, tm_tok)
        x = _moe_layer(xg, cnt, lw, tm_moe, tm_tok)
    return x.reshape(batch, seq, D_MODEL)


def kernel(x_prompt, x_sample, ln0_g, ln0_b, w_in, g_q, g_kv, w_uq, w_ukv, lam_q1, lam_k1, lam_q2, lam_k2,
           g_sub, w_o, ln1_g, ln1_b, w_router, b_router, w_gate, w_up, w_down, ln2_g, ln2_b):
    layers = [_prep_layer(l, w_in, g_q, g_kv, w_uq, w_ukv, lam_q1, lam_k1, lam_q2, lam_k2, g_sub, w_o,
                          ln1_g, ln1_b, w_gate, w_up, w_down, ln2_g, ln2_b) for l in range(DEPTH)]
    perm = jnp.arange(N_EXPERTS).reshape(N_GROUPS, EXPERTS_PER_GROUP).T.reshape(-1)
    wrt = w_router.T[perm].astype(F32)
    br = b_router[perm].astype(F32)[:, None]
    y_prompt = _trunk(x_prompt, ln0_g, ln0_b, layers, wrt, br)
    y_sample = _trunk(x_sample, ln0_g, ln0_b, layers, wrt, br)
    return (y_prompt, y_sample)
```

```python
import functools
import math

import jax
import jax.numpy as jnp
from jax import lax
from jax.experimental import pallas as pl
from jax.experimental.pallas import tpu as pltpu

F32 = jnp.float32
BF16 = jnp.bfloat16

D_MODEL = 1024
DEPTH = 4
MLA_HEADS = 8
MLA_NOPE_DIM = 64
MLA_ROPE_DIM = 32
MLA_V_DIM = 64
Q_LORA_RANK = 256
KV_LORA_RANK = 128
ROPE_THETA = 10000.0
DIFF_HEADS = 4
DIFF_HEAD_DIM = 64
MLA_WIDTH = MLA_HEADS * MLA_V_DIM
DIFF_WIDTH = DIFF_HEADS * 2 * DIFF_HEAD_DIM
N_EXPERTS = 16
N_GROUPS = 4
EXPERTS_PER_GROUP = N_EXPERTS // N_GROUPS
D_FF_EXPERT = 512
LN_EPS = 1e-5
RMS_EPS = 1e-6
ALPHA = (2 * DEPTH) ** 0.25

LANES = 128
HEAD_W = LANES
MLA_SLAB = MLA_HEADS * HEAD_W
ROPE_HALF = MLA_ROPE_DIM // 2
C_Q = 0
C_KV = C_Q + Q_LORA_RANK
C_KR_A = C_KV + KV_LORA_RANK
C_KR_B = C_KR_A + HEAD_W
C_D = C_KR_B + HEAD_W
IN_W = C_D + 2 * DIFF_WIDTH
BF16_SUBLANES = 16
MLA_VT_ROWS = MLA_V_DIM + BF16_SUBLANES
DIFF_VT_ROWS = 2 * DIFF_HEAD_DIM + BF16_SUBLANES
LOG2E = math.log2(math.e)
XG_W = D_MODEL + LANES
ROUTE_GROUP_LANE = N_EXPERTS
ROUTE_RANK_LANE = N_EXPERTS + 1

VMEM_LIMIT_BYTES = 48 * 1024 * 1024


def _cparams(*sem):
    return pltpu.CompilerParams(dimension_semantics=sem, vmem_limit_bytes=VMEM_LIMIT_BYTES)


def _layer_norm(x, g, b):
    mu = jnp.mean(x, axis=-1, keepdims=True)
    xc = x - mu
    var = jnp.mean(xc * xc, axis=-1, keepdims=True)
    return xc * lax.rsqrt(var + LN_EPS) * g + b


def _rms(x, g, eps):
    return x * lax.rsqrt(jnp.mean(x * x, axis=-1, keepdims=True) + eps) * g


def _ln_kernel(x_ref, g_ref, b_ref, o_ref):
    o_ref[...] = _layer_norm(x_ref[...], g_ref[...], b_ref[...])


def _ln_call(x, g, b, tm):
    t = x.shape[0]
    return pl.pallas_call(
        _ln_kernel,
        grid=(t // tm,),
        in_specs=[pl.BlockSpec((tm, D_MODEL), lambda i: (i, 0)),
                  pl.BlockSpec((1, D_MODEL), lambda i: (0, 0)),
                  pl.BlockSpec((1, D_MODEL), lambda i: (0, 0))],
        out_specs=pl.BlockSpec((tm, D_MODEL), lambda i: (i, 0)),
        out_shape=jax.ShapeDtypeStruct((t, D_MODEL), F32),
        compiler_params=_cparams("parallel"),
        name="ln0",
    )(x, g, b)


def _ones_rows(n):
    row = lax.broadcasted_iota(jnp.int32, (BF16_SUBLANES, n), 0)
    return jnp.where(row == 0, 1.0, 0.0).astype(BF16)


def _nt_dot(a, b):
    return lax.dot_general(a, b, (((1,), (1,)), ((), ())), preferred_element_type=F32)


def _proj_kernel(x_ref, tab_ref, win_ref, gq_ref, gkv_ref, wqa_ref, wqb_ref, wk_ref, wvt_ref, wdvt_ref,
                 qm_ref, km_ref, vt_ref, d_ref, dvt_ref):
    tm = x_ref.shape[0]
    xb = x_ref[...].astype(BF16)
    h = jnp.dot(xb, win_ref[...], preferred_element_type=F32)
    ct_q = tab_ref[:, 0 * HEAD_W:1 * HEAD_W]
    st_q = tab_ref[:, 1 * HEAD_W:2 * HEAD_W]
    ct_k = tab_ref[:, 2 * HEAD_W:3 * HEAD_W]
    st_k = tab_ref[:, 3 * HEAD_W:4 * HEAD_W]

    cq = _rms(h[:, C_Q:C_KV], gq_ref[...], RMS_EPS).astype(BF16)
    qa = jnp.dot(cq, wqa_ref[...], preferred_element_type=F32)
    qb = jnp.dot(cq, wqb_ref[...], preferred_element_type=F32)
    ckv = _rms(h[:, C_KV:C_KR_A], gkv_ref[...], RMS_EPS).astype(BF16)
    kn = jnp.dot(ckv, wk_ref[...], preferred_element_type=F32)
    kr = h[:, C_KR_A:C_KR_B] * ct_k + h[:, C_KR_B:C_D] * st_k
    ones = _ones_rows(tm)
    vt = _nt_dot(wvt_ref[...], ckv).astype(BF16)
    for hd in range(MLA_HEADS):
        sl = slice(hd * HEAD_W, (hd + 1) * HEAD_W)
        qm_ref[:, sl] = (qa[:, sl] * ct_q + qb[:, sl] * st_q).astype(BF16)
        km_ref[:, sl] = (kn[:, sl] + kr).astype(BF16)
        r0 = hd * MLA_VT_ROWS
        vt_ref[r0:r0 + MLA_V_DIM, :] = vt[hd * MLA_V_DIM:(hd + 1) * MLA_V_DIM, :]
        vt_ref[r0 + MLA_V_DIM:r0 + MLA_VT_ROWS, :] = ones
    d_ref[:, :DIFF_WIDTH] = (h[:, C_D:C_D + DIFF_WIDTH] * (LOG2E * DIFF_HEAD_DIM ** -0.5)).astype(BF16)
    d_ref[:, DIFF_WIDTH:] = h[:, C_D + DIFF_WIDTH:].astype(BF16)
    dvt = _nt_dot(wdvt_ref[...], xb).astype(BF16)
    for hd in range(DIFF_HEADS):
        r0 = hd * DIFF_VT_ROWS
        dvt_ref[r0:r0 + HEAD_W, :] = dvt[hd * HEAD_W:(hd + 1) * HEAD_W, :]
        dvt_ref[r0 + HEAD_W:r0 + DIFF_VT_ROWS, :] = ones


def _proj_call(x, tab, lw, seq, tm):
    t = x.shape[0]
    batch = t // seq
    nblk = seq // tm
    const = lambda i: (0, 0)
    row = lambda i: (i, 0)
    tcol = lambda i: (i // nblk, 0, i % nblk)
    return pl.pallas_call(
        _proj_kernel,
        grid=(t // tm,),
        in_specs=[pl.BlockSpec((tm, D_MODEL), row),
                  pl.BlockSpec((tm, 4 * HEAD_W), lambda i: (i % nblk, 0)),
                  pl.BlockSpec((D_MODEL, IN_W), const),
                  pl.BlockSpec((1, Q_LORA_RANK), const),
                  pl.BlockSpec((1, KV_LORA_RANK), const),
                  pl.BlockSpec((Q_LORA_RANK, MLA_SLAB), const),
                  pl.BlockSpec((Q_LORA_RANK, MLA_SLAB), const),
                  pl.BlockSpec((KV_LORA_RANK, MLA_SLAB), const),
                  pl.BlockSpec((MLA_WIDTH, KV_LORA_RANK), const),
                  pl.BlockSpec((DIFF_WIDTH, D_MODEL), const)],
        out_specs=[pl.BlockSpec((tm, MLA_SLAB), row),
                   pl.BlockSpec((tm, MLA_SLAB), row),
                   pl.BlockSpec((None, MLA_HEADS * MLA_VT_ROWS, tm), tcol),
                   pl.BlockSpec((tm, 2 * DIFF_WIDTH), row),
                   pl.BlockSpec((None, DIFF_HEADS * DIFF_VT_ROWS, tm), tcol)],
        out_shape=[jax.ShapeDtypeStruct((t, MLA_SLAB), BF16),
                   jax.ShapeDtypeStruct((t, MLA_SLAB), BF16),
                   jax.ShapeDtypeStruct((batch, MLA_HEADS * MLA_VT_ROWS, seq), BF16),
                   jax.ShapeDtypeStruct((t, 2 * DIFF_WIDTH), BF16),
                   jax.ShapeDtypeStruct((batch, DIFF_HEADS * DIFF_VT_ROWS, seq), BF16)],
        compiler_params=_cparams("parallel"),
        name="proj",
    )(x, tab, lw["w_in"], lw["g_q"], lw["g_kv"], lw["wqa"], lw["wqb"], lw["wk"], lw["wvt"], lw["wdvt"])


def _flash_loop(score_fn, vt_ref, s_ref, rows, unroll):
    _, nstream, tk, tq = s_ref.shape
    nk = vt_ref.shape[1] // tk

    def key_offset(j):
        return j * tk if isinstance(j, int) else pl.multiple_of(j * tk, tk)

    def produce_one(fn, slot, i):
        st = fn()
        s_ref[slot, i] = st
        return jnp.max(st, axis=0, keepdims=True)

    def consume_one(vt, slot, i, mx_i, m, acc):
        m_new = jnp.maximum(m, mx_i)
        p = jnp.exp2(s_ref[slot, i] - m_new).astype(BF16)
        return m_new, jnp.exp2(m - m_new) * acc + jnp.dot(vt, p, preferred_element_type=F32)

    def produce(j, slot):
        return tuple(produce_one(fn, slot, i) for i, fn in enumerate(score_fn(key_offset(j))))

    def consume(j, slot, mx, carry):
        vt = vt_ref[:, pl.ds(key_offset(j), tk)]
        return tuple(consume_one(vt, slot, i, mx[i], m, acc) for i, (m, acc) in enumerate(carry))

    def step(jp, slot_p, jc, slot_c, mx_c, carry):
        fns = score_fn(key_offset(jp))
        vt = vt_ref[:, pl.ds(key_offset(jc), tk)]
        mx_p, out = [], []
        for i, (m, acc) in enumerate(carry):
            mx_p.append(produce_one(fns[i], slot_p, i))
            out.append(consume_one(vt, slot_c, i, mx_c[i], m, acc))
        return tuple(mx_p), tuple(out)

    carry = tuple((jnp.full((1, tq), -jnp.inf, F32), jnp.zeros((rows, tq), F32)) for _ in range(nstream))
    mx = produce(0, 0)
    if unroll:
        for j in range(nk - 2):
            mx, carry = step(j + 1, (j + 1) % 2, j, j % 2, mx, carry)
    else:
        def pair(jj, state):
            mx_even, carry = state
            j = 2 * jj
            mx_odd, carry = step(j + 1, 1, j, 0, mx_even, carry)
            return step(j + 2, 0, j + 1, 1, mx_odd, carry)

        mx, carry = lax.fori_loop(0, nk // 2 - 1, pair, (mx, carry))
    mx, carry = step(nk - 1, 1, nk - 2, 0, mx, carry)
    carry = consume(nk - 1, 1, mx, carry)
    return [acc for _, acc in carry]


def _finish(acc, rows):
    return acc[:rows, :] * (1.0 / acc[rows:rows + 1, :])


def _mla_kernel(q_ref, k_ref, vt_ref, o_ref, s_ref):
    _, nq, tk, tq = s_ref.shape
    qs = [q_ref[i * tq:(i + 1) * tq, :] for i in range(nq)]

    def scores(off):
        ks = k_ref[pl.ds(off, tk), :]
        return [functools.partial(_nt_dot, ks, q) for q in qs]

    accs = _flash_loop(scores, vt_ref, s_ref, MLA_VT_ROWS, unroll=True)
    pad = jnp.zeros((HEAD_W - MLA_V_DIM, tq), F32)
    for i, acc in enumerate(accs):
        ot = jnp.concatenate([_finish(acc, MLA_V_DIM), pad], axis=0)
        o_ref[i * tq:(i + 1) * tq, :] = ot.T.astype(BF16)


def _mla_call(qm, km, vt, batch, seq, tq, tk, nq):
    qm, km = (a.reshape(batch, seq, MLA_SLAB) for a in (qm, km))
    bq = tq * nq
    out = pl.pallas_call(
        _mla_kernel,
        grid=(batch, MLA_HEADS, seq // bq),
        scratch_shapes=[pltpu.VMEM((2, nq, tk, tq), F32)],
        in_specs=[pl.BlockSpec((None, bq, HEAD_W), lambda b, h, i: (b, i, h)),
                  pl.BlockSpec((None, seq, HEAD_W), lambda b, h, i: (b, 0, h)),
                  pl.BlockSpec((None, MLA_VT_ROWS, seq), lambda b, h, i: (b, h, 0))],
        out_specs=pl.BlockSpec((None, bq, HEAD_W), lambda b, h, i: (b, i, h)),
        out_shape=jax.ShapeDtypeStruct((batch, seq, MLA_SLAB), BF16),
        compiler_params=_cparams("parallel", "parallel", "parallel"),
        name="mla",
    )(qm, km, vt)
    return out.reshape(batch * seq, MLA_SLAB)


def _diff_kernel(lam_ref, gsub_ref, q_ref, k_ref, vt_ref, o_ref, s_ref, *, lam_init):
    _, _, tk, tq = s_ref.shape
    hd = pl.program_id(1)
    q0 = pl.program_id(2) * q_ref.shape[0]
    slope = LOG2E * jnp.exp2(-2.0 * (hd + 1).astype(F32) * jnp.ones((1, 1), F32))
    lane = lax.broadcasted_iota(jnp.int32, (1, HEAD_W), 1)
    nq = q_ref.shape[0] // tq
    qs = []
    for i in range(nq):
        q = q_ref[i * tq:(i + 1) * tq, :]
        qs.append((jnp.where(lane < DIFF_HEAD_DIM, q, jnp.zeros_like(q)),
                   jnp.where(lane >= DIFF_HEAD_DIM, q, jnp.zeros_like(q))))
    rel = (lax.broadcasted_iota(jnp.int32, (tk, tq), 0) - lax.broadcasted_iota(jnp.int32, (tk, tq), 1)).astype(F32)

    def scores(off):
        ks = k_ref[pl.ds(off, tk), :]
        out = []
        for i, (q1, q2) in enumerate(qs):
            bias = jnp.abs(rel + (off - q0 - i * tq).astype(F32)) * (-slope)
            out += [lambda q=q1, b=bias: _nt_dot(ks, q) + b, lambda q=q2, b=bias: _nt_dot(ks, q) + b]
        return out

    accs = _flash_loop(scores, vt_ref, s_ref, DIFF_VT_ROWS, unroll=False)

    lam = (jnp.exp(jnp.sum(lam_ref[0:1, :] * lam_ref[1:2, :], axis=-1, keepdims=True))
           - jnp.exp(jnp.sum(lam_ref[2:3, :] * lam_ref[3:4, :], axis=-1, keepdims=True)) + lam_init)
    for i in range(nq):
        o = (_finish(accs[2 * i], HEAD_W) - lam * _finish(accs[2 * i + 1], HEAD_W)).T
        o_ref[i * tq:(i + 1) * tq, :] = (_rms(o, gsub_ref[...], LN_EPS) * (1.0 - lam_init)).astype(BF16)


def _diff_call(d, dvt, lamv, g_sub, lam_init, batch, seq, tq, tk, nq):
    d = d.reshape(batch, seq, 2 * DIFF_WIDTH)
    bq = tq * nq
    out = pl.pallas_call(
        functools.partial(_diff_kernel, lam_init=lam_init),
        grid=(batch, DIFF_HEADS, seq // bq),
        scratch_shapes=[pltpu.VMEM((2, 2 * nq, tk, tq), F32)],
        in_specs=[pl.BlockSpec((8, LANES), lambda b, h, i: (0, 0)),
                  pl.BlockSpec((1, HEAD_W), lambda b, h, i: (0, 0)),
                  pl.BlockSpec((None, bq, HEAD_W), lambda b, h, i: (b, i, h)),
                  pl.BlockSpec((None, seq, HEAD_W), lambda b, h, i: (b, 0, DIFF_HEADS + h)),
                  pl.BlockSpec((None, DIFF_VT_ROWS, seq), lambda b, h, i: (b, h, 0))],
        out_specs=pl.BlockSpec((None, bq, HEAD_W), lambda b, h, i: (b, i, h)),
        out_shape=jax.ShapeDtypeStruct((batch, seq, DIFF_WIDTH), BF16),
        compiler_params=_cparams("parallel", "parallel", "parallel"),
        name="diff",
    )(lamv, g_sub, d, d, dvt)
    return out.reshape(batch * seq, DIFF_WIDTH)


def _first_max(vals):
    mx = functools.reduce(jnp.maximum, vals)
    taken = None
    masks = []
    for v in vals:
        hit = v == mx
        if taken is None:
            masks.append(hit)
            taken = hit
        else:
            masks.append(jnp.logical_and(hit, jnp.logical_not(taken)))
            taken = jnp.logical_or(taken, hit)
    return masks, mx


def _outproj_kernel(om_ref, od_ref, x_ref, woa_ref, wob_ref, g_ref, b_ref, wrt_ref, br_ref,
                    xg_ref, cnt_ref, upper_ref):
    tm = x_ref.shape[0]

    @pl.when(pl.program_id(0) == 0)
    def _():
        cnt_ref[...] = jnp.zeros_like(cnt_ref)
        upper_ref[...] = jnp.where(
            lax.broadcasted_iota(jnp.int32, (tm, tm), 0) <= lax.broadcasted_iota(jnp.int32, (tm, tm), 1),
            1.0, 0.0).astype(BF16)

    mix = (jnp.dot(om_ref[...], woa_ref[...], preferred_element_type=F32)
           + jnp.dot(od_ref[...], wob_ref[...], preferred_element_type=F32))
    x1 = _layer_norm(ALPHA * x_ref[...] + mix, g_ref[...], b_ref[...])
    xg_ref[:, :D_MODEL] = x1

    logits = lax.dot_general(wrt_ref[...], x1, (((1,), (1,)), ((), ())),
                             precision=lax.Precision.HIGHEST, preferred_element_type=F32)
    scores = jax.nn.sigmoid(logits)
    biased = scores + br_ref[...]
    member = [biased[k * N_GROUPS:(k + 1) * N_GROUPS, :] for k in range(EXPERTS_PER_GROUP)]
    score_m = [scores[k * N_GROUPS:(k + 1) * N_GROUPS, :] for k in range(EXPERTS_PER_GROUP)]
    first, top1 = _first_max(member)
    rest = [jnp.where(f, -jnp.inf, v) for f, v in zip(first, member)]
    second, top2 = _first_max(rest)
    grp_score = top1 + top2
    grp_rows = [grp_score[g:g + 1, :] for g in range(N_GROUPS)]
    grp_sel, _ = _first_max(grp_rows)
    in_grp_f = jnp.concatenate([jnp.where(s, 1.0, 0.0) for s in grp_sel], axis=0)
    in_grp = in_grp_f > 0.5
    picked = [jnp.logical_and(in_grp, jnp.logical_or(f, s)) for f, s in zip(first, second)]
    w = [jnp.where(p, sc, 0.0) for p, sc in zip(picked, score_m)]
    denom = jnp.sum(functools.reduce(jnp.add, w), axis=0, keepdims=True)
    gates = jnp.concatenate(w, axis=0) / denom

    nrow = cnt_ref.shape[0]
    grp_rows_f = jnp.concatenate([in_grp_f, jnp.zeros((BF16_SUBLANES - N_GROUPS, tm), F32)], axis=0)
    incl = jnp.dot(grp_rows_f.astype(BF16), upper_ref[...], preferred_element_type=F32)[:nrow]
    onehot = grp_rows_f[:nrow]
    rank = jnp.sum(onehot * (incl - 1.0 + cnt_ref[:, 0:1]), axis=0, keepdims=True)
    row = lax.broadcasted_iota(jnp.int32, (nrow, tm), 0)
    gid = jnp.sum(onehot * row.astype(F32), axis=0, keepdims=True)
    cnt_ref[...] = cnt_ref[...] + incl[:, tm - 1:tm]
    meta = jnp.where(row == 0, gid, jnp.where(row == 1, rank, 0.0))
    slab_t = jnp.concatenate([gates, meta, jnp.zeros((LANES - N_EXPERTS - nrow, tm), F32)], axis=0)
    xg_ref[:, D_MODEL:] = slab_t.T


def _outproj_call(om, od, x, lw, wrt, br, tm):
    t = x.shape[0]
    const = lambda i: (0, 0)
    row = lambda i: (i, 0)
    return pl.pallas_call(
        _outproj_kernel,
        grid=(t // tm,),
        in_specs=[pl.BlockSpec((tm, MLA_SLAB), row),
                  pl.BlockSpec((tm, DIFF_WIDTH), row),
                  pl.BlockSpec((tm, D_MODEL), row),
                  pl.BlockSpec((MLA_SLAB, D_MODEL), const),
                  pl.BlockSpec((DIFF_WIDTH, D_MODEL), const),
                  pl.BlockSpec((1, D_MODEL), const),
                  pl.BlockSpec((1, D_MODEL), const),
                  pl.BlockSpec((N_EXPERTS, D_MODEL), const),
                  pl.BlockSpec((N_EXPERTS, 1), const)],
        out_specs=[pl.BlockSpec((tm, XG_W), row),
                   pl.BlockSpec((8, LANES), const)],
        out_shape=[jax.ShapeDtypeStruct((t, XG_W), F32),
                   jax.ShapeDtypeStruct((8, LANES), F32)],
        scratch_shapes=[pltpu.VMEM((tm, tm), BF16)],
        compiler_params=_cparams("arbitrary"),
        name="outproj",
    )(om, od, x, lw["woa"], lw["wob"], lw["ln1_g"], lw["ln1_b"], wrt, br)


ROW_DMA_UNROLL = 8


def _row_dma_all(row_copy, n):
    def start(r, c):
        row_copy(r).start()
        return c

    def wait(r, c):
        row_copy(r).wait()
        return c

    lax.fori_loop(0, n, start, 0, unroll=ROW_DMA_UNROLL)
    lax.fori_loop(0, n, wait, 0, unroll=ROW_DMA_UNROLL)


def _scatter_kernel(pos_ref, xg_ref, init_ref, xs_ref, sem):
    del init_ref
    tm = xg_ref.shape[0]
    base = pl.program_id(0) * tm
    _row_dma_all(lambda r: pltpu.make_async_copy(xg_ref.at[pl.ds(r, 1), :],
                                                 xs_ref.at[pl.ds(pos_ref[base + r], 1), :], sem), tm)


def _scatter_call(pos, xg, rows, tm):
    t = xg.shape[0]
    return pl.pallas_call(
        _scatter_kernel,
        grid_spec=pltpu.PrefetchScalarGridSpec(
            num_scalar_prefetch=1,
            grid=(t // tm,),
            in_specs=[pl.BlockSpec((tm, XG_W), lambda i, pos: (i, 0)),
                      pl.BlockSpec(memory_space=pl.ANY)],
            out_specs=pl.BlockSpec(memory_space=pl.ANY),
            scratch_shapes=[pltpu.SemaphoreType.DMA]),
        out_shape=jax.ShapeDtypeStruct((rows, XG_W), F32),
        input_output_aliases={2: 0},
        compiler_params=_cparams("arbitrary"),
        name="moe_scatter",
    )(pos, xg, jnp.zeros((rows, XG_W), F32))


def _moe_kernel(tg_ref, nused_ref, xs_ref, wg_ref, wu_ref, wd_ref, ys_ref):
    del tg_ref

    @pl.when(pl.program_id(0) < nused_ref[0])
    def _():
        xb = xs_ref[:, :D_MODEL].astype(BF16)
        lane = lax.broadcasted_iota(jnp.int32, (1, LANES), 1)
        y = None
        for k in range(EXPERTS_PER_GROUP):
            hg = jnp.dot(xb, wg_ref[k], preferred_element_type=F32)
            hu = jnp.dot(xb, wu_ref[k], preferred_element_type=F32)
            mine = jnp.logical_and(lane >= k * N_GROUPS, lane < (k + 1) * N_GROUPS)
            gate = jnp.sum(jnp.where(mine, xs_ref[:, D_MODEL:], 0.0), axis=-1, keepdims=True)
            hh = (hg * jax.nn.sigmoid(hg) * hu * gate).astype(BF16)
            d = jnp.dot(hh, wd_ref[k], preferred_element_type=F32)
            y = d if y is None else y + d
        ys_ref[...] = y

    @pl.when(pl.program_id(0) >= nused_ref[0])
    def _():
        ys_ref[...] = jnp.zeros_like(ys_ref)


def _moe_call(tile_group, nused, xs, lw, tm):
    rows = xs.shape[0]
    wmap = lambda i, tg, nu: (tg[i], 0, 0, 0)
    wg, wu, wd = (lw[n].reshape((N_GROUPS, EXPERTS_PER_GROUP) + lw[n].shape[1:])
                  for n in ("w_gate", "w_up", "w_down"))
    return pl.pallas_call(
        _moe_kernel,
        grid_spec=pltpu.PrefetchScalarGridSpec(
            num_scalar_prefetch=2,
            grid=(rows // tm,),
            in_specs=[pl.BlockSpec((tm, XG_W), lambda i, tg, nu: (i, 0)),
                      pl.BlockSpec((None, EXPERTS_PER_GROUP, D_MODEL, D_FF_EXPERT), wmap),
                      pl.BlockSpec((None, EXPERTS_PER_GROUP, D_MODEL, D_FF_EXPERT), wmap),
                      pl.BlockSpec((None, EXPERTS_PER_GROUP, D_FF_EXPERT, D_MODEL), wmap)],
            out_specs=pl.BlockSpec((tm, D_MODEL), lambda i, tg, nu: (i, 0))),
        out_shape=jax.ShapeDtypeStruct((rows, D_MODEL), F32),
        compiler_params=_cparams("parallel"),
        name="moe",
    )(tile_group, nused, xs, wg, wu, wd)


def _unsort_ln_kernel(pos_ref, x1_ref, ys_ref, g_ref, b_ref, o_ref, buf_ref, sem):
    tm = x1_ref.shape[0]
    base = pl.program_id(0) * tm
    _row_dma_all(lambda r: pltpu.make_async_copy(ys_ref.at[pl.ds(pos_ref[base + r], 1), :],
                                                 buf_ref.at[pl.ds(r, 1), :], sem), tm)
    o_ref[...] = _layer_norm(ALPHA * x1_ref[...] + buf_ref[...], g_ref[...], b_ref[...])


def _unsort_ln_call(pos, xg, ys, lw, tm):
    t = xg.shape[0]
    return pl.pallas_call(
        _unsort_ln_kernel,
        grid_spec=pltpu.PrefetchScalarGridSpec(
            num_scalar_prefetch=1,
            grid=(t // tm,),
            in_specs=[pl.BlockSpec((tm, D_MODEL), lambda i, pos: (i, 0)),
                      pl.BlockSpec(memory_space=pl.ANY),
                      pl.BlockSpec((1, D_MODEL), lambda i, pos: (0, 0)),
                      pl.BlockSpec((1, D_MODEL), lambda i, pos: (0, 0))],
            out_specs=pl.BlockSpec((tm, D_MODEL), lambda i, pos: (i, 0)),
            scratch_shapes=[pltpu.VMEM((tm, D_MODEL), F32), pltpu.SemaphoreType.DMA]),
        out_shape=jax.ShapeDtypeStruct((t, D_MODEL), F32),
        compiler_params=_cparams("arbitrary"),
        name="moe_unsort_ln",
    )(pos, xg, ys, lw["ln2_g"], lw["ln2_b"])


def _moe_layer(xg, cnt, lw, tm_sorted, tm_tok):
    t = xg.shape[0]
    counts = cnt[:N_GROUPS, 0].astype(jnp.int32)
    padded = (counts + tm_sorted - 1) // tm_sorted * tm_sorted
    ends = jnp.cumsum(padded)
    starts = ends - padded
    grp = xg[:, D_MODEL + ROUTE_GROUP_LANE].astype(jnp.int32)
    rank = xg[:, D_MODEL + ROUTE_RANK_LANE].astype(jnp.int32)
    pos = rank + jnp.sum(jnp.where(grp[:, None] == jnp.arange(N_GROUPS)[None, :], starts[None, :], 0), axis=1)
    ntiles = t // tm_sorted + N_GROUPS
    tile_start = jnp.arange(ntiles, dtype=jnp.int32) * tm_sorted
    tile_group = jnp.minimum(jnp.sum((tile_start[:, None] >= ends[None, :]).astype(jnp.int32), axis=1), N_GROUPS - 1)
    nused = (ends[-1:] // tm_sorted).astype(jnp.int32)
    xs = _scatter_call(pos, xg, ntiles * tm_sorted, tm_tok)
    ys = _moe_call(tile_group, nused, xs, lw, tm_sorted)
    return _unsort_ln_call(pos, xg, ys, lw, tm_tok)


def _rotate_half_cols(w):
    return jnp.concatenate([-w[..., ROPE_HALF:], w[..., :ROPE_HALF]], axis=-1)


def _prep_layer(l, w_in, g_q, g_kv, w_uq, w_ukv, lam_q1, lam_k1, lam_q2, lam_k2, g_sub, w_o,
                ln1_g, ln1_b, w_gate, w_up, w_down, ln2_g, ln2_b):
    d = D_MODEL
    z = lambda *s: jnp.zeros(s, F32)
    wi = w_in[l]
    c_kr = Q_LORA_RANK + KV_LORA_RANK
    c_dq = c_kr + MLA_ROPE_DIM
    w_kr = wi[:, c_kr:c_dq]
    pad_l, pad_r = z(d, MLA_NOPE_DIM), z(d, HEAD_W - MLA_NOPE_DIM - MLA_ROPE_DIM)
    w_in_wide = jnp.concatenate([
        wi[:, :c_kr],
        pad_l, w_kr, pad_r,
        pad_l, _rotate_half_cols(w_kr), pad_r,
        wi[:, c_dq:c_dq + 2 * DIFF_WIDTH]], axis=1).astype(BF16)
    wdvt = wi[:, c_dq + 2 * DIFF_WIDTH:].T.astype(BF16)

    r = Q_LORA_RANK
    wq = w_uq[l].reshape(r, MLA_HEADS, MLA_NOPE_DIM + MLA_ROPE_DIM)
    wq_rope = wq[..., MLA_NOPE_DIM:]
    tail = z(r, MLA_HEADS, HEAD_W - MLA_NOPE_DIM - MLA_ROPE_DIM)
    wqa = jnp.concatenate([wq, tail], axis=-1).reshape(r, MLA_SLAB).astype(BF16)
    wqb = jnp.concatenate([z(r, MLA_HEADS, MLA_NOPE_DIM), _rotate_half_cols(wq_rope), tail],
                          axis=-1).reshape(r, MLA_SLAB).astype(BF16)
    r = KV_LORA_RANK
    wkv = w_ukv[l].reshape(r, MLA_HEADS, MLA_NOPE_DIM + MLA_V_DIM)
    wk = jnp.concatenate([wkv[..., :MLA_NOPE_DIM], z(r, MLA_HEADS, HEAD_W - MLA_NOPE_DIM)],
                         axis=-1).reshape(r, MLA_SLAB).astype(BF16)
    wvt = wkv[..., MLA_NOPE_DIM:].reshape(r, MLA_WIDTH).T.astype(BF16)

    wo_m = w_o[l][:MLA_WIDTH].reshape(MLA_HEADS, MLA_V_DIM, d)
    woa = jnp.concatenate([wo_m, z(MLA_HEADS, HEAD_W - MLA_V_DIM, d)], axis=1).reshape(MLA_SLAB, d).astype(BF16)
    wob = w_o[l][MLA_WIDTH:].astype(BF16)

    lamv = jnp.zeros((8, LANES), F32).at[:4, :DIFF_HEAD_DIM].set(
        jnp.stack([lam_q1[l], lam_k1[l], lam_q2[l], lam_k2[l]]).astype(F32))
    return dict(
        w_in=w_in_wide, g_q=g_q[l][None], g_kv=g_kv[l][None], wqa=wqa, wqb=wqb, wk=wk, wvt=wvt, wdvt=wdvt,
        lamv=lamv, g_sub=g_sub[l][None], woa=woa, wob=wob, ln1_g=ln1_g[l][None], ln1_b=ln1_b[l][None],
        w_gate=w_gate[l].astype(BF16), w_up=w_up[l].astype(BF16), w_down=w_down[l].astype(BF16),
        ln2_g=ln2_g[l][None], ln2_b=ln2_b[l][None])


def _rope_table(seq):
    inv = 1.0 / (ROPE_THETA ** (jnp.arange(0, MLA_ROPE_DIM, 2, dtype=F32) / MLA_ROPE_DIM))
    ang = jnp.arange(seq, dtype=F32)[:, None] * inv[None, :]
    cos, sin = jnp.cos(ang), jnp.sin(ang)
    scale = LOG2E * (MLA_NOPE_DIM + MLA_ROPE_DIM) ** -0.5
    ones = jnp.ones((seq, MLA_NOPE_DIM), F32)
    zl = jnp.zeros((seq, MLA_NOPE_DIM), F32)
    zr = jnp.zeros((seq, HEAD_W - MLA_NOPE_DIM - MLA_ROPE_DIM), F32)
    ct_q = jnp.concatenate([ones, cos, cos, zr], axis=1) * scale
    st_q = jnp.concatenate([zl, sin, sin, zr], axis=1) * scale
    ct_k = jnp.concatenate([zl, cos, cos, zr], axis=1)
    st_k = jnp.concatenate([zl, sin, sin, zr], axis=1)
    return jnp.concatenate([ct_q, st_q, ct_k, st_k], axis=1)


def _pick(n, pref):
    while n % pref:
        pref //= 2
    return pref


def _trunk(x, ln0_g, ln0_b, layers, wrt, br):
    batch, seq, _ = x.shape
    t = batch * seq
    tm_proj = _pick(seq, 256)
    tm_tok = _pick(t, 1024)
    tm_moe = _pick(t, 512)
    tq = _pick(seq, 256)
    tk = _pick(seq // 2, 512)
    nq = _pick(seq // tq, 8)
    nq_diff = _pick(seq // tq, 4 if seq // tk >= 8 else 2)
    tab = _rope_table(seq)
    x = _ln_call(x.reshape(t, D_MODEL), ln0_g[None], ln0_b[None], tm_tok)
    for l, lw in enumerate(layers):
        lam_init = 0.8 - 0.6 * math.exp(-0.3 * l)
        qm, km, vt, d, dvt = _proj_call(x, tab, lw, seq, tm_proj)
        om = _mla_call(qm, km, vt, batch, seq, tq, tk, nq)
        od = _diff_call(d, dvt, lw["lamv"], lw["g_sub"], lam_init, batch, seq, tq, tk, nq_diff)
        xg, cnt = _outproj_call(om, od, x, lw, wrt, br, tm_tok)
        x = _moe_layer(xg, cnt, lw, tm_moe, tm_tok)
    return x.reshape(batch, seq, D_MODEL)


def kernel(x_prompt, x_sample, ln0_g, ln0_b, w_in, g_q, g_kv, w_uq, w_ukv, lam_q1, lam_k1, lam_q2, lam_k2,
           g_sub, w_o, ln1_g, ln1_b, w_router, b_router, w_gate, w_up, w_down, ln2_g, ln2_b):
    layers = [_prep_layer(l, w_in, g_q, g_kv, w_uq, w_ukv, lam_q1, lam_k1, lam_q2, lam_k2, g_sub, w_o,
                          ln1_g, ln1_b, w_gate, w_up, w_down, ln2_g, ln2_b) for l in range(DEPTH)]
    perm = jnp.arange(N_EXPERTS).reshape(N_GROUPS, EXPERTS_PER_GROUP).T.reshape(-1)
    wrt = w_router.T[perm].astype(F32)
    br = b_router[perm].astype(F32)[:, None]
    y_prompt = _trunk(x_prompt, ln0_g, ln0_b, layers, wrt, br)
    y_sample = _trunk(x_sample, ln0_g, ln0_b, layers, wrt, br)
    return (y_prompt, y_sample)
```

```python
import functools
import math

import jax
import jax.numpy as jnp
from jax import lax
from jax.experimental import pallas as pl
from jax.experimental.pallas import tpu as pltpu

F32 = jnp.float32
BF16 = jnp.bfloat16

D_MODEL = 1024
DEPTH = 4
MLA_HEADS = 8
MLA_NOPE_DIM = 64
MLA_ROPE_DIM = 32
MLA_V_DIM = 64
Q_LORA_RANK = 256
KV_LORA_RANK = 128
ROPE_THETA = 10000.0
DIFF_HEADS = 4
DIFF_HEAD_DIM = 64
MLA_WIDTH = MLA_HEADS * MLA_V_DIM
DIFF_WIDTH = DIFF_HEADS * 2 * DIFF_HEAD_DIM
N_EXPERTS = 16
N_GROUPS = 4
EXPERTS_PER_GROUP = N_EXPERTS // N_GROUPS
D_FF_EXPERT = 512
LN_EPS = 1e-5
RMS_EPS = 1e-6
ALPHA = (2 * DEPTH) ** 0.25

LANES = 128
HEAD_W = LANES
MLA_SLAB = MLA_HEADS * HEAD_W
ROPE_HALF = MLA_ROPE_DIM // 2
C_Q = 0
C_KV = C_Q + Q_LORA_RANK
C_KR_A = C_KV + KV_LORA_RANK
C_KR_B = C_KR_A + HEAD_W
C_D = C_KR_B + HEAD_W
IN_W = C_D + 2 * DIFF_WIDTH
BF16_SUBLANES = 16
MLA_VT_ROWS = MLA_V_DIM + BF16_SUBLANES
DIFF_VT_ROWS = 2 * DIFF_HEAD_DIM + BF16_SUBLANES
LOG2E = math.log2(math.e)
XG_W = D_MODEL + LANES
ROUTE_GROUP_LANE = N_EXPERTS
ROUTE_RANK_LANE = N_EXPERTS + 1

VMEM_LIMIT_BYTES = 48 * 1024 * 1024


def _cparams(*sem):
    return pltpu.CompilerParams(dimension_semantics=sem, vmem_limit_bytes=VMEM_LIMIT_BYTES)


def _layer_norm(x, g, b):
    mu = jnp.mean(x, axis=-1, keepdims=True)
    xc = x - mu
    var = jnp.mean(xc * xc, axis=-1, keepdims=True)
    return xc * lax.rsqrt(var + LN_EPS) * g + b


def _rms(x, g, eps):
    return x * lax.rsqrt(jnp.mean(x * x, axis=-1, keepdims=True) + eps) * g


def _ln_kernel(x_ref, g_ref, b_ref, o_ref):
    o_ref[...] = _layer_norm(x_ref[...], g_ref[...], b_ref[...])


def _ln_call(x, g, b, tm):
    t = x.shape[0]
    return pl.pallas_call(
        _ln_kernel,
        grid=(t // tm,),
        in_specs=[pl.BlockSpec((tm, D_MODEL), lambda i: (i, 0)),
                  pl.BlockSpec((1, D_MODEL), lambda i: (0, 0)),
                  pl.BlockSpec((1, D_MODEL), lambda i: (0, 0))],
        out_specs=pl.BlockSpec((tm, D_MODEL), lambda i: (i, 0)),
        out_shape=jax.ShapeDtypeStruct((t, D_MODEL), F32),
        compiler_params=_cparams("parallel"),
        name="ln0",
    )(x, g, b)


def _ones_rows(n):
    row = lax.broadcasted_iota(jnp.int32, (BF16_SUBLANES, n), 0)
    return jnp.where(row == 0, 1.0, 0.0).astype(BF16)


def _nt_dot(a, b):
    return lax.dot_general(a, b, (((1,), (1,)), ((), ())), preferred_element_type=F32)


def _proj_kernel(x_ref, tab_ref, win_ref, gq_ref, gkv_ref, wqa_ref, wqb_ref, wk_ref, wvt_ref, wdvt_ref,
                 qm_ref, km_ref, vt_ref, d_ref, dvt_ref):
    tm = x_ref.shape[0]
    xb = x_ref[...].astype(BF16)
    h = jnp.dot(xb, win_ref[...], preferred_element_type=F32)
    ct_q = tab_ref[:, 0 * HEAD_W:1 * HEAD_W]
    st_q = tab_ref[:, 1 * HEAD_W:2 * HEAD_W]
    ct_k = tab_ref[:, 2 * HEAD_W:3 * HEAD_W]
    st_k = tab_ref[:, 3 * HEAD_W:4 * HEAD_W]

    cq = _rms(h[:, C_Q:C_KV], gq_ref[...], RMS_EPS).astype(BF16)
    qa = jnp.dot(cq, wqa_ref[...], preferred_element_type=F32)
    qb = jnp.dot(cq, wqb_ref[...], preferred_element_type=F32)
    ckv = _rms(h[:, C_KV:C_KR_A], gkv_ref[...], RMS_EPS).astype(BF16)
    kn = jnp.dot(ckv, wk_ref[...], preferred_element_type=F32)
    kr = h[:, C_KR_A:C_KR_B] * ct_k + h[:, C_KR_B:C_D] * st_k
    ones = _ones_rows(tm)
    vt = _nt_dot(wvt_ref[...], ckv).astype(BF16)
    for hd in range(MLA_HEADS):
        sl = slice(hd * HEAD_W, (hd + 1) * HEAD_W)
        qm_ref[:, sl] = (qa[:, sl] * ct_q + qb[:, sl] * st_q).astype(BF16)
        km_ref[:, sl] = (kn[:, sl] + kr).astype(BF16)
        r0 = hd * MLA_VT_ROWS
        vt_ref[r0:r0 + MLA_V_DIM, :] = vt[hd * MLA_V_DIM:(hd + 1) * MLA_V_DIM, :]
        vt_ref[r0 + MLA_V_DIM:r0 + MLA_VT_ROWS, :] = ones
    d_ref[:, :DIFF_WIDTH] = (h[:, C_D:C_D + DIFF_WIDTH] * (LOG2E * DIFF_HEAD_DIM ** -0.5)).astype(BF16)
    d_ref[:, DIFF_WIDTH:] = h[:, C_D + DIFF_WIDTH:].astype(BF16)
    dvt = _nt_dot(wdvt_ref[...], xb).astype(BF16)
    for hd in range(DIFF_HEADS):
        r0 = hd * DIFF_VT_ROWS
        dvt_ref[r0:r0 + HEAD_W, :] = dvt[hd * HEAD_W:(hd + 1) * HEAD_W, :]
        dvt_ref[r0 + HEAD_W:r0 + DIFF_VT_ROWS, :] = ones


def _proj_call(x, tab, lw, seq, tm):
    t = x.shape[0]
    batch = t // seq
    nblk = seq // tm
    const = lambda i: (0, 0)
    row = lambda i: (i, 0)
    tcol = lambda i: (i // nblk, 0, i % nblk)
    return pl.pallas_call(
        _proj_kernel,
        grid=(t // tm,),
        in_specs=[pl.BlockSpec((tm, D_MODEL), row),
                  pl.BlockSpec((tm, 4 * HEAD_W), lambda i: (i % nblk, 0)),
                  pl.BlockSpec((D_MODEL, IN_W), const),
                  pl.BlockSpec((1, Q_LORA_RANK), const),
                  pl.BlockSpec((1, KV_LORA_RANK), const),
                  pl.BlockSpec((Q_LORA_RANK, MLA_SLAB), const),
                  pl.BlockSpec((Q_LORA_RANK, MLA_SLAB), const),
                  pl.BlockSpec((KV_LORA_RANK, MLA_SLAB), const),
                  pl.BlockSpec((MLA_WIDTH, KV_LORA_RANK), const),
                  pl.BlockSpec((DIFF_WIDTH, D_MODEL), const)],
        out_specs=[pl.BlockSpec((tm, MLA_SLAB), row),
                   pl.BlockSpec((tm, MLA_SLAB), row),
                   pl.BlockSpec((None, MLA_HEADS * MLA_VT_ROWS, tm), tcol),
                   pl.BlockSpec((tm, 2 * DIFF_WIDTH), row),
                   pl.BlockSpec((None, DIFF_HEADS * DIFF_VT_ROWS, tm), tcol)],
        out_shape=[jax.ShapeDtypeStruct((t, MLA_SLAB), BF16),
                   jax.ShapeDtypeStruct((t, MLA_SLAB), BF16),
                   jax.ShapeDtypeStruct((batch, MLA_HEADS * MLA_VT_ROWS, seq), BF16),
                   jax.ShapeDtypeStruct((t, 2 * DIFF_WIDTH), BF16),
                   jax.ShapeDtypeStruct((batch, DIFF_HEADS * DIFF_VT_ROWS, seq), BF16)],
        compiler_params=_cparams("parallel"),
        name="proj",
    )(x, tab, lw["w_in"], lw["g_q"], lw["g_kv"], lw["wqa"], lw["wqb"], lw["wk"], lw["wvt"], lw["wdvt"])


def _flash_loop(score_fn, vt_ref, s_ref, rows, unroll):
    _, nstream, tk, tq = s_ref.shape
    nk = vt_ref.shape[1] // tk

    def key_offset(j):
        return j * tk if isinstance(j, int) else pl.multiple_of(j * tk, tk)

    def produce_one(fn, slot, i):
        st = fn()
        s_ref[slot, i] = st
        return jnp.max(st, axis=0, keepdims=True)

    def consume_one(vt, slot, i, mx_i, m, acc):
        m_new = jnp.maximum(m, mx_i)
        p = jnp.exp2(s_ref[slot, i] - m_new).astype(BF16)
        return m_new, jnp.exp2(m - m_new) * acc + jnp.dot(vt, p, preferred_element_type=F32)

    def produce(j, slot):
        return tuple(produce_one(fn, slot, i) for i, fn in enumerate(score_fn(key_offset(j))))

    def consume(j, slot, mx, carry):
        vt = vt_ref[:, pl.ds(key_offset(j), tk)]
        return tuple(consume_one(vt, slot, i, mx[i], m, acc) for i, (m, acc) in enumerate(carry))

    def step(jp, slot_p, jc, slot_c, mx_c, carry):
        fns = score_fn(key_offset(jp))
        vt = vt_ref[:, pl.ds(key_offset(jc), tk)]
        mx_p, out = [], []
        for i, (m, acc) in enumerate(carry):
            mx_p.append(produce_one(fns[i], slot_p, i))
            out.append(consume_one(vt, slot_c, i, mx_c[i], m, acc))
        return tuple(mx_p), tuple(out)

    carry = tuple((jnp.full((1, tq), -jnp.inf, F32), jnp.zeros((rows, tq), F32)) for _ in range(nstream))
    mx = produce(0, 0)
    if unroll:
        for j in range(nk - 2):
            mx, carry = step(j + 1, (j + 1) % 2, j, j % 2, mx, carry)
    else:
        def pair(jj, state):
            mx_even, carry = state
            j = 2 * jj
            mx_odd, carry = step(j + 1, 1, j, 0, mx_even, carry)
            return step(j + 2, 0, j + 1, 1, mx_odd, carry)

        mx, carry = lax.fori_loop(0, nk // 2 - 1, pair, (mx, carry))
    mx, carry = step(nk - 1, 1, nk - 2, 0, mx, carry)
    carry = consume(nk - 1, 1, mx, carry)
    return [acc for _, acc in carry]


def _finish(acc, rows):
    return acc[:rows, :] * (1.0 / acc[rows:rows + 1, :])


def _mla_kernel(q_ref, k_ref, vt_ref, o_ref, s_ref):
    _, nq, tk, tq = s_ref.shape
    qs = [q_ref[i * tq:(i + 1) * tq, :] for i in range(nq)]

    def scores(off):
        ks = k_ref[pl.ds(off, tk), :]
        return [functools.partial(_nt_dot, ks, q) for q in qs]

    accs = _flash_loop(scores, vt_ref, s_ref, MLA_VT_ROWS, unroll=True)
    pad = jnp.zeros((HEAD_W - MLA_V_DIM, tq), F32)
    for i, acc in enumerate(accs):
        ot = jnp.concatenate([_finish(acc, MLA_V_DIM), pad], axis=0)
        o_ref[i * tq:(i + 1) * tq, :] = ot.T.astype(BF16)


def _mla_call(qm, km, vt, batch, seq, tq, tk, nq):
    qm, km = (a.reshape(batch, seq, MLA_SLAB) for a in (qm, km))
    bq = tq * nq
    out = pl.pallas_call(
        _mla_kernel,
        grid=(batch, MLA_HEADS, seq // bq),
        scratch_shapes=[pltpu.VMEM((2, nq, tk, tq), F32)],
        in_specs=[pl.BlockSpec((None, bq, HEAD_W), lambda b, h, i: (b, i, h)),
                  pl.BlockSpec((None, seq, HEAD_W), lambda b, h, i: (b, 0, h)),
                  pl.BlockSpec((None, MLA_VT_ROWS, seq), lambda b, h, i: (b, h, 0))],
        out_specs=pl.BlockSpec((None, bq, HEAD_W), lambda b, h, i: (b, i, h)),
        out_shape=jax.ShapeDtypeStruct((batch, seq, MLA_SLAB), BF16),
        compiler_params=_cparams("parallel", "parallel", "parallel"),
        name="mla",
    )(qm, km, vt)
    return out.reshape(batch * seq, MLA_SLAB)


def _diff_kernel(lam_ref, gsub_ref, q_ref, k_ref, vt_ref, o_ref, s_ref, *, lam_init):
    _, _, tk, tq = s_ref.shape
    hd = pl.program_id(1)
    q0 = pl.program_id(2) * q_ref.shape[0]
    slope = LOG2E * jnp.exp2(-2.0 * (hd + 1).astype(F32) * jnp.ones((1, 1), F32))
    lane = lax.broadcasted_iota(jnp.int32, (1, HEAD_W), 1)
    nq = q_ref.shape[0] // tq
    qs = []
    for i in range(nq):
        q = q_ref[i * tq:(i + 1) * tq, :]
        qs.append((jnp.where(lane < DIFF_HEAD_DIM, q, jnp.zeros_like(q)),
                   jnp.where(lane >= DIFF_HEAD_DIM, q, jnp.zeros_like(q))))
    rel = (lax.broadcasted_iota(jnp.int32, (tk, tq), 0) - lax.broadcasted_iota(jnp.int32, (tk, tq), 1)).astype(F32)

    def scores(off):
        ks = k_ref[pl.ds(off, tk), :]
        out = []
        for i, (q1, q2) in enumerate(qs):
            bias = jnp.abs(rel + (off - q0 - i * tq).astype(F32)) * (-slope)
            out += [lambda q=q1, b=bias: _nt_dot(ks, q) + b, lambda q=q2, b=bias: _nt_dot(ks, q) + b]
        return out

    accs = _flash_loop(scores, vt_ref, s_ref, DIFF_VT_ROWS, unroll=False)

    lam = (jnp.exp(jnp.sum(lam_ref[0:1, :] * lam_ref[1:2, :], axis=-1, keepdims=True))
           - jnp.exp(jnp.sum(lam_ref[2:3, :] * lam_ref[3:4, :], axis=-1, keepdims=True)) + lam_init)
    for i in range(nq):
        o = (_finish(accs[2 * i], HEAD_W) - lam * _finish(accs[2 * i + 1], HEAD_W)).T
        o_ref[i * tq:(i + 1) * tq, :] = (_rms(o, gsub_ref[...], LN_EPS) * (1.0 - lam_init)).astype(BF16)


def _diff_call(d, dvt, lamv, g_sub, lam_init, batch, seq, tq, tk, nq):
    d = d.reshape(batch, seq, 2 * DIFF_WIDTH)
    bq = tq * nq
    out = pl.pallas_call(
        functools.partial(_diff_kernel, lam_init=lam_init),
        grid=(batch, DIFF_HEADS, seq // bq),
        scratch_shapes=[pltpu.VMEM((2, 2 * nq, tk, tq), F32)],
        in_specs=[pl.BlockSpec((8, LANES), lambda b, h, i: (0, 0)),
                  pl.BlockSpec((1, HEAD_W), lambda b, h, i: (0, 0)),
                  pl.BlockSpec((None, bq, HEAD_W), lambda b, h, i: (b, i, h)),
                  pl.BlockSpec((None, seq, HEAD_W), lambda b, h, i: (b, 0, DIFF_HEADS + h)),
                  pl.BlockSpec((None, DIFF_VT_ROWS, seq), lambda b, h, i: (b, h, 0))],
        out_specs=pl.BlockSpec((None, bq, HEAD_W), lambda b, h, i: (b, i, h)),
        out_shape=jax.ShapeDtypeStruct((batch, seq, DIFF_WIDTH), BF16),
        compiler_params=_cparams("parallel", "parallel", "parallel"),
        name="diff",
    )(lamv, g_sub, d, d, dvt)
    return out.reshape(batch * seq, DIFF_WIDTH)


def _first_max(vals):
    mx = functools.reduce(jnp.maximum, vals)
    taken = None
    masks = []
    for v in vals:
        hit = v == mx
        if taken is None:
            masks.append(hit)
            taken = hit
        else:
            masks.append(jnp.logical_and(hit, jnp.logical_not(taken)))
            taken = jnp.logical_or(taken, hit)
    return masks, mx


def _outproj_kernel(om_ref, od_ref, x_ref, woa_ref, wob_ref, g_ref, b_ref, wrt_ref, br_ref,
                    xg_ref, cnt_ref, upper_ref):
    tm = x_ref.shape[0]

    @pl.when(pl.program_id(0) == 0)
    def _():
        cnt_ref[...] = jnp.zeros_like(cnt_ref)
        upper_ref[...] = jnp.where(
            lax.broadcasted_iota(jnp.int32, (tm, tm), 0) <= lax.broadcasted_iota(jnp.int32, (tm, tm), 1),
            1.0, 0.0).astype(BF16)

    mix = (jnp.dot(om_ref[...], woa_ref[...], preferred_element_type=F32)
           + jnp.dot(od_ref[...], wob_ref[...], preferred_element_type=F32))
    x1 = _layer_norm(ALPHA * x_ref[...] + mix, g_ref[...], b_ref[...])
    xg_ref[:, :D_MODEL] = x1

    logits = lax.dot_general(wrt_ref[...], x1, (((1,), (1,)), ((), ())),
                             precision=lax.Precision.HIGHEST, preferred_element_type=F32)
    scores = jax.nn.sigmoid(logits)
    biased = scores + br_ref[...]
    member = [biased[k * N_GROUPS:(k + 1) * N_GROUPS, :] for k in range(EXPERTS_PER_GROUP)]
    score_m = [scores[k * N_GROUPS:(k + 1) * N_GROUPS, :] for k in range(EXPERTS_PER_GROUP)]
    first, top1 = _first_max(member)
    rest = [jnp.where(f, -jnp.inf, v) for f, v in zip(first, member)]
    second, top2 = _first_max(rest)
    grp_score = top1 + top2
    grp_rows = [grp_score[g:g + 1, :] for g in range(N_GROUPS)]
    grp_sel, _ = _first_max(grp_rows)
    in_grp_f = jnp.concatenate([jnp.where(s, 1.0, 0.0) for s in grp_sel], axis=0)
    in_grp = in_grp_f > 0.5
    picked = [jnp.logical_and(in_grp, jnp.logical_or(f, s)) for f, s in zip(first, second)]
    w = [jnp.where(p, sc, 0.0) for p, sc in zip(picked, score_m)]
    denom = jnp.sum(functools.reduce(jnp.add, w), axis=0, keepdims=True)
    gates = jnp.concatenate(w, axis=0) / denom

    nrow = cnt_ref.shape[0]
    grp_rows_f = jnp.concatenate([in_grp_f, jnp.zeros((BF16_SUBLANES - N_GROUPS, tm), F32)], axis=0)
    incl = jnp.dot(grp_rows_f.astype(BF16), upper_ref[...], preferred_element_type=F32)[:nrow]
    onehot = grp_rows_f[:nrow]
    rank = jnp.sum(onehot * (incl - 1.0 + cnt_ref[:, 0:1]), axis=0, keepdims=True)
    row = lax.broadcasted_iota(jnp.int32, (nrow, tm), 0)
    gid = jnp.sum(onehot * row.astype(F32), axis=0, keepdims=True)
    cnt_ref[...] = cnt_ref[...] + incl[:, tm - 1:tm]
    meta = jnp.where(row == 0, gid, jnp.where(row == 1, rank, 0.0))
    slab_t = jnp.concatenate([gates, meta, jnp.zeros((LANES - N_EXPERTS - nrow, tm), F32)], axis=0)
    xg_ref[:, D_MODEL:] = slab_t.T


def _outproj_call(om, od, x, lw, wrt, br, tm):
    t = x.shape[0]
    const = lambda i: (0, 0)
    row = lambda i: (i, 0)
    return pl.pallas_call(
        _outproj_kernel,
        grid=(t // tm,),
        in_specs=[pl.BlockSpec((tm, MLA_SLAB), row),
                  pl.BlockSpec((tm, DIFF_WIDTH), row),
                  pl.BlockSpec((tm, D_MODEL), row),
                  pl.BlockSpec((MLA_SLAB, D_MODEL), const),
                  pl.BlockSpec((DIFF_WIDTH, D_MODEL), const),
                  pl.BlockSpec((1, D_MODEL), const),
                  pl.BlockSpec((1, D_MODEL), const),
                  pl.BlockSpec((N_EXPERTS, D_MODEL), const),
                  pl.BlockSpec((N_EXPERTS, 1), const)],
        out_specs=[pl.BlockSpec((tm, XG_W), row),
                   pl.BlockSpec((8, LANES), const)],
        out_shape=[jax.ShapeDtypeStruct((t, XG_W), F32),
                   jax.ShapeDtypeStruct((8, LANES), F32)],
        scratch_shapes=[pltpu.VMEM((tm, tm), BF16)],
        compiler_params=_cparams("arbitrary"),
        name="outproj",
    )(om, od, x, lw["woa"], lw["wob"], lw["ln1_g"], lw["ln1_b"], wrt, br)


ROW_DMA_UNROLL = 8


def _row_dma_all(row_copy, n):
    def start(r, c):
        row_copy(r).start()
        return c

    def wait(r, c):
        row_copy(r).wait()
        return c

    lax.fori_loop(0, n, start, 0, unroll=ROW_DMA_UNROLL)
    lax.fori_loop(0, n, wait, 0, unroll=ROW_DMA_UNROLL)


def _scatter_kernel(pos_ref, xg_ref, init_ref, xs_ref, sem):
    del init_ref
    tm = xg_ref.shape[0]
    base = pl.program_id(0) * tm
    _row_dma_all(lambda r: pltpu.make_async_copy(xg_ref.at[pl.ds(r, 1), :],
                                                 xs_ref.at[pl.ds(pos_ref[base + r], 1), :], sem), tm)


def _scatter_call(pos, xg, init, tm):
    t = xg.shape[0]
    return pl.pallas_call(
        _scatter_kernel,
        grid_spec=pltpu.PrefetchScalarGridSpec(
            num_scalar_prefetch=1,
            grid=(t // tm,),
            in_specs=[pl.BlockSpec((tm, XG_W), lambda i, pos: (i, 0)),
                      pl.BlockSpec(memory_space=pl.ANY)],
            out_specs=pl.BlockSpec(memory_space=pl.ANY),
            scratch_shapes=[pltpu.SemaphoreType.DMA]),
        out_shape=jax.ShapeDtypeStruct(init.shape, F32),
        input_output_aliases={2: 0},
        compiler_params=_cparams("arbitrary"),
        name="moe_scatter",
    )(pos, xg, init)


def _moe_kernel(tg_ref, nused_ref, xs_ref, wg_ref, wu_ref, wd_ref, ys_ref):
    del tg_ref

    @pl.when(pl.program_id(0) < nused_ref[0])
    def _():
        xb = xs_ref[:, :D_MODEL].astype(BF16)
        lane = lax.broadcasted_iota(jnp.int32, (1, LANES), 1)
        y = None
        for k in range(EXPERTS_PER_GROUP):
            hg = jnp.dot(xb, wg_ref[k], preferred_element_type=F32)
            hu = jnp.dot(xb, wu_ref[k], preferred_element_type=F32)
            mine = jnp.logical_and(lane >= k * N_GROUPS, lane < (k + 1) * N_GROUPS)
            gate = jnp.sum(jnp.where(mine, xs_ref[:, D_MODEL:], 0.0), axis=-1, keepdims=True)
            hh = (hg * jax.nn.sigmoid(hg) * hu * gate).astype(BF16)
            d = jnp.dot(hh, wd_ref[k], preferred_element_type=F32)
            y = d if y is None else y + d
        ys_ref[...] = y

    @pl.when(pl.program_id(0) >= nused_ref[0])
    def _():
        ys_ref[...] = jnp.zeros_like(ys_ref)


def _moe_call(tile_group, nused, xs, layer, moe_w, tm):
    rows = xs.shape[0]
    wmap = lambda i, tg, nu: (layer * N_GROUPS + tg[i], 0, 0, 0)
    wg, wu, wd = moe_w
    return pl.pallas_call(
        _moe_kernel,
        grid_spec=pltpu.PrefetchScalarGridSpec(
            num_scalar_prefetch=2,
            grid=(rows // tm,),
            in_specs=[pl.BlockSpec((tm, XG_W), lambda i, tg, nu: (i, 0)),
                      pl.BlockSpec((None, EXPERTS_PER_GROUP, D_MODEL, D_FF_EXPERT), wmap),
                      pl.BlockSpec((None, EXPERTS_PER_GROUP, D_MODEL, D_FF_EXPERT), wmap),
                      pl.BlockSpec((None, EXPERTS_PER_GROUP, D_FF_EXPERT, D_MODEL), wmap)],
            out_specs=pl.BlockSpec((tm, D_MODEL), lambda i, tg, nu: (i, 0))),
        out_shape=jax.ShapeDtypeStruct((rows, D_MODEL), F32),
        compiler_params=_cparams("parallel"),
        name="moe",
    )(tile_group, nused, xs, wg, wu, wd)


def _unsort_ln_kernel(pos_ref, x1_ref, ys_ref, g_ref, b_ref, o_ref, buf_ref, sem):
    tm = x1_ref.shape[0]
    base = pl.program_id(0) * tm
    _row_dma_all(lambda r: pltpu.make_async_copy(ys_ref.at[pl.ds(pos_ref[base + r], 1), :],
                                                 buf_ref.at[pl.ds(r, 1), :], sem), tm)
    o_ref[...] = _layer_norm(ALPHA * x1_ref[...] + buf_ref[...], g_ref[...], b_ref[...])


def _unsort_ln_call(pos, xg, ys, lw, tm):
    t = xg.shape[0]
    return pl.pallas_call(
        _unsort_ln_kernel,
        grid_spec=pltpu.PrefetchScalarGridSpec(
            num_scalar_prefetch=1,
            grid=(t // tm,),
            in_specs=[pl.BlockSpec((tm, D_MODEL), lambda i, pos: (i, 0)),
                      pl.BlockSpec(memory_space=pl.ANY),
                      pl.BlockSpec((1, D_MODEL), lambda i, pos: (0, 0)),
                      pl.BlockSpec((1, D_MODEL), lambda i, pos: (0, 0))],
            out_specs=pl.BlockSpec((tm, D_MODEL), lambda i, pos: (i, 0)),
            scratch_shapes=[pltpu.VMEM((tm, D_MODEL), F32), pltpu.SemaphoreType.DMA]),
        out_shape=jax.ShapeDtypeStruct((t, D_MODEL), F32),
        compiler_params=_cparams("arbitrary"),
        name="moe_unsort_ln",
    )(pos, xg, ys, lw["ln2_g"], lw["ln2_b"])


def _moe_layer(xg, cnt, lw, layer, moe_w, xs_prev, tm_sorted, tm_tok):
    t = xg.shape[0]
    counts = cnt[:N_GROUPS, 0].astype(jnp.int32)
    padded = (counts + tm_sorted - 1) // tm_sorted * tm_sorted
    ends = jnp.cumsum(padded)
    starts = ends - padded
    grp = xg[:, D_MODEL + ROUTE_GROUP_LANE].astype(jnp.int32)
    rank = xg[:, D_MODEL + ROUTE_RANK_LANE].astype(jnp.int32)
    pos = rank + jnp.sum(jnp.where(grp[:, None] == jnp.arange(N_GROUPS)[None, :], starts[None, :], 0), axis=1)
    ntiles = t // tm_sorted + N_GROUPS
    tile_start = jnp.arange(ntiles, dtype=jnp.int32) * tm_sorted
    tile_group = jnp.minimum(jnp.sum((tile_start[:, None] >= ends[None, :]).astype(jnp.int32), axis=1), N_GROUPS - 1)
    nused = (ends[-1:] // tm_sorted).astype(jnp.int32)
    if xs_prev is None:
        xs_prev = jnp.zeros((ntiles * tm_sorted, XG_W), F32)
    xs = _scatter_call(pos, xg, xs_prev, tm_tok)
    ys = _moe_call(tile_group, nused, xs, layer, moe_w, tm_sorted)
    return _unsort_ln_call(pos, xg, ys, lw, tm_tok), xs


def _rotate_half_cols(w):
    return jnp.concatenate([-w[..., ROPE_HALF:], w[..., :ROPE_HALF]], axis=-1)


def _prep_layer(l, w_in, g_q, g_kv, w_uq, w_ukv, lam_q1, lam_k1, lam_q2, lam_k2, g_sub, w_o,
                ln1_g, ln1_b, ln2_g, ln2_b):
    d = D_MODEL
    z = lambda *s: jnp.zeros(s, F32)
    wi = w_in[l]
    c_kr = Q_LORA_RANK + KV_LORA_RANK
    c_dq = c_kr + MLA_ROPE_DIM
    w_kr = wi[:, c_kr:c_dq]
    pad_l, pad_r = z(d, MLA_NOPE_DIM), z(d, HEAD_W - MLA_NOPE_DIM - MLA_ROPE_DIM)
    w_in_wide = jnp.concatenate([
        wi[:, :c_kr],
        pad_l, w_kr, pad_r,
        pad_l, _rotate_half_cols(w_kr), pad_r,
        wi[:, c_dq:c_dq + 2 * DIFF_WIDTH]], axis=1).astype(BF16)
    wdvt = wi[:, c_dq + 2 * DIFF_WIDTH:].T.astype(BF16)

    r = Q_LORA_RANK
    wq = w_uq[l].reshape(r, MLA_HEADS, MLA_NOPE_DIM + MLA_ROPE_DIM)
    wq_rope = wq[..., MLA_NOPE_DIM:]
    tail = z(r, MLA_HEADS, HEAD_W - MLA_NOPE_DIM - MLA_ROPE_DIM)
    wqa = jnp.concatenate([wq, tail], axis=-1).reshape(r, MLA_SLAB).astype(BF16)
    wqb = jnp.concatenate([z(r, MLA_HEADS, MLA_NOPE_DIM), _rotate_half_cols(wq_rope), tail],
                          axis=-1).reshape(r, MLA_SLAB).astype(BF16)
    r = KV_LORA_RANK
    wkv = w_ukv[l].reshape(r, MLA_HEADS, MLA_NOPE_DIM + MLA_V_DIM)
    wk = jnp.concatenate([wkv[..., :MLA_NOPE_DIM], z(r, MLA_HEADS, HEAD_W - MLA_NOPE_DIM)],
                         axis=-1).reshape(r, MLA_SLAB).astype(BF16)
    wvt = wkv[..., MLA_NOPE_DIM:].reshape(r, MLA_WIDTH).T.astype(BF16)

    wo_m = w_o[l][:MLA_WIDTH].reshape(MLA_HEADS, MLA_V_DIM, d)
    woa = jnp.concatenate([wo_m, z(MLA_HEADS, HEAD_W - MLA_V_DIM, d)], axis=1).reshape(MLA_SLAB, d).astype(BF16)
    wob = w_o[l][MLA_WIDTH:].astype(BF16)

    lamv = jnp.zeros((8, LANES), F32).at[:4, :DIFF_HEAD_DIM].set(
        jnp.stack([lam_q1[l], lam_k1[l], lam_q2[l], lam_k2[l]]).astype(F32))
    return dict(
        w_in=w_in_wide, g_q=g_q[l][None], g_kv=g_kv[l][None], wqa=wqa, wqb=wqb, wk=wk, wvt=wvt, wdvt=wdvt,
        lamv=lamv, g_sub=g_sub[l][None], woa=woa, wob=wob, ln1_g=ln1_g[l][None], ln1_b=ln1_b[l][None],
        ln2_g=ln2_g[l][None], ln2_b=ln2_b[l][None])


def _rope_table(seq):
    inv = 1.0 / (ROPE_THETA ** (jnp.arange(0, MLA_ROPE_DIM, 2, dtype=F32) / MLA_ROPE_DIM))
    ang = jnp.arange(seq, dtype=F32)[:, None] * inv[None, :]
    cos, sin = jnp.cos(ang), jnp.sin(ang)
    scale = LOG2E * (MLA_NOPE_DIM + MLA_ROPE_DIM) ** -0.5
    ones = jnp.ones((seq, MLA_NOPE_DIM), F32)
    zl = jnp.zeros((seq, MLA_NOPE_DIM), F32)
    zr = jnp.zeros((seq, HEAD_W - MLA_NOPE_DIM - MLA_ROPE_DIM), F32)
    ct_q = jnp.concatenate([ones, cos, cos, zr], axis=1) * scale
    st_q = jnp.concatenate([zl, sin, sin, zr], axis=1) * scale
    ct_k = jnp.concatenate([zl, cos, cos, zr], axis=1)
    st_k = jnp.concatenate([zl, sin, sin, zr], axis=1)
    return jnp.concatenate([ct_q, st_q, ct_k, st_k], axis=1)


def _pick(n, pref):
    while n % pref:
        pref //= 2
    return pref


def _trunk(x, ln0_g, ln0_b, layers, moe_w, wrt, br):
    batch, seq, _ = x.shape
    t = batch * seq
    tm_proj = _pick(seq, 512)
    tm_tok = _pick(t, 1024)
    tm_moe = _pick(t, 512)
    tq = _pick(seq, 256)
    tk = _pick(seq // 2, 512)
    nq = _pick(seq // tq, 8)
    nq_diff = _pick(seq // tq, 4 if seq // tk >= 8 else 2)
    tab = _rope_table(seq)
    x = _ln_call(x.reshape(t, D_MODEL), ln0_g[None], ln0_b[None], tm_tok)
    xs = None
    for l, lw in enumerate(layers):
        lam_init = 0.8 - 0.6 * math.exp(-0.3 * l)
        qm, km, vt, d, dvt = _proj_call(x, tab, lw, seq, tm_proj)
        om = _mla_call(qm, km, vt, batch, seq, tq, tk, nq)
        od = _diff_call(d, dvt, lw["lamv"], lw["g_sub"], lam_init, batch, seq, tq, tk, nq_diff)
        xg, cnt = _outproj_call(om, od, x, lw, wrt, br, tm_tok)
        x, xs = _moe_layer(xg, cnt, lw, l, moe_w, xs, tm_moe, tm_tok)
    return x.reshape(batch, seq, D_MODEL)


def kernel(x_prompt, x_sample, ln0_g, ln0_b, w_in, g_q, g_kv, w_uq, w_ukv, lam_q1, lam_k1, lam_q2, lam_k2,
           g_sub, w_o, ln1_g, ln1_b, w_router, b_router, w_gate, w_up, w_down, ln2_g, ln2_b):
    layers = [_prep_layer(l, w_in, g_q, g_kv, w_uq, w_ukv, lam_q1, lam_k1, lam_q2, lam_k2, g_sub, w_o,
                          ln1_g, ln1_b, ln2_g, ln2_b) for l in range(DEPTH)]
    moe_w = tuple(w.astype(BF16).reshape((DEPTH * N_GROUPS, EXPERTS_PER_GROUP) + w.shape[2:])
                  for w in (w_gate, w_up, w_down))
    perm = jnp.arange(N_EXPERTS).reshape(N_GROUPS, EXPERTS_PER_GROUP).T.reshape(-1)
    wrt = w_router.T[perm].astype(F32)
    br = b_router[perm].astype(F32)[:, None]
    y_prompt = _trunk(x_prompt, ln0_g, ln0_b, layers, moe_w, wrt, br)
    y_sample = _trunk(x_sample, ln0_g, ln0_b, layers, moe_w, wrt, br)
    return (y_prompt, y_sample)
```

```python
import functools
import math

import jax
import jax.numpy as jnp
from jax import lax
from jax.experimental import pallas as pl
from jax.experimental.pallas import tpu as pltpu

F32 = jnp.float32
BF16 = jnp.bfloat16

D_MODEL = 1024
DEPTH = 4
MLA_HEADS = 8
MLA_NOPE_DIM = 64
MLA_ROPE_DIM = 32
MLA_V_DIM = 64
Q_LORA_RANK = 256
KV_LORA_RANK = 128
ROPE_THETA = 10000.0
DIFF_HEADS = 4
DIFF_HEAD_DIM = 64
MLA_WIDTH = MLA_HEADS * MLA_V_DIM
DIFF_WIDTH = DIFF_HEADS * 2 * DIFF_HEAD_DIM
N_EXPERTS = 16
N_GROUPS = 4
EXPERTS_PER_GROUP = N_EXPERTS // N_GROUPS
D_FF_EXPERT = 512
LN_EPS = 1e-5
RMS_EPS = 1e-6
ALPHA = (2 * DEPTH) ** 0.25

LANES = 128
HEAD_W = LANES
MLA_SLAB = MLA_HEADS * HEAD_W
ROPE_HALF = MLA_ROPE_DIM // 2
C_Q = 0
C_KV = C_Q + Q_LORA_RANK
C_KR_A = C_KV + KV_LORA_RANK
C_KR_B = C_KR_A + HEAD_W
C_D = C_KR_B + HEAD_W
IN_W = C_D + 2 * DIFF_WIDTH
BF16_SUBLANES = 16
MLA_VT_ROWS = MLA_V_DIM + BF16_SUBLANES
DIFF_VT_ROWS = 2 * DIFF_HEAD_DIM + BF16_SUBLANES
LOG2E = math.log2(math.e)
XG_W = D_MODEL + LANES
ROUTE_GROUP_ROW = 0
ROUTE_RANK_ROW = 1

VMEM_LIMIT_BYTES = 48 * 1024 * 1024


def _cparams(*sem):
    return pltpu.CompilerParams(dimension_semantics=sem, vmem_limit_bytes=VMEM_LIMIT_BYTES)


def _layer_norm(x, g, b):
    mu = jnp.mean(x, axis=-1, keepdims=True)
    xc = x - mu
    var = jnp.mean(xc * xc, axis=-1, keepdims=True)
    return xc * lax.rsqrt(var + LN_EPS) * g + b


def _rms(x, g, eps):
    return x * lax.rsqrt(jnp.mean(x * x, axis=-1, keepdims=True) + eps) * g


def _ln_kernel(x_ref, g_ref, b_ref, o_ref):
    o_ref[...] = _layer_norm(x_ref[...], g_ref[...], b_ref[...])


def _ln_call(x, g, b, tm):
    t = x.shape[0]
    return pl.pallas_call(
        _ln_kernel,
        grid=(t // tm,),
        in_specs=[pl.BlockSpec((tm, D_MODEL), lambda i: (i, 0)),
                  pl.BlockSpec((1, D_MODEL), lambda i: (0, 0)),
                  pl.BlockSpec((1, D_MODEL), lambda i: (0, 0))],
        out_specs=pl.BlockSpec((tm, D_MODEL), lambda i: (i, 0)),
        out_shape=jax.ShapeDtypeStruct((t, D_MODEL), F32),
        compiler_params=_cparams("parallel"),
        name="ln0",
    )(x, g, b)


def _ones_rows(n):
    row = lax.broadcasted_iota(jnp.int32, (BF16_SUBLANES, n), 0)
    return jnp.where(row == 0, 1.0, 0.0).astype(BF16)


def _nt_dot(a, b):
    return lax.dot_general(a, b, (((1,), (1,)), ((), ())), preferred_element_type=F32)


def _proj_kernel(x_ref, tab_ref, win_ref, gq_ref, gkv_ref, wqa_ref, wqb_ref, wk_ref, wvt_ref, wdvt_ref,
                 qm_ref, km_ref, vt_ref, d_ref, dvt_ref):
    tm = x_ref.shape[0]
    xb = x_ref[...].astype(BF16)
    h = jnp.dot(xb, win_ref[...], preferred_element_type=F32)
    ct_q = tab_ref[:, 0 * HEAD_W:1 * HEAD_W]
    st_q = tab_ref[:, 1 * HEAD_W:2 * HEAD_W]
    ct_k = tab_ref[:, 2 * HEAD_W:3 * HEAD_W]
    st_k = tab_ref[:, 3 * HEAD_W:4 * HEAD_W]

    cq = _rms(h[:, C_Q:C_KV], gq_ref[...], RMS_EPS).astype(BF16)
    qa = jnp.dot(cq, wqa_ref[...], preferred_element_type=F32)
    qb = jnp.dot(cq, wqb_ref[...], preferred_element_type=F32)
    ckv = _rms(h[:, C_KV:C_KR_A], gkv_ref[...], RMS_EPS).astype(BF16)
    kn = jnp.dot(ckv, wk_ref[...], preferred_element_type=F32)
    kr = h[:, C_KR_A:C_KR_B] * ct_k + h[:, C_KR_B:C_D] * st_k
    ones = _ones_rows(tm)
    vt = _nt_dot(wvt_ref[...], ckv).astype(BF16)
    for hd in range(MLA_HEADS):
        sl = slice(hd * HEAD_W, (hd + 1) * HEAD_W)
        qm_ref[:, sl] = (qa[:, sl] * ct_q + qb[:, sl] * st_q).astype(BF16)
        km_ref[:, sl] = (kn[:, sl] + kr).astype(BF16)
        r0 = hd * MLA_VT_ROWS
        vt_ref[r0:r0 + MLA_V_DIM, :] = vt[hd * MLA_V_DIM:(hd + 1) * MLA_V_DIM, :]
        vt_ref[r0 + MLA_V_DIM:r0 + MLA_VT_ROWS, :] = ones
    d_ref[:, :DIFF_WIDTH] = (h[:, C_D:C_D + DIFF_WIDTH] * (LOG2E * DIFF_HEAD_DIM ** -0.5)).astype(BF16)
    d_ref[:, DIFF_WIDTH:] = h[:, C_D + DIFF_WIDTH:].astype(BF16)
    dvt = _nt_dot(wdvt_ref[...], xb).astype(BF16)
    for hd in range(DIFF_HEADS):
        r0 = hd * DIFF_VT_ROWS
        dvt_ref[r0:r0 + HEAD_W, :] = dvt[hd * HEAD_W:(hd + 1) * HEAD_W, :]
        dvt_ref[r0 + HEAD_W:r0 + DIFF_VT_ROWS, :] = ones


def _proj_call(x, tab, lw, seq, tm):
    t = x.shape[0]
    batch = t // seq
    nblk = seq // tm
    const = lambda i: (0, 0)
    row = lambda i: (i, 0)
    tcol = lambda i: (i // nblk, 0, i % nblk)
    return pl.pallas_call(
        _proj_kernel,
        grid=(t // tm,),
        in_specs=[pl.BlockSpec((tm, D_MODEL), row),
                  pl.BlockSpec((tm, 4 * HEAD_W), lambda i: (i % nblk, 0)),
                  pl.BlockSpec((D_MODEL, IN_W), const),
                  pl.BlockSpec((1, Q_LORA_RANK), const),
                  pl.BlockSpec((1, KV_LORA_RANK), const),
                  pl.BlockSpec((Q_LORA_RANK, MLA_SLAB), const),
                  pl.BlockSpec((Q_LORA_RANK, MLA_SLAB), const),
                  pl.BlockSpec((KV_LORA_RANK, MLA_SLAB), const),
                  pl.BlockSpec((MLA_WIDTH, KV_LORA_RANK), const),
                  pl.BlockSpec((DIFF_WIDTH, D_MODEL), const)],
        out_specs=[pl.BlockSpec((tm, MLA_SLAB), row),
                   pl.BlockSpec((tm, MLA_SLAB), row),
                   pl.BlockSpec((None, MLA_HEADS * MLA_VT_ROWS, tm), tcol),
                   pl.BlockSpec((tm, 2 * DIFF_WIDTH), row),
                   pl.BlockSpec((None, DIFF_HEADS * DIFF_VT_ROWS, tm), tcol)],
        out_shape=[jax.ShapeDtypeStruct((t, MLA_SLAB), BF16),
                   jax.ShapeDtypeStruct((t, MLA_SLAB), BF16),
                   jax.ShapeDtypeStruct((batch, MLA_HEADS * MLA_VT_ROWS, seq), BF16),
                   jax.ShapeDtypeStruct((t, 2 * DIFF_WIDTH), BF16),
                   jax.ShapeDtypeStruct((batch, DIFF_HEADS * DIFF_VT_ROWS, seq), BF16)],
        compiler_params=_cparams("parallel"),
        name="proj",
    )(x, tab, lw["w_in"], lw["g_q"], lw["g_kv"], lw["wqa"], lw["wqb"], lw["wk"], lw["wvt"], lw["wdvt"])


def _flash_loop(score_fn, vt_ref, s_ref, rows, unroll):
    _, nstream, tk, tq = s_ref.shape
    nk = vt_ref.shape[1] // tk

    def key_offset(j):
        return j * tk if isinstance(j, int) else pl.multiple_of(j * tk, tk)

    def produce_one(fn, slot, i):
        st = fn()
        s_ref[slot, i] = st
        return jnp.max(st, axis=0, keepdims=True)

    def consume_one(vt, slot, i, mx_i, m, acc):
        m_new = jnp.maximum(m, mx_i)
        p = jnp.exp2(s_ref[slot, i] - m_new).astype(BF16)
        return m_new, jnp.exp2(m - m_new) * acc + jnp.dot(vt, p, preferred_element_type=F32)

    def produce(j, slot):
        return tuple(produce_one(fn, slot, i) for i, fn in enumerate(score_fn(key_offset(j))))

    def consume(j, slot, mx, carry):
        vt = vt_ref[:, pl.ds(key_offset(j), tk)]
        return tuple(consume_one(vt, slot, i, mx[i], m, acc) for i, (m, acc) in enumerate(carry))

    def step(jp, slot_p, jc, slot_c, mx_c, carry):
        fns = score_fn(key_offset(jp))
        vt = vt_ref[:, pl.ds(key_offset(jc), tk)]
        mx_p, out = [], []
        for i, (m, acc) in enumerate(carry):
            mx_p.append(produce_one(fns[i], slot_p, i))
            out.append(consume_one(vt, slot_c, i, mx_c[i], m, acc))
        return tuple(mx_p), tuple(out)

    carry = tuple((jnp.full((1, tq), -jnp.inf, F32), jnp.zeros((rows, tq), F32)) for _ in range(nstream))
    mx = produce(0, 0)
    if unroll:
        for j in range(nk - 2):
            mx, carry = step(j + 1, (j + 1) % 2, j, j % 2, mx, carry)
    else:
        def pair(jj, state):
            mx_even, carry = state
            j = 2 * jj
            mx_odd, carry = step(j + 1, 1, j, 0, mx_even, carry)
            return step(j + 2, 0, j + 1, 1, mx_odd, carry)

        mx, carry = lax.fori_loop(0, nk // 2 - 1, pair, (mx, carry))
    mx, carry = step(nk - 1, 1, nk - 2, 0, mx, carry)
    carry = consume(nk - 1, 1, mx, carry)
    return [acc for _, acc in carry]


def _finish(acc, rows):
    return acc[:rows, :] * (1.0 / acc[rows:rows + 1, :])


def _mla_kernel(q_ref, k_ref, vt_ref, o_ref, s_ref):
    _, nq, tk, tq = s_ref.shape
    qs = [q_ref[i * tq:(i + 1) * tq, :] for i in range(nq)]

    def scores(off):
        ks = k_ref[pl.ds(off, tk), :]
        return [functools.partial(_nt_dot, ks, q) for q in qs]

    accs = _flash_loop(scores, vt_ref, s_ref, MLA_VT_ROWS, unroll=True)
    pad = jnp.zeros((HEAD_W - MLA_V_DIM, tq), F32)
    for i, acc in enumerate(accs):
        ot = jnp.concatenate([_finish(acc, MLA_V_DIM), pad], axis=0)
        o_ref[i * tq:(i + 1) * tq, :] = ot.T.astype(BF16)


def _mla_call(qm, km, vt, batch, seq, tq, tk, nq):
    qm, km = (a.reshape(batch, seq, MLA_SLAB) for a in (qm, km))
    bq = tq * nq
    out = pl.pallas_call(
        _mla_kernel,
        grid=(batch, MLA_HEADS, seq // bq),
        scratch_shapes=[pltpu.VMEM((2, nq, tk, tq), F32)],
        in_specs=[pl.BlockSpec((None, bq, HEAD_W), lambda b, h, i: (b, i, h)),
                  pl.BlockSpec((None, seq, HEAD_W), lambda b, h, i: (b, 0, h)),
                  pl.BlockSpec((None, MLA_VT_ROWS, seq), lambda b, h, i: (b, h, 0))],
        out_specs=pl.BlockSpec((None, bq, HEAD_W), lambda b, h, i: (b, i, h)),
        out_shape=jax.ShapeDtypeStruct((batch, seq, MLA_SLAB), BF16),
        compiler_params=_cparams("parallel", "parallel", "parallel"),
        name="mla",
    )(qm, km, vt)
    return out.reshape(batch * seq, MLA_SLAB)


def _diff_kernel(lam_ref, gsub_ref, q_ref, k_ref, vt_ref, o_ref, s_ref, *, lam_init):
    _, _, tk, tq = s_ref.shape
    hd = pl.program_id(1)
    q0 = pl.program_id(2) * q_ref.shape[0]
    slope = LOG2E * jnp.exp2(-2.0 * (hd + 1).astype(F32) * jnp.ones((1, 1), F32))
    lane = lax.broadcasted_iota(jnp.int32, (1, HEAD_W), 1)
    nq = q_ref.shape[0] // tq
    qs = []
    for i in range(nq):
        q = q_ref[i * tq:(i + 1) * tq, :]
        qs.append((jnp.where(lane < DIFF_HEAD_DIM, q, jnp.zeros_like(q)),
                   jnp.where(lane >= DIFF_HEAD_DIM, q, jnp.zeros_like(q))))
    rel = (lax.broadcasted_iota(jnp.int32, (tk, tq), 0) - lax.broadcasted_iota(jnp.int32, (tk, tq), 1)).astype(F32)

    def scores(off):
        ks = k_ref[pl.ds(off, tk), :]
        out = []
        for i, (q1, q2) in enumerate(qs):
            bias = jnp.abs(rel + (off - q0 - i * tq).astype(F32)) * (-slope)
            out += [lambda q=q1, b=bias: _nt_dot(ks, q) + b, lambda q=q2, b=bias: _nt_dot(ks, q) + b]
        return out

    accs = _flash_loop(scores, vt_ref, s_ref, DIFF_VT_ROWS, unroll=False)

    lam = (jnp.exp(jnp.sum(lam_ref[0:1, :] * lam_ref[1:2, :], axis=-1, keepdims=True))
           - jnp.exp(jnp.sum(lam_ref[2:3, :] * lam_ref[3:4, :], axis=-1, keepdims=True)) + lam_init)
    for i in range(nq):
        o = (_finish(accs[2 * i], HEAD_W) - lam * _finish(accs[2 * i + 1], HEAD_W)).T
        o_ref[i * tq:(i + 1) * tq, :] = (_rms(o, gsub_ref[...], LN_EPS) * (1.0 - lam_init)).astype(BF16)


def _diff_call(d, dvt, lamv, g_sub, lam_init, batch, seq, tq, tk, nq):
    d = d.reshape(batch, seq, 2 * DIFF_WIDTH)
    bq = tq * nq
    out = pl.pallas_call(
        functools.partial(_diff_kernel, lam_init=lam_init),
        grid=(batch, DIFF_HEADS, seq // bq),
        scratch_shapes=[pltpu.VMEM((2, 2 * nq, tk, tq), F32)],
        in_specs=[pl.BlockSpec((8, LANES), lambda b, h, i: (0, 0)),
                  pl.BlockSpec((1, HEAD_W), lambda b, h, i: (0, 0)),
                  pl.BlockSpec((None, bq, HEAD_W), lambda b, h, i: (b, i, h)),
                  pl.BlockSpec((None, seq, HEAD_W), lambda b, h, i: (b, 0, DIFF_HEADS + h)),
                  pl.BlockSpec((None, DIFF_VT_ROWS, seq), lambda b, h, i: (b, h, 0))],
        out_specs=pl.BlockSpec((None, bq, HEAD_W), lambda b, h, i: (b, i, h)),
        out_shape=jax.ShapeDtypeStruct((batch, seq, DIFF_WIDTH), BF16),
        compiler_params=_cparams("parallel", "parallel", "parallel"),
        name="diff",
    )(lamv, g_sub, d, d, dvt)
    return out.reshape(batch * seq, DIFF_WIDTH)


def _first_max(vals):
    mx = functools.reduce(jnp.maximum, vals)
    taken = None
    masks = []
    for v in vals:
        hit = v == mx
        if taken is None:
            masks.append(hit)
            taken = hit
        else:
            masks.append(jnp.logical_and(hit, jnp.logical_not(taken)))
            taken = jnp.logical_or(taken, hit)
    return masks, mx


def _outproj_kernel(om_ref, od_ref, x_ref, woa_ref, wob_ref, g_ref, b_ref, wrt_ref, br_ref,
                    xg_ref, meta_ref, cnt_ref, upper_ref):
    tm = x_ref.shape[0]

    @pl.when(pl.program_id(0) == 0)
    def _():
        cnt_ref[...] = jnp.zeros_like(cnt_ref)
        upper_ref[...] = jnp.where(
            lax.broadcasted_iota(jnp.int32, (tm, tm), 0) <= lax.broadcasted_iota(jnp.int32, (tm, tm), 1),
            1.0, 0.0).astype(BF16)

    mix = (jnp.dot(om_ref[...], woa_ref[...], preferred_element_type=F32)
           + jnp.dot(od_ref[...], wob_ref[...], preferred_element_type=F32))
    x1 = _layer_norm(ALPHA * x_ref[...] + mix, g_ref[...], b_ref[...])
    xg_ref[:, :D_MODEL] = x1

    logits = lax.dot_general(wrt_ref[...], x1, (((1,), (1,)), ((), ())),
                             precision=lax.Precision.HIGHEST, preferred_element_type=F32)
    scores = jax.nn.sigmoid(logits)
    biased = scores + br_ref[...]
    member = [biased[k * N_GROUPS:(k + 1) * N_GROUPS, :] for k in range(EXPERTS_PER_GROUP)]
    score_m = [scores[k * N_GROUPS:(k + 1) * N_GROUPS, :] for k in range(EXPERTS_PER_GROUP)]
    first, top1 = _first_max(member)
    rest = [jnp.where(f, -jnp.inf, v) for f, v in zip(first, member)]
    second, top2 = _first_max(rest)
    grp_score = top1 + top2
    grp_rows = [grp_score[g:g + 1, :] for g in range(N_GROUPS)]
    grp_sel, _ = _first_max(grp_rows)
    in_grp_f = jnp.concatenate([jnp.where(s, 1.0, 0.0) for s in grp_sel], axis=0)
    in_grp = in_grp_f > 0.5
    picked = [jnp.logical_and(in_grp, jnp.logical_or(f, s)) for f, s in zip(first, second)]
    w = [jnp.where(p, sc, 0.0) for p, sc in zip(picked, score_m)]
    denom = jnp.sum(functools.reduce(jnp.add, w), axis=0, keepdims=True)
    gates = jnp.concatenate(w, axis=0) / denom

    nrow = cnt_ref.shape[0]
    grp_rows_f = jnp.concatenate([in_grp_f, jnp.zeros((BF16_SUBLANES - N_GROUPS, tm), F32)], axis=0)
    incl = jnp.dot(grp_rows_f.astype(BF16), upper_ref[...], preferred_element_type=F32)[:nrow]
    onehot = grp_rows_f[:nrow]
    rank = jnp.sum(onehot * (incl - 1.0 + cnt_ref[:, 0:1]), axis=0, keepdims=True)
    row = lax.broadcasted_iota(jnp.int32, (nrow, tm), 0)
    gid = jnp.sum(onehot * row.astype(F32), axis=0, keepdims=True)
    cnt_ref[...] = cnt_ref[...] + incl[:, tm - 1:tm]
    meta_ref[...] = jnp.where(row == ROUTE_GROUP_ROW, gid, jnp.where(row == ROUTE_RANK_ROW, rank, 0.0))
    slab_t = jnp.concatenate([gates, jnp.zeros((LANES - N_EXPERTS, tm), F32)], axis=0)
    xg_ref[:, D_MODEL:] = slab_t.T


def _outproj_call(om, od, x, lw, wrt, br, tm):
    t = x.shape[0]
    const = lambda i: (0, 0)
    row = lambda i: (i, 0)
    return pl.pallas_call(
        _outproj_kernel,
        grid=(t // tm,),
        in_specs=[pl.BlockSpec((tm, MLA_SLAB), row),
                  pl.BlockSpec((tm, DIFF_WIDTH), row),
                  pl.BlockSpec((tm, D_MODEL), row),
                  pl.BlockSpec((MLA_SLAB, D_MODEL), const),
                  pl.BlockSpec((DIFF_WIDTH, D_MODEL), const),
                  pl.BlockSpec((1, D_MODEL), const),
                  pl.BlockSpec((1, D_MODEL), const),
                  pl.BlockSpec((N_EXPERTS, D_MODEL), const),
                  pl.BlockSpec((N_EXPERTS, 1), const)],
        out_specs=[pl.BlockSpec((tm, XG_W), row),
                   pl.BlockSpec((8, tm), lambda i: (0, i)),
                   pl.BlockSpec((8, LANES), const)],
        out_shape=[jax.ShapeDtypeStruct((t, XG_W), F32),
                   jax.ShapeDtypeStruct((8, t), F32),
                   jax.ShapeDtypeStruct((8, LANES), F32)],
        scratch_shapes=[pltpu.VMEM((tm, tm), BF16)],
        compiler_params=_cparams("arbitrary"),
        name="outproj",
    )(om, od, x, lw["woa"], lw["wob"], lw["ln1_g"], lw["ln1_b"], wrt, br)


ROW_DMA_UNROLL = 8


def _row_dma_all(row_copy, n):
    def start(r, c):
        row_copy(r).start()
        return c

    def wait(r, c):
        row_copy(r).wait()
        return c

    lax.fori_loop(0, n, start, 0, unroll=ROW_DMA_UNROLL)
    lax.fori_loop(0, n, wait, 0, unroll=ROW_DMA_UNROLL)


def _scatter_kernel(pos_ref, xg_ref, init_ref, xs_ref, sem):
    del init_ref
    tm = xg_ref.shape[0]
    base = pl.program_id(0) * tm
    _row_dma_all(lambda r: pltpu.make_async_copy(xg_ref.at[pl.ds(r, 1), :],
                                                 xs_ref.at[pl.ds(pos_ref[base + r], 1), :], sem), tm)


def _scatter_call(pos, xg, init, tm):
    t = xg.shape[0]
    return pl.pallas_call(
        _scatter_kernel,
        grid_spec=pltpu.PrefetchScalarGridSpec(
            num_scalar_prefetch=1,
            grid=(t // tm,),
            in_specs=[pl.BlockSpec((tm, XG_W), lambda i, pos: (i, 0)),
                      pl.BlockSpec(memory_space=pl.ANY)],
            out_specs=pl.BlockSpec(memory_space=pl.ANY),
            scratch_shapes=[pltpu.SemaphoreType.DMA]),
        out_shape=jax.ShapeDtypeStruct(init.shape, F32),
        input_output_aliases={2: 0},
        compiler_params=_cparams("arbitrary"),
        name="moe_scatter",
    )(pos, xg, init)


def _moe_kernel(tg_ref, nused_ref, xs_ref, wg_ref, wu_ref, wd_ref, ys_ref):
    del tg_ref

    @pl.when(pl.program_id(0) < nused_ref[0])
    def _():
        xb = xs_ref[:, :D_MODEL].astype(BF16)
        lane = lax.broadcasted_iota(jnp.int32, (1, LANES), 1)
        y = None
        for k in range(EXPERTS_PER_GROUP):
            hg = jnp.dot(xb, wg_ref[k], preferred_element_type=F32)
            hu = jnp.dot(xb, wu_ref[k], preferred_element_type=F32)
            mine = jnp.logical_and(lane >= k * N_GROUPS, lane < (k + 1) * N_GROUPS)
            gate = jnp.sum(jnp.where(mine, xs_ref[:, D_MODEL:], 0.0), axis=-1, keepdims=True)
            hh = (hg * jax.nn.sigmoid(hg) * hu * gate).astype(BF16)
            d = jnp.dot(hh, wd_ref[k], preferred_element_type=F32)
            y = d if y is None else y + d
        ys_ref[...] = y

    @pl.when(pl.program_id(0) >= nused_ref[0])
    def _():
        ys_ref[...] = jnp.zeros_like(ys_ref)


def _moe_call(tile_group, nused, xs, layer, moe_w, tm):
    rows = xs.shape[0]
    wmap = lambda i, tg, nu: (layer * N_GROUPS + tg[i], 0, 0, 0)
    wg, wu, wd = moe_w
    return pl.pallas_call(
        _moe_kernel,
        grid_spec=pltpu.PrefetchScalarGridSpec(
            num_scalar_prefetch=2,
            grid=(rows // tm,),
            in_specs=[pl.BlockSpec((tm, XG_W), lambda i, tg, nu: (i, 0)),
                      pl.BlockSpec((None, EXPERTS_PER_GROUP, D_MODEL, D_FF_EXPERT), wmap),
                      pl.BlockSpec((None, EXPERTS_PER_GROUP, D_MODEL, D_FF_EXPERT), wmap),
                      pl.BlockSpec((None, EXPERTS_PER_GROUP, D_FF_EXPERT, D_MODEL), wmap)],
            out_specs=pl.BlockSpec((tm, D_MODEL), lambda i, tg, nu: (i, 0))),
        out_shape=jax.ShapeDtypeStruct((rows, D_MODEL), F32),
        compiler_params=_cparams("parallel"),
        name="moe",
    )(tile_group, nused, xs, wg, wu, wd)


def _unsort_ln_kernel(pos_ref, x1_ref, ys_ref, g_ref, b_ref, o_ref, buf_ref, sem):
    tm = x1_ref.shape[0]
    base = pl.program_id(0) * tm
    _row_dma_all(lambda r: pltpu.make_async_copy(ys_ref.at[pl.ds(pos_ref[base + r], 1), :],
                                                 buf_ref.at[pl.ds(r, 1), :], sem), tm)
    o_ref[...] = _layer_norm(ALPHA * x1_ref[...] + buf_ref[...], g_ref[...], b_ref[...])


def _unsort_ln_call(pos, xg, ys, lw, tm):
    t = xg.shape[0]
    return pl.pallas_call(
        _unsort_ln_kernel,
        grid_spec=pltpu.PrefetchScalarGridSpec(
            num_scalar_prefetch=1,
            grid=(t // tm,),
            in_specs=[pl.BlockSpec((tm, D_MODEL), lambda i, pos: (i, 0)),
                      pl.BlockSpec(memory_space=pl.ANY),
                      pl.BlockSpec((1, D_MODEL), lambda i, pos: (0, 0)),
                      pl.BlockSpec((1, D_MODEL), lambda i, pos: (0, 0))],
            out_specs=pl.BlockSpec((tm, D_MODEL), lambda i, pos: (i, 0)),
            scratch_shapes=[pltpu.VMEM((tm, D_MODEL), F32), pltpu.SemaphoreType.DMA]),
        out_shape=jax.ShapeDtypeStruct((t, D_MODEL), F32),
        compiler_params=_cparams("arbitrary"),
        name="moe_unsort_ln",
    )(pos, xg, ys, lw["ln2_g"], lw["ln2_b"])


def _moe_layer(xg, meta, cnt, lw, layer, moe_w, xs_prev, tm_sorted, tm_tok):
    t = xg.shape[0]
    counts = cnt[:N_GROUPS, 0].astype(jnp.int32)
    padded = (counts + tm_sorted - 1) // tm_sorted * tm_sorted
    ends = jnp.cumsum(padded)
    starts = ends - padded
    grp = meta[ROUTE_GROUP_ROW].astype(jnp.int32)
    rank = meta[ROUTE_RANK_ROW].astype(jnp.int32)
    pos = rank + jnp.sum(jnp.where(grp[:, None] == jnp.arange(N_GROUPS)[None, :], starts[None, :], 0), axis=1)
    ntiles = t // tm_sorted + N_GROUPS
    tile_start = jnp.arange(ntiles, dtype=jnp.int32) * tm_sorted
    tile_group = jnp.minimum(jnp.sum((tile_start[:, None] >= ends[None, :]).astype(jnp.int32), axis=1), N_GROUPS - 1)
    nused = (ends[-1:] // tm_sorted).astype(jnp.int32)
    if xs_prev is None:
        xs_prev = jnp.zeros((ntiles * tm_sorted, XG_W), F32)
    xs = _scatter_call(pos, xg, xs_prev, tm_tok)
    ys = _moe_call(tile_group, nused, xs, layer, moe_w, tm_sorted)
    return _unsort_ln_call(pos, xg, ys, lw, tm_tok), xs


def _rotate_half_cols(w):
    return jnp.concatenate([-w[..., ROPE_HALF:], w[..., :ROPE_HALF]], axis=-1)


def _prep_layer(l, w_in, g_q, g_kv, w_uq, w_ukv, lam_q1, lam_k1, lam_q2, lam_k2, g_sub, w_o,
                ln1_g, ln1_b, ln2_g, ln2_b):
    d = D_MODEL
    z = lambda *s: jnp.zeros(s, F32)
    wi = w_in[l]
    c_kr = Q_LORA_RANK + KV_LORA_RANK
    c_dq = c_kr + MLA_ROPE_DIM
    w_kr = wi[:, c_kr:c_dq]
    pad_l, pad_r = z(d, MLA_NOPE_DIM), z(d, HEAD_W - MLA_NOPE_DIM - MLA_ROPE_DIM)
    w_in_wide = jnp.concatenate([
        wi[:, :c_kr],
        pad_l, w_kr, pad_r,
        pad_l, _rotate_half_cols(w_kr), pad_r,
        wi[:, c_dq:c_dq + 2 * DIFF_WIDTH]], axis=1).astype(BF16)
    wdvt = wi[:, c_dq + 2 * DIFF_WIDTH:].T.astype(BF16)

    r = Q_LORA_RANK
    wq = w_uq[l].reshape(r, MLA_HEADS, MLA_NOPE_DIM + MLA_ROPE_DIM)
    wq_rope = wq[..., MLA_NOPE_DIM:]
    tail = z(r, MLA_HEADS, HEAD_W - MLA_NOPE_DIM - MLA_ROPE_DIM)
    wqa = jnp.concatenate([wq, tail], axis=-1).reshape(r, MLA_SLAB).astype(BF16)
    wqb = jnp.concatenate([z(r, MLA_HEADS, MLA_NOPE_DIM), _rotate_half_cols(wq_rope), tail],
                          axis=-1).reshape(r, MLA_SLAB).astype(BF16)
    r = KV_LORA_RANK
    wkv = w_ukv[l].reshape(r, MLA_HEADS, MLA_NOPE_DIM + MLA_V_DIM)
    wk = jnp.concatenate([wkv[..., :MLA_NOPE_DIM], z(r, MLA_HEADS, HEAD_W - MLA_NOPE_DIM)],
                         axis=-1).reshape(r, MLA_SLAB).astype(BF16)
    wvt = wkv[..., MLA_NOPE_DIM:].reshape(r, MLA_WIDTH).T.astype(BF16)

    wo_m = w_o[l][:MLA_WIDTH].reshape(MLA_HEADS, MLA_V_DIM, d)
    woa = jnp.concatenate([wo_m, z(MLA_HEADS, HEAD_W - MLA_V_DIM, d)], axis=1).reshape(MLA_SLAB, d).astype(BF16)
    wob = w_o[l][MLA_WIDTH:].astype(BF16)

    lamv = jnp.zeros((8, LANES), F32).at[:4, :DIFF_HEAD_DIM].set(
        jnp.stack([lam_q1[l], lam_k1[l], lam_q2[l], lam_k2[l]]).astype(F32))
    return dict(
        w_in=w_in_wide, g_q=g_q[l][None], g_kv=g_kv[l][None], wqa=wqa, wqb=wqb, wk=wk, wvt=wvt, wdvt=wdvt,
        lamv=lamv, g_sub=g_sub[l][None], woa=woa, wob=wob, ln1_g=ln1_g[l][None], ln1_b=ln1_b[l][None],
        ln2_g=ln2_g[l][None], ln2_b=ln2_b[l][None])


def _rope_table(seq):
    inv = 1.0 / (ROPE_THETA ** (jnp.arange(0, MLA_ROPE_DIM, 2, dtype=F32) / MLA_ROPE_DIM))
    ang = jnp.arange(seq, dtype=F32)[:, None] * inv[None, :]
    cos, sin = jnp.cos(ang), jnp.sin(ang)
    scale = LOG2E * (MLA_NOPE_DIM + MLA_ROPE_DIM) ** -0.5
    ones = jnp.ones((seq, MLA_NOPE_DIM), F32)
    zl = jnp.zeros((seq, MLA_NOPE_DIM), F32)
    zr = jnp.zeros((seq, HEAD_W - MLA_NOPE_DIM - MLA_ROPE_DIM), F32)
    ct_q = jnp.concatenate([ones, cos, cos, zr], axis=1) * scale
    st_q = jnp.concatenate([zl, sin, sin, zr], axis=1) * scale
    ct_k = jnp.concatenate([zl, cos, cos, zr], axis=1)
    st_k = jnp.concatenate([zl, sin, sin, zr], axis=1)
    return jnp.concatenate([ct_q, st_q, ct_k, st_k], axis=1)


def _pick(n, pref):
    while n % pref:
        pref //= 2
    return pref


def _trunk(x, ln0_g, ln0_b, layers, moe_w, wrt, br):
    batch, seq, _ = x.shape
    t = batch * seq
    tm_proj = _pick(seq, 512)
    tm_tok = _pick(t, 1024)
    tm_moe = _pick(t, 512)
    tq = _pick(seq, 256)
    tk = _pick(seq // 2, 512)
    nq = _pick(seq // tq, 8)
    nq_diff = _pick(seq // tq, 4 if seq // tk >= 8 else 2)
    tab = _rope_table(seq)
    x = _ln_call(x.reshape(t, D_MODEL), ln0_g[None], ln0_b[None], tm_tok)
    xs = None
    for l, lw in enumerate(layers):
        lam_init = 0.8 - 0.6 * math.exp(-0.3 * l)
        qm, km, vt, d, dvt = _proj_call(x, tab, lw, seq, tm_proj)
        om = _mla_call(qm, km, vt, batch, seq, tq, tk, nq)
        od = _diff_call(d, dvt, lw["lamv"], lw["g_sub"], lam_init, batch, seq, tq, tk, nq_diff)
        xg, meta, cnt = _outproj_call(om, od, x, lw, wrt, br, tm_tok)
        x, xs = _moe_layer(xg, meta, cnt, lw, l, moe_w, xs, tm_moe, tm_tok)
    return x.reshape(batch, seq, D_MODEL)


def kernel(x_prompt, x_sample, ln0_g, ln0_b, w_in, g_q, g_kv, w_uq, w_ukv, lam_q1, lam_k1, lam_q2, lam_k2,
           g_sub, w_o, ln1_g, ln1_b, w_router, b_router, w_gate, w_up, w_down, ln2_g, ln2_b):
    layers = [_prep_layer(l, w_in, g_q, g_kv, w_uq, w_ukv, lam_q1, lam_k1, lam_q2, lam_k2, g_sub, w_o,
                          ln1_g, ln1_b, ln2_g, ln2_b) for l in range(DEPTH)]
    moe_w = tuple(w.astype(BF16).reshape((DEPTH * N_GROUPS, EXPERTS_PER_GROUP) + w.shape[2:])
                  for w in (w_gate, w_up, w_down))
    perm = jnp.arange(N_EXPERTS).reshape(N_GROUPS, EXPERTS_PER_GROUP).T.reshape(-1)
    wrt = w_router.T[perm].astype(F32)
    br = b_router[perm].astype(F32)[:, None]
    y_prompt = _trunk(x_prompt, ln0_g, ln0_b, layers, moe_w, wrt, br)
    y_sample = _trunk(x_sample, ln0_g, ln0_b, layers, moe_w, wrt, br)
    return (y_prompt, y_sample)
```

```python
import functools
import math

import jax
import jax.numpy as jnp
from jax import lax
from jax.experimental import pallas as pl
from jax.experimental.pallas import tpu as pltpu

F32 = jnp.float32
BF16 = jnp.bfloat16

D_MODEL = 1024
DEPTH = 4
MLA_HEADS = 8
MLA_NOPE_DIM = 64
MLA_ROPE_DIM = 32
MLA_V_DIM = 64
Q_LORA_RANK = 256
KV_LORA_RANK = 128
ROPE_THETA = 10000.0
DIFF_HEADS = 4
DIFF_HEAD_DIM = 64
MLA_WIDTH = MLA_HEADS * MLA_V_DIM
DIFF_WIDTH = DIFF_HEADS * 2 * DIFF_HEAD_DIM
N_EXPERTS = 16
N_GROUPS = 4
EXPERTS_PER_GROUP = N_EXPERTS // N_GROUPS
D_FF_EXPERT = 512
LN_EPS = 1e-5
RMS_EPS = 1e-6
ALPHA = (2 * DEPTH) ** 0.25

LANES = 128
HEAD_W = LANES
MLA_SLAB = MLA_HEADS * HEAD_W
ROPE_HALF = MLA_ROPE_DIM // 2
C_Q = 0
C_KV = C_Q + Q_LORA_RANK
C_KR_A = C_KV + KV_LORA_RANK
C_KR_B = C_KR_A + HEAD_W
C_D = C_KR_B + HEAD_W
IN_W = C_D + 2 * DIFF_WIDTH
BF16_SUBLANES = 16
MLA_VT_ROWS = MLA_V_DIM + BF16_SUBLANES
DIFF_VT_ROWS = 2 * DIFF_HEAD_DIM + BF16_SUBLANES
LOG2E = math.log2(math.e)
XG_W = D_MODEL + LANES
ROUTE_GROUP_ROW = 0
ROUTE_RANK_ROW = 1

VMEM_LIMIT_BYTES = 48 * 1024 * 1024


def _cparams(*sem):
    return pltpu.CompilerParams(dimension_semantics=sem, vmem_limit_bytes=VMEM_LIMIT_BYTES)


def _layer_norm(x, g, b):
    mu = jnp.mean(x, axis=-1, keepdims=True)
    xc = x - mu
    var = jnp.mean(xc * xc, axis=-1, keepdims=True)
    return xc * lax.rsqrt(var + LN_EPS) * g + b


def _rms(x, g, eps):
    return x * lax.rsqrt(jnp.mean(x * x, axis=-1, keepdims=True) + eps) * g


def _ln_kernel(x_ref, g_ref, b_ref, o_ref):
    o_ref[...] = _layer_norm(x_ref[...], g_ref[...], b_ref[...])


def _ln_call(x, g, b, tm):
    t = x.shape[0]
    return pl.pallas_call(
        _ln_kernel,
        grid=(t // tm,),
        in_specs=[pl.BlockSpec((tm, D_MODEL), lambda i: (i, 0)),
                  pl.BlockSpec((1, D_MODEL), lambda i: (0, 0)),
                  pl.BlockSpec((1, D_MODEL), lambda i: (0, 0))],
        out_specs=pl.BlockSpec((tm, D_MODEL), lambda i: (i, 0)),
        out_shape=jax.ShapeDtypeStruct((t, D_MODEL), F32),
        compiler_params=_cparams("parallel"),
        name="ln0",
    )(x, g, b)


def _ones_rows(n):
    row = lax.broadcasted_iota(jnp.int32, (BF16_SUBLANES, n), 0)
    return jnp.where(row == 0, 1.0, 0.0).astype(BF16)


def _nt_dot(a, b):
    return lax.dot_general(a, b, (((1,), (1,)), ((), ())), preferred_element_type=F32)


def _proj_kernel(x_ref, tab_ref, win_ref, gq_ref, gkv_ref, wqa_ref, wqb_ref, wk_ref, wvt_ref, wdvt_ref,
                 qm_ref, km_ref, vt_ref, d_ref, dvt_ref):
    tm = x_ref.shape[0]
    xb = x_ref[...].astype(BF16)
    h = jnp.dot(xb, win_ref[...], preferred_element_type=F32)
    ct_q = tab_ref[:, 0 * HEAD_W:1 * HEAD_W]
    st_q = tab_ref[:, 1 * HEAD_W:2 * HEAD_W]
    ct_k = tab_ref[:, 2 * HEAD_W:3 * HEAD_W]
    st_k = tab_ref[:, 3 * HEAD_W:4 * HEAD_W]

    cq = _rms(h[:, C_Q:C_KV], gq_ref[...], RMS_EPS).astype(BF16)
    qa = jnp.dot(cq, wqa_ref[...], preferred_element_type=F32)
    qb = jnp.dot(cq, wqb_ref[...], preferred_element_type=F32)
    ckv = _rms(h[:, C_KV:C_KR_A], gkv_ref[...], RMS_EPS).astype(BF16)
    kn = jnp.dot(ckv, wk_ref[...], preferred_element_type=F32)
    kr = h[:, C_KR_A:C_KR_B] * ct_k + h[:, C_KR_B:C_D] * st_k
    ones = _ones_rows(tm)
    vt = _nt_dot(wvt_ref[...], ckv).astype(BF16)
    for hd in range(MLA_HEADS):
        sl = slice(hd * HEAD_W, (hd + 1) * HEAD_W)
        qm_ref[:, sl] = (qa[:, sl] * ct_q + qb[:, sl] * st_q).astype(BF16)
        km_ref[:, sl] = (kn[:, sl] + kr).astype(BF16)
        r0 = hd * MLA_VT_ROWS
        vt_ref[r0:r0 + MLA_V_DIM, :] = vt[hd * MLA_V_DIM:(hd + 1) * MLA_V_DIM, :]
        vt_ref[r0 + MLA_V_DIM:r0 + MLA_VT_ROWS, :] = ones
    d_ref[:, :DIFF_WIDTH] = (h[:, C_D:C_D + DIFF_WIDTH] * (LOG2E * DIFF_HEAD_DIM ** -0.5)).astype(BF16)
    d_ref[:, DIFF_WIDTH:] = h[:, C_D + DIFF_WIDTH:].astype(BF16)
    dvt = _nt_dot(wdvt_ref[...], xb).astype(BF16)
    for hd in range(DIFF_HEADS):
        r0 = hd * DIFF_VT_ROWS
        dvt_ref[r0:r0 + HEAD_W, :] = dvt[hd * HEAD_W:(hd + 1) * HEAD_W, :]
        dvt_ref[r0 + HEAD_W:r0 + DIFF_VT_ROWS, :] = ones


def _proj_call(x, tab, lw, seq, tm):
    t = x.shape[0]
    batch = t // seq
    nblk = seq // tm
    const = lambda i: (0, 0)
    row = lambda i: (i, 0)
    tcol = lambda i: (i // nblk, 0, i % nblk)
    return pl.pallas_call(
        _proj_kernel,
        grid=(t // tm,),
        in_specs=[pl.BlockSpec((tm, D_MODEL), row),
                  pl.BlockSpec((tm, 4 * HEAD_W), lambda i: (i % nblk, 0)),
                  pl.BlockSpec((D_MODEL, IN_W), const),
                  pl.BlockSpec((1, Q_LORA_RANK), const),
                  pl.BlockSpec((1, KV_LORA_RANK), const),
                  pl.BlockSpec((Q_LORA_RANK, MLA_SLAB), const),
                  pl.BlockSpec((Q_LORA_RANK, MLA_SLAB), const),
                  pl.BlockSpec((KV_LORA_RANK, MLA_SLAB), const),
                  pl.BlockSpec((MLA_WIDTH, KV_LORA_RANK), const),
                  pl.BlockSpec((DIFF_WIDTH, D_MODEL), const)],
        out_specs=[pl.BlockSpec((tm, MLA_SLAB), row),
                   pl.BlockSpec((tm, MLA_SLAB), row),
                   pl.BlockSpec((None, MLA_HEADS * MLA_VT_ROWS, tm), tcol),
                   pl.BlockSpec((tm, 2 * DIFF_WIDTH), row),
                   pl.BlockSpec((None, DIFF_HEADS * DIFF_VT_ROWS, tm), tcol)],
        out_shape=[jax.ShapeDtypeStruct((t, MLA_SLAB), BF16),
                   jax.ShapeDtypeStruct((t, MLA_SLAB), BF16),
                   jax.ShapeDtypeStruct((batch, MLA_HEADS * MLA_VT_ROWS, seq), BF16),
                   jax.ShapeDtypeStruct((t, 2 * DIFF_WIDTH), BF16),
                   jax.ShapeDtypeStruct((batch, DIFF_HEADS * DIFF_VT_ROWS, seq), BF16)],
        compiler_params=_cparams("parallel"),
        name="proj",
    )(x, tab, lw["w_in"], lw["g_q"], lw["g_kv"], lw["wqa"], lw["wqb"], lw["wk"], lw["wvt"], lw["wdvt"])


def _flash_loop(score_fn, vt_ref, s_ref, rows, unroll):
    _, nstream, tk, tq = s_ref.shape
    nk = vt_ref.shape[1] // tk

    def key_offset(j):
        return j * tk if isinstance(j, int) else pl.multiple_of(j * tk, tk)

    def produce_one(fn, slot, i):
        st = fn()
        s_ref[slot, i] = st
        return jnp.max(st, axis=0, keepdims=True)

    def consume_one(vt, slot, i, mx_i, m, acc):
        m_new = jnp.maximum(m, mx_i)
        p = jnp.exp2(s_ref[slot, i] - m_new).astype(BF16)
        return m_new, jnp.exp2(m - m_new) * acc + jnp.dot(vt, p, preferred_element_type=F32)

    def produce(j, slot):
        return tuple(produce_one(fn, slot, i) for i, fn in enumerate(score_fn(key_offset(j))))

    def consume(j, slot, mx, carry):
        vt = vt_ref[:, pl.ds(key_offset(j), tk)]
        return tuple(consume_one(vt, slot, i, mx[i], m, acc) for i, (m, acc) in enumerate(carry))

    def step(jp, slot_p, jc, slot_c, mx_c, carry):
        fns = score_fn(key_offset(jp))
        vt = vt_ref[:, pl.ds(key_offset(jc), tk)]
        mx_p, out = [], []
        for i, (m, acc) in enumerate(carry):
            mx_p.append(produce_one(fns[i], slot_p, i))
            out.append(consume_one(vt, slot_c, i, mx_c[i], m, acc))
        return tuple(mx_p), tuple(out)

    carry = tuple((jnp.full((1, tq), -jnp.inf, F32), jnp.zeros((rows, tq), F32)) for _ in range(nstream))
    mx = produce(0, 0)
    if unroll:
        for j in range(nk - 2):
            mx, carry = step(j + 1, (j + 1) % 2, j, j % 2, mx, carry)
    else:
        def pair(jj, state):
            mx_even, carry = state
            j = 2 * jj
            mx_odd, carry = step(j + 1, 1, j, 0, mx_even, carry)
            return step(j + 2, 0, j + 1, 1, mx_odd, carry)

        mx, carry = lax.fori_loop(0, nk // 2 - 1, pair, (mx, carry))
    mx, carry = step(nk - 1, 1, nk - 2, 0, mx, carry)
    carry = consume(nk - 1, 1, mx, carry)
    return [acc for _, acc in carry]


def _finish(acc, rows):
    return acc[:rows, :] * (1.0 / acc[rows:rows + 1, :])


def _mla_kernel(q_ref, k_ref, vt_ref, o_ref, s_ref):
    _, nq, tk, tq = s_ref.shape
    qs = [q_ref[i * tq:(i + 1) * tq, :] for i in range(nq)]

    def scores(off):
        ks = k_ref[pl.ds(off, tk), :]
        return [functools.partial(_nt_dot, ks, q) for q in qs]

    accs = _flash_loop(scores, vt_ref, s_ref, MLA_VT_ROWS, unroll=True)
    pad = jnp.zeros((HEAD_W - MLA_V_DIM, tq), F32)
    for i, acc in enumerate(accs):
        ot = jnp.concatenate([_finish(acc, MLA_V_DIM), pad], axis=0)
        o_ref[i * tq:(i + 1) * tq, :] = ot.T.astype(BF16)


def _mla_call(qm, km, vt, batch, seq, tq, tk, nq):
    qm, km = (a.reshape(batch, seq, MLA_SLAB) for a in (qm, km))
    bq = tq * nq
    out = pl.pallas_call(
        _mla_kernel,
        grid=(batch, MLA_HEADS, seq // bq),
        scratch_shapes=[pltpu.VMEM((2, nq, tk, tq), F32)],
        in_specs=[pl.BlockSpec((None, bq, HEAD_W), lambda b, h, i: (b, i, h)),
                  pl.BlockSpec((None, seq, HEAD_W), lambda b, h, i: (b, 0, h)),
                  pl.BlockSpec((None, MLA_VT_ROWS, seq), lambda b, h, i: (b, h, 0))],
        out_specs=pl.BlockSpec((None, bq, HEAD_W), lambda b, h, i: (b, i, h)),
        out_shape=jax.ShapeDtypeStruct((batch, seq, MLA_SLAB), BF16),
        compiler_params=_cparams("parallel", "parallel", "parallel"),
        name="mla",
    )(qm, km, vt)
    return out.reshape(batch * seq, MLA_SLAB)


def _diff_kernel(lam_ref, gsub_ref, q_ref, k_ref, vt_ref, o_ref, s_ref, *, lam_init):
    _, _, tk, tq = s_ref.shape
    hd = pl.program_id(1)
    q0 = pl.program_id(2) * q_ref.shape[0]
    slope = LOG2E * jnp.exp2(-2.0 * (hd + 1).astype(F32) * jnp.ones((1, 1), F32))
    lane = lax.broadcasted_iota(jnp.int32, (1, HEAD_W), 1)
    nq = q_ref.shape[0] // tq
    qs = []
    for i in range(nq):
        q = q_ref[i * tq:(i + 1) * tq, :]
        qs.append((jnp.where(lane < DIFF_HEAD_DIM, q, jnp.zeros_like(q)),
                   jnp.where(lane >= DIFF_HEAD_DIM, q, jnp.zeros_like(q))))
    rel = (lax.broadcasted_iota(jnp.int32, (tk, tq), 0) - lax.broadcasted_iota(jnp.int32, (tk, tq), 1)).astype(F32)

    def scores(off):
        ks = k_ref[pl.ds(off, tk), :]
        out = []
        for i, (q1, q2) in enumerate(qs):
            bias = jnp.abs(rel + (off - q0 - i * tq).astype(F32)) * (-slope)
            out += [lambda q=q1, b=bias: _nt_dot(ks, q) + b, lambda q=q2, b=bias: _nt_dot(ks, q) + b]
        return out

    accs = _flash_loop(scores, vt_ref, s_ref, DIFF_VT_ROWS, unroll=False)

    lam = (jnp.exp(jnp.sum(lam_ref[0:1, :] * lam_ref[1:2, :], axis=-1, keepdims=True))
           - jnp.exp(jnp.sum(lam_ref[2:3, :] * lam_ref[3:4, :], axis=-1, keepdims=True)) + lam_init)
    for i in range(nq):
        o = (_finish(accs[2 * i], HEAD_W) - lam * _finish(accs[2 * i + 1], HEAD_W)).T
        o_ref[i * tq:(i + 1) * tq, :] = (_rms(o, gsub_ref[...], LN_EPS) * (1.0 - lam_init)).astype(BF16)


def _diff_call(d, dvt, lamv, g_sub, lam_init, batch, seq, tq, tk, nq):
    d = d.reshape(batch, seq, 2 * DIFF_WIDTH)
    bq = tq * nq
    out = pl.pallas_call(
        functools.partial(_diff_kernel, lam_init=lam_init),
        grid=(batch, DIFF_HEADS, seq // bq),
        scratch_shapes=[pltpu.VMEM((2, 2 * nq, tk, tq), F32)],
        in_specs=[pl.BlockSpec((8, LANES), lambda b, h, i: (0, 0)),
                  pl.BlockSpec((1, HEAD_W), lambda b, h, i: (0, 0)),
                  pl.BlockSpec((None, bq, HEAD_W), lambda b, h, i: (b, i, h)),
                  pl.BlockSpec((None, seq, HEAD_W), lambda b, h, i: (b, 0, DIFF_HEADS + h)),
                  pl.BlockSpec((None, DIFF_VT_ROWS, seq), lambda b, h, i: (b, h, 0))],
        out_specs=pl.BlockSpec((None, bq, HEAD_W), lambda b, h, i: (b, i, h)),
        out_shape=jax.ShapeDtypeStruct((batch, seq, DIFF_WIDTH), BF16),
        compiler_params=_cparams("parallel", "parallel", "parallel"),
        name="diff",
    )(lamv, g_sub, d, d, dvt)
    return out.reshape(batch * seq, DIFF_WIDTH)


def _first_max(vals):
    mx = functools.reduce(jnp.maximum, vals)
    taken = None
    masks = []
    for v in vals:
        hit = v == mx
        if taken is None:
            masks.append(hit)
            taken = hit
        else:
            masks.append(jnp.logical_and(hit, jnp.logical_not(taken)))
            taken = jnp.logical_or(taken, hit)
    return masks, mx


def _outproj_kernel(om_ref, od_ref, x_ref, woa_ref, wob_ref, g_ref, b_ref, wrt_ref, br_ref,
                    xg_ref, meta_ref, cnt_ref, upper_ref):
    tm = x_ref.shape[0]

    @pl.when(pl.program_id(0) == 0)
    def _():
        cnt_ref[...] = jnp.zeros_like(cnt_ref)
        upper_ref[...] = jnp.where(
            lax.broadcasted_iota(jnp.int32, (tm, tm), 0) <= lax.broadcasted_iota(jnp.int32, (tm, tm), 1),
            1.0, 0.0).astype(BF16)

    mix = (jnp.dot(om_ref[...], woa_ref[...], preferred_element_type=F32)
           + jnp.dot(od_ref[...], wob_ref[...], preferred_element_type=F32))
    x1 = _layer_norm(ALPHA * x_ref[...] + mix, g_ref[...], b_ref[...])
    xg_ref[:, :D_MODEL] = x1

    logits = lax.dot_general(wrt_ref[...], x1, (((1,), (1,)), ((), ())),
                             precision=lax.Precision.HIGHEST, preferred_element_type=F32)
    scores = jax.nn.sigmoid(logits)
    biased = scores + br_ref[...]
    member = [biased[k * N_GROUPS:(k + 1) * N_GROUPS, :] for k in range(EXPERTS_PER_GROUP)]
    score_m = [scores[k * N_GROUPS:(k + 1) * N_GROUPS, :] for k in range(EXPERTS_PER_GROUP)]
    first, top1 = _first_max(member)
    rest = [jnp.where(f, -jnp.inf, v) for f, v in zip(first, member)]
    second, top2 = _first_max(rest)
    grp_score = top1 + top2
    grp_rows = [grp_score[g:g + 1, :] for g in range(N_GROUPS)]
    grp_sel, _ = _first_max(grp_rows)
    in_grp_f = jnp.concatenate([jnp.where(s, 1.0, 0.0) for s in grp_sel], axis=0)
    in_grp = in_grp_f > 0.5
    picked = [jnp.logical_and(in_grp, jnp.logical_or(f, s)) for f, s in zip(first, second)]
    w = [jnp.where(p, sc, 0.0) for p, sc in zip(picked, score_m)]
    denom = jnp.sum(functools.reduce(jnp.add, w), axis=0, keepdims=True)
    gates = jnp.concatenate(w, axis=0) / denom

    nrow = cnt_ref.shape[0]
    grp_rows_f = jnp.concatenate([in_grp_f, jnp.zeros((BF16_SUBLANES - N_GROUPS, tm), F32)], axis=0)
    incl = jnp.dot(grp_rows_f.astype(BF16), upper_ref[...], preferred_element_type=F32)[:nrow]
    onehot = grp_rows_f[:nrow]
    rank = jnp.sum(onehot * (incl - 1.0 + cnt_ref[:, 0:1]), axis=0, keepdims=True)
    row = lax.broadcasted_iota(jnp.int32, (nrow, tm), 0)
    gid = jnp.sum(onehot * row.astype(F32), axis=0, keepdims=True)
    cnt_ref[...] = cnt_ref[...] + incl[:, tm - 1:tm]
    meta_ref[...] = jnp.where(row == ROUTE_GROUP_ROW, gid, jnp.where(row == ROUTE_RANK_ROW, rank, 0.0))
    slab_t = jnp.concatenate([gates, jnp.zeros((LANES - N_EXPERTS, tm), F32)], axis=0)
    xg_ref[:, D_MODEL:] = slab_t.T


def _outproj_call(om, od, x, lw, wrt, br, tm):
    t = x.shape[0]
    const = lambda i: (0, 0)
    row = lambda i: (i, 0)
    return pl.pallas_call(
        _outproj_kernel,
        grid=(t // tm,),
        in_specs=[pl.BlockSpec((tm, MLA_SLAB), row),
                  pl.BlockSpec((tm, DIFF_WIDTH), row),
                  pl.BlockSpec((tm, D_MODEL), row),
                  pl.BlockSpec((MLA_SLAB, D_MODEL), const),
                  pl.BlockSpec((DIFF_WIDTH, D_MODEL), const),
                  pl.BlockSpec((1, D_MODEL), const),
                  pl.BlockSpec((1, D_MODEL), const),
                  pl.BlockSpec((N_EXPERTS, D_MODEL), const),
                  pl.BlockSpec((N_EXPERTS, 1), const)],
        out_specs=[pl.BlockSpec((tm, XG_W), row),
                   pl.BlockSpec((8, tm), lambda i: (0, i)),
                   pl.BlockSpec((8, LANES), const)],
        out_shape=[jax.ShapeDtypeStruct((t, XG_W), F32),
                   jax.ShapeDtypeStruct((8, t), F32),
                   jax.ShapeDtypeStruct((8, LANES), F32)],
        scratch_shapes=[pltpu.VMEM((tm, tm), BF16)],
        compiler_params=_cparams("arbitrary"),
        name="outproj",
    )(om, od, x, lw["woa"], lw["wob"], lw["ln1_g"], lw["ln1_b"], wrt, br)


ROW_DMA_UNROLL = 32


def _row_dma_all(row_copy, n):
    def start(r, c):
        row_copy(r).start()
        return c

    def wait(r, c):
        row_copy(r).wait()
        return c

    lax.fori_loop(0, n, start, 0, unroll=ROW_DMA_UNROLL)
    lax.fori_loop(0, n, wait, 0, unroll=ROW_DMA_UNROLL)


def _scatter_kernel(pos_ref, xg_ref, init_ref, xs_ref, sem):
    del init_ref
    tm = xg_ref.shape[0]
    base = pl.program_id(0) * tm
    _row_dma_all(lambda r: pltpu.make_async_copy(xg_ref.at[pl.ds(r, 1), :],
                                                 xs_ref.at[pl.ds(pos_ref[base + r], 1), :], sem), tm)


def _scatter_call(pos, xg, init, tm):
    t = xg.shape[0]
    return pl.pallas_call(
        _scatter_kernel,
        grid_spec=pltpu.PrefetchScalarGridSpec(
            num_scalar_prefetch=1,
            grid=(t // tm,),
            in_specs=[pl.BlockSpec((tm, XG_W), lambda i, pos: (i, 0)),
                      pl.BlockSpec(memory_space=pl.ANY)],
            out_specs=pl.BlockSpec(memory_space=pl.ANY),
            scratch_shapes=[pltpu.SemaphoreType.DMA]),
        out_shape=jax.ShapeDtypeStruct(init.shape, F32),
        input_output_aliases={2: 0},
        compiler_params=_cparams("arbitrary"),
        name="moe_scatter",
    )(pos, xg, init)


def _moe_kernel(tg_ref, nused_ref, xs_ref, wg_ref, wu_ref, wd_ref, ys_ref):
    del tg_ref

    @pl.when(pl.program_id(0) < nused_ref[0])
    def _():
        xb = xs_ref[:, :D_MODEL].astype(BF16)
        lane = lax.broadcasted_iota(jnp.int32, (1, LANES), 1)
        y = None
        for k in range(EXPERTS_PER_GROUP):
            hg = jnp.dot(xb, wg_ref[k], preferred_element_type=F32)
            hu = jnp.dot(xb, wu_ref[k], preferred_element_type=F32)
            mine = jnp.logical_and(lane >= k * N_GROUPS, lane < (k + 1) * N_GROUPS)
            gate = jnp.sum(jnp.where(mine, xs_ref[:, D_MODEL:], 0.0), axis=-1, keepdims=True)
            hh = (hg * jax.nn.sigmoid(hg) * hu * gate).astype(BF16)
            d = jnp.dot(hh, wd_ref[k], preferred_element_type=F32)
            y = d if y is None else y + d
        ys_ref[...] = y

    @pl.when(pl.program_id(0) >= nused_ref[0])
    def _():
        ys_ref[...] = jnp.zeros_like(ys_ref)


def _moe_call(tile_group, nused, xs, layer, moe_w, tm):
    rows = xs.shape[0]
    wmap = lambda i, tg, nu: (layer * N_GROUPS + tg[i], 0, 0, 0)
    wg, wu, wd = moe_w
    return pl.pallas_call(
        _moe_kernel,
        grid_spec=pltpu.PrefetchScalarGridSpec(
            num_scalar_prefetch=2,
            grid=(rows // tm,),
            in_specs=[pl.BlockSpec((tm, XG_W), lambda i, tg, nu: (i, 0)),
                      pl.BlockSpec((None, EXPERTS_PER_GROUP, D_MODEL, D_FF_EXPERT), wmap),
                      pl.BlockSpec((None, EXPERTS_PER_GROUP, D_MODEL, D_FF_EXPERT), wmap),
                      pl.BlockSpec((None, EXPERTS_PER_GROUP, D_FF_EXPERT, D_MODEL), wmap)],
            out_specs=pl.BlockSpec((tm, D_MODEL), lambda i, tg, nu: (i, 0))),
        out_shape=jax.ShapeDtypeStruct((rows, D_MODEL), F32),
        compiler_params=_cparams("parallel"),
        name="moe",
    )(tile_group, nused, xs, wg, wu, wd)


def _unsort_ln_kernel(pos_ref, x1_ref, ys_ref, g_ref, b_ref, o_ref, buf_ref, sem):
    tm = x1_ref.shape[0]
    base = pl.program_id(0) * tm
    _row_dma_all(lambda r: pltpu.make_async_copy(ys_ref.at[pl.ds(pos_ref[base + r], 1), :],
                                                 buf_ref.at[pl.ds(r, 1), :], sem), tm)
    o_ref[...] = _layer_norm(ALPHA * x1_ref[...] + buf_ref[...], g_ref[...], b_ref[...])


def _unsort_ln_call(pos, xg, ys, lw, tm):
    t = xg.shape[0]
    return pl.pallas_call(
        _unsort_ln_kernel,
        grid_spec=pltpu.PrefetchScalarGridSpec(
            num_scalar_prefetch=1,
            grid=(t // tm,),
            in_specs=[pl.BlockSpec((tm, D_MODEL), lambda i, pos: (i, 0)),
                      pl.BlockSpec(memory_space=pl.ANY),
                      pl.BlockSpec((1, D_MODEL), lambda i, pos: (0, 0)),
                      pl.BlockSpec((1, D_MODEL), lambda i, pos: (0, 0))],
            out_specs=pl.BlockSpec((tm, D_MODEL), lambda i, pos: (i, 0)),
            scratch_shapes=[pltpu.VMEM((tm, D_MODEL), F32), pltpu.SemaphoreType.DMA]),
        out_shape=jax.ShapeDtypeStruct((t, D_MODEL), F32),
        compiler_params=_cparams("arbitrary"),
        name="moe_unsort_ln",
    )(pos, xg, ys, lw["ln2_g"], lw["ln2_b"])


def _moe_layer(xg, meta, cnt, lw, layer, moe_w, xs_prev, tm_sorted, tm_tok):
    t = xg.shape[0]
    counts = cnt[:N_GROUPS, 0].astype(jnp.int32)
    padded = (counts + tm_sorted - 1) // tm_sorted * tm_sorted
    ends = jnp.cumsum(padded)
    starts = ends - padded
    grp = meta[ROUTE_GROUP_ROW].astype(jnp.int32)
    rank = meta[ROUTE_RANK_ROW].astype(jnp.int32)
    pos = rank + jnp.sum(jnp.where(grp[:, None] == jnp.arange(N_GROUPS)[None, :], starts[None, :], 0), axis=1)
    ntiles = t // tm_sorted + N_GROUPS
    tile_start = jnp.arange(ntiles, dtype=jnp.int32) * tm_sorted
    tile_group = jnp.minimum(jnp.sum((tile_start[:, None] >= ends[None, :]).astype(jnp.int32), axis=1), N_GROUPS - 1)
    nused = (ends[-1:] // tm_sorted).astype(jnp.int32)
    if xs_prev is None:
        xs_prev = jnp.zeros((ntiles * tm_sorted, XG_W), F32)
    xs = _scatter_call(pos, xg, xs_prev, tm_tok)
    ys = _moe_call(tile_group, nused, xs, layer, moe_w, tm_sorted)
    return _unsort_ln_call(pos, xg, ys, lw, tm_tok), xs


def _rotate_half_cols(w):
    return jnp.concatenate([-w[..., ROPE_HALF:], w[..., :ROPE_HALF]], axis=-1)


def _prep_layer(l, w_in, g_q, g_kv, w_uq, w_ukv, lam_q1, lam_k1, lam_q2, lam_k2, g_sub, w_o,
                ln1_g, ln1_b, ln2_g, ln2_b):
    d = D_MODEL
    z = lambda *s: jnp.zeros(s, F32)
    wi = w_in[l]
    c_kr = Q_LORA_RANK + KV_LORA_RANK
    c_dq = c_kr + MLA_ROPE_DIM
    w_kr = wi[:, c_kr:c_dq]
    pad_l, pad_r = z(d, MLA_NOPE_DIM), z(d, HEAD_W - MLA_NOPE_DIM - MLA_ROPE_DIM)
    w_in_wide = jnp.concatenate([
        wi[:, :c_kr],
        pad_l, w_kr, pad_r,
        pad_l, _rotate_half_cols(w_kr), pad_r,
        wi[:, c_dq:c_dq + 2 * DIFF_WIDTH]], axis=1).astype(BF16)
    wdvt = wi[:, c_dq + 2 * DIFF_WIDTH:].T.astype(BF16)

    r = Q_LORA_RANK
    wq = w_uq[l].reshape(r, MLA_HEADS, MLA_NOPE_DIM + MLA_ROPE_DIM)
    wq_rope = wq[..., MLA_NOPE_DIM:]
    tail = z(r, MLA_HEADS, HEAD_W - MLA_NOPE_DIM - MLA_ROPE_DIM)
    wqa = jnp.concatenate([wq, tail], axis=-1).reshape(r, MLA_SLAB).astype(BF16)
    wqb = jnp.concatenate([z(r, MLA_HEADS, MLA_NOPE_DIM), _rotate_half_cols(wq_rope), tail],
                          axis=-1).reshape(r, MLA_SLAB).astype(BF16)
    r = KV_LORA_RANK
    wkv = w_ukv[l].reshape(r, MLA_HEADS, MLA_NOPE_DIM + MLA_V_DIM)
    wk = jnp.concatenate([wkv[..., :MLA_NOPE_DIM], z(r, MLA_HEADS, HEAD_W - MLA_NOPE_DIM)],
                         axis=-1).reshape(r, MLA_SLAB).astype(BF16)
    wvt = wkv[..., MLA_NOPE_DIM:].reshape(r, MLA_WIDTH).T.astype(BF16)

    wo_m = w_o[l][:MLA_WIDTH].reshape(MLA_HEADS, MLA_V_DIM, d)
    woa = jnp.concatenate([wo_m, z(MLA_HEADS, HEAD_W - MLA_V_DIM, d)], axis=1).reshape(MLA_SLAB, d).astype(BF16)
    wob = w_o[l][MLA_WIDTH:].astype(BF16)

    lamv = jnp.zeros((8, LANES), F32).at[:4, :DIFF_HEAD_DIM].set(
        jnp.stack([lam_q1[l], lam_k1[l], lam_q2[l], lam_k2[l]]).astype(F32))
    return dict(
        w_in=w_in_wide, g_q=g_q[l][None], g_kv=g_kv[l][None], wqa=wqa, wqb=wqb, wk=wk, wvt=wvt, wdvt=wdvt,
        lamv=lamv, g_sub=g_sub[l][None], woa=woa, wob=wob, ln1_g=ln1_g[l][None], ln1_b=ln1_b[l][None],
        ln2_g=ln2_g[l][None], ln2_b=ln2_b[l][None])


def _rope_table(seq):
    inv = 1.0 / (ROPE_THETA ** (jnp.arange(0, MLA_ROPE_DIM, 2, dtype=F32) / MLA_ROPE_DIM))
    ang = jnp.arange(seq, dtype=F32)[:, None] * inv[None, :]
    cos, sin = jnp.cos(ang), jnp.sin(ang)
    scale = LOG2E * (MLA_NOPE_DIM + MLA_ROPE_DIM) ** -0.5
    ones = jnp.ones((seq, MLA_NOPE_DIM), F32)
    zl = jnp.zeros((seq, MLA_NOPE_DIM), F32)
    zr = jnp.zeros((seq, HEAD_W - MLA_NOPE_DIM - MLA_ROPE_DIM), F32)
    ct_q = jnp.concatenate([ones, cos, cos, zr], axis=1) * scale
    st_q = jnp.concatenate([zl, sin, sin, zr], axis=1) * scale
    ct_k = jnp.concatenate([zl, cos, cos, zr], axis=1)
    st_k = jnp.concatenate([zl, sin, sin, zr], axis=1)
    return jnp.concatenate([ct_q, st_q, ct_k, st_k], axis=1)


def _pick(n, pref):
    while n % pref:
        pref //= 2
    return pref


def _trunk(x, ln0_g, ln0_b, layers, moe_w, wrt, br):
    batch, seq, _ = x.shape
    t = batch * seq
    tm_proj = _pick(seq, 512)
    tm_tok = _pick(t, 1024)
    tm_moe = _pick(t, 512)
    tq = _pick(seq, 256)
    tk = _pick(seq // 2, 512)
    nq = _pick(seq // tq, 8)
    nq_diff = _pick(seq // tq, 4 if seq // tk >= 8 else 2)
    tab = _rope_table(seq)
    x = _ln_call(x.reshape(t, D_MODEL), ln0_g[None], ln0_b[None], tm_tok)
    xs = None
    for l, lw in enumerate(layers):
        lam_init = 0.8 - 0.6 * math.exp(-0.3 * l)
        qm, km, vt, d, dvt = _proj_call(x, tab, lw, seq, tm_proj)
        om = _mla_call(qm, km, vt, batch, seq, tq, tk, nq)
        od = _diff_call(d, dvt, lw["lamv"], lw["g_sub"], lam_init, batch, seq, tq, tk, nq_diff)
        xg, meta, cnt = _outproj_call(om, od, x, lw, wrt, br, tm_tok)
        x, xs = _moe_layer(xg, meta, cnt, lw, l, moe_w, xs, tm_moe, tm_tok)
    return x.reshape(batch, seq, D_MODEL)


def kernel(x_prompt, x_sample, ln0_g, ln0_b, w_in, g_q, g_kv, w_uq, w_ukv, lam_q1, lam_k1, lam_q2, lam_k2,
           g_sub, w_o, ln1_g, ln1_b, w_router, b_router, w_gate, w_up, w_down, ln2_g, ln2_b):
    layers = [_prep_layer(l, w_in, g_q, g_kv, w_uq, w_ukv, lam_q1, lam_k1, lam_q2, lam_k2, g_sub, w_o,
                          ln1_g, ln1_b, ln2_g, ln2_b) for l in range(DEPTH)]
    moe_w = tuple(w.astype(BF16).reshape((DEPTH * N_GROUPS, EXPERTS_PER_GROUP) + w.shape[2:])
                  for w in (w_gate, w_up, w_down))
    perm = jnp.arange(N_EXPERTS).reshape(N_GROUPS, EXPERTS_PER_GROUP).T.reshape(-1)
    wrt = w_router.T[perm].astype(F32)
    br = b_router[perm].astype(F32)[:, None]
    y_prompt = _trunk(x_prompt, ln0_g, ln0_b, layers, moe_w, wrt, br)
    y_sample = _trunk(x_sample, ln0_g, ln0_b, layers, moe_w, wrt, br)
    return (y_prompt, y_sample)
```

```python
import functools
import math

import jax
import jax.numpy as jnp
from jax import lax
from jax.experimental import pallas as pl
from jax.experimental.pallas import tpu as pltpu

F32 = jnp.float32
BF16 = jnp.bfloat16

D_MODEL = 1024
DEPTH = 4
MLA_HEADS = 8
MLA_NOPE_DIM = 64
MLA_ROPE_DIM = 32
MLA_V_DIM = 64
Q_LORA_RANK = 256
KV_LORA_RANK = 128
ROPE_THETA = 10000.0
DIFF_HEADS = 4
DIFF_HEAD_DIM = 64
MLA_WIDTH = MLA_HEADS * MLA_V_DIM
DIFF_WIDTH = DIFF_HEADS * 2 * DIFF_HEAD_DIM
N_EXPERTS = 16
N_GROUPS = 4
EXPERTS_PER_GROUP = N_EXPERTS // N_GROUPS
D_FF_EXPERT = 512
LN_EPS = 1e-5
RMS_EPS = 1e-6
ALPHA = (2 * DEPTH) ** 0.25

LANES = 128
HEAD_W = LANES
MLA_SLAB = MLA_HEADS * HEAD_W
ROPE_HALF = MLA_ROPE_DIM // 2
C_Q = 0
C_KV = C_Q + Q_LORA_RANK
C_KR_A = C_KV + KV_LORA_RANK
C_KR_B = C_KR_A + HEAD_W
C_D = C_KR_B + HEAD_W
IN_W = C_D + 2 * DIFF_WIDTH
BF16_SUBLANES = 16
MLA_VT_ROWS = MLA_V_DIM + BF16_SUBLANES
DIFF_VT_ROWS = 2 * DIFF_HEAD_DIM + BF16_SUBLANES
LOG2E = math.log2(math.e)
XG_W = D_MODEL + LANES
ROUTE_GROUP_ROW = 0
ROUTE_RANK_ROW = 1

VMEM_LIMIT_BYTES = 48 * 1024 * 1024


def _cparams(*sem):
    return pltpu.CompilerParams(dimension_semantics=sem, vmem_limit_bytes=VMEM_LIMIT_BYTES)


def _layer_norm(x, g, b):
    mu = jnp.mean(x, axis=-1, keepdims=True)
    xc = x - mu
    var = jnp.mean(xc * xc, axis=-1, keepdims=True)
    return xc * lax.rsqrt(var + LN_EPS) * g + b


def _rms(x, g, eps):
    return x * lax.rsqrt(jnp.mean(x * x, axis=-1, keepdims=True) + eps) * g


def _ln_kernel(x_ref, g_ref, b_ref, o_ref):
    o_ref[...] = _layer_norm(x_ref[...], g_ref[...], b_ref[...])


def _ln_call(x, g, b, tm):
    t = x.shape[0]
    return pl.pallas_call(
        _ln_kernel,
        grid=(t // tm,),
        in_specs=[pl.BlockSpec((tm, D_MODEL), lambda i: (i, 0)),
                  pl.BlockSpec((1, D_MODEL), lambda i: (0, 0)),
                  pl.BlockSpec((1, D_MODEL), lambda i: (0, 0))],
        out_specs=pl.BlockSpec((tm, D_MODEL), lambda i: (i, 0)),
        out_shape=jax.ShapeDtypeStruct((t, D_MODEL), F32),
        compiler_params=_cparams("parallel"),
        name="ln0",
    )(x, g, b)


def _ones_rows(n):
    row = lax.broadcasted_iota(jnp.int32, (BF16_SUBLANES, n), 0)
    return jnp.where(row == 0, 1.0, 0.0).astype(BF16)


def _nt_dot(a, b):
    return lax.dot_general(a, b, (((1,), (1,)), ((), ())), preferred_element_type=F32)


def _proj_kernel(x_ref, tab_ref, win_ref, gq_ref, gkv_ref, wqa_ref, wqb_ref, wk_ref, wvt_ref, wdvt_ref,
                 qm_ref, km_ref, vt_ref, d_ref, dvt_ref):
    tm = x_ref.shape[0]
    xb = x_ref[...].astype(BF16)
    h = jnp.dot(xb, win_ref[...], preferred_element_type=F32)
    ct_q = tab_ref[:, 0 * HEAD_W:1 * HEAD_W]
    st_q = tab_ref[:, 1 * HEAD_W:2 * HEAD_W]
    ct_k = tab_ref[:, 2 * HEAD_W:3 * HEAD_W]
    st_k = tab_ref[:, 3 * HEAD_W:4 * HEAD_W]

    cq = _rms(h[:, C_Q:C_KV], gq_ref[...], RMS_EPS).astype(BF16)
    qa = jnp.dot(cq, wqa_ref[...], preferred_element_type=F32)
    qb = jnp.dot(cq, wqb_ref[...], preferred_element_type=F32)
    ckv = _rms(h[:, C_KV:C_KR_A], gkv_ref[...], RMS_EPS).astype(BF16)
    kn = jnp.dot(ckv, wk_ref[...], preferred_element_type=F32)
    kr = h[:, C_KR_A:C_KR_B] * ct_k + h[:, C_KR_B:C_D] * st_k
    ones = _ones_rows(tm)
    vt = _nt_dot(wvt_ref[...], ckv).astype(BF16)
    for hd in range(MLA_HEADS):
        sl = slice(hd * HEAD_W, (hd + 1) * HEAD_W)
        qm_ref[:, sl] = (qa[:, sl] * ct_q + qb[:, sl] * st_q).astype(BF16)
        km_ref[:, sl] = (kn[:, sl] + kr).astype(BF16)
        r0 = hd * MLA_VT_ROWS
        vt_ref[r0:r0 + MLA_V_DIM, :] = vt[hd * MLA_V_DIM:(hd + 1) * MLA_V_DIM, :]
        vt_ref[r0 + MLA_V_DIM:r0 + MLA_VT_ROWS, :] = ones
    d_ref[:, :DIFF_WIDTH] = (h[:, C_D:C_D + DIFF_WIDTH] * (LOG2E * DIFF_HEAD_DIM ** -0.5)).astype(BF16)
    d_ref[:, DIFF_WIDTH:] = h[:, C_D + DIFF_WIDTH:].astype(BF16)
    dvt = _nt_dot(wdvt_ref[...], xb).astype(BF16)
    for hd in range(DIFF_HEADS):
        r0 = hd * DIFF_VT_ROWS
        dvt_ref[r0:r0 + HEAD_W, :] = dvt[hd * HEAD_W:(hd + 1) * HEAD_W, :]
        dvt_ref[r0 + HEAD_W:r0 + DIFF_VT_ROWS, :] = ones


def _proj_call(x, tab, lw, seq, tm):
    t = x.shape[0]
    batch = t // seq
    nblk = seq // tm
    const = lambda i: (0, 0)
    row = lambda i: (i, 0)
    tcol = lambda i: (i // nblk, 0, i % nblk)
    return pl.pallas_call(
        _proj_kernel,
        grid=(t // tm,),
        in_specs=[pl.BlockSpec((tm, D_MODEL), row),
                  pl.BlockSpec((tm, 4 * HEAD_W), lambda i: (i % nblk, 0)),
                  pl.BlockSpec((D_MODEL, IN_W), const),
                  pl.BlockSpec((1, Q_LORA_RANK), const),
                  pl.BlockSpec((1, KV_LORA_RANK), const),
                  pl.BlockSpec((Q_LORA_RANK, MLA_SLAB), const),
                  pl.BlockSpec((Q_LORA_RANK, MLA_SLAB), const),
                  pl.BlockSpec((KV_LORA_RANK, MLA_SLAB), const),
                  pl.BlockSpec((MLA_WIDTH, KV_LORA_RANK), const),
                  pl.BlockSpec((DIFF_WIDTH, D_MODEL), const)],
        out_specs=[pl.BlockSpec((tm, MLA_SLAB), row),
                   pl.BlockSpec((tm, MLA_SLAB), row),
                   pl.BlockSpec((None, MLA_HEADS * MLA_VT_ROWS, tm), tcol),
                   pl.BlockSpec((tm, 2 * DIFF_WIDTH), row),
                   pl.BlockSpec((None, DIFF_HEADS * DIFF_VT_ROWS, tm), tcol)],
        out_shape=[jax.ShapeDtypeStruct((t, MLA_SLAB), BF16),
                   jax.ShapeDtypeStruct((t, MLA_SLAB), BF16),
                   jax.ShapeDtypeStruct((batch, MLA_HEADS * MLA_VT_ROWS, seq), BF16),
                   jax.ShapeDtypeStruct((t, 2 * DIFF_WIDTH), BF16),
                   jax.ShapeDtypeStruct((batch, DIFF_HEADS * DIFF_VT_ROWS, seq), BF16)],
        compiler_params=_cparams("parallel"),
        name="proj",
    )(x, tab, lw["w_in"], lw["g_q"], lw["g_kv"], lw["wqa"], lw["wqb"], lw["wk"], lw["wvt"], lw["wdvt"])


def _flash_loop(score_fn, vt_ref, s_ref, rows, unroll):
    _, nstream, tk, tq = s_ref.shape
    nk = vt_ref.shape[1] // tk

    def key_offset(j):
        return j * tk if isinstance(j, int) else pl.multiple_of(j * tk, tk)

    def produce_one(fn, slot, i):
        st = fn()
        s_ref[slot, i] = st
        return jnp.max(st, axis=0, keepdims=True)

    def consume_one(vt, slot, i, mx_i, m, acc):
        m_new = jnp.maximum(m, mx_i)
        p = jnp.exp2(s_ref[slot, i] - m_new).astype(BF16)
        return m_new, jnp.exp2(m - m_new) * acc + jnp.dot(vt, p, preferred_element_type=F32)

    def produce(j, slot):
        return tuple(produce_one(fn, slot, i) for i, fn in enumerate(score_fn(key_offset(j))))

    def consume(j, slot, mx, carry):
        vt = vt_ref[:, pl.ds(key_offset(j), tk)]
        return tuple(consume_one(vt, slot, i, mx[i], m, acc) for i, (m, acc) in enumerate(carry))

    def step(jp, slot_p, jc, slot_c, mx_c, carry):
        fns = score_fn(key_offset(jp))
        vt = vt_ref[:, pl.ds(key_offset(jc), tk)]
        mx_p, out = [], []
        for i, (m, acc) in enumerate(carry):
            mx_p.append(produce_one(fns[i], slot_p, i))
            out.append(consume_one(vt, slot_c, i, mx_c[i], m, acc))
        return tuple(mx_p), tuple(out)

    carry = tuple((jnp.full((1, tq), -jnp.inf, F32), jnp.zeros((rows, tq), F32)) for _ in range(nstream))
    mx = produce(0, 0)
    if unroll:
        for j in range(nk - 2):
            mx, carry = step(j + 1, (j + 1) % 2, j, j % 2, mx, carry)
    else:
        def pair(jj, state):
            mx_even, carry = state
            j = 2 * jj
            mx_odd, carry = step(j + 1, 1, j, 0, mx_even, carry)
            return step(j + 2, 0, j + 1, 1, mx_odd, carry)

        mx, carry = lax.fori_loop(0, nk // 2 - 1, pair, (mx, carry))
    mx, carry = step(nk - 1, 1, nk - 2, 0, mx, carry)
    carry = consume(nk - 1, 1, mx, carry)
    return [acc for _, acc in carry]


def _finish(acc, rows):
    return acc[:rows, :] * (1.0 / acc[rows:rows + 1, :])


def _mla_kernel(q_ref, k_ref, vt_ref, o_ref, s_ref):
    _, nq, tk, tq = s_ref.shape
    qs = [q_ref[i * tq:(i + 1) * tq, :] for i in range(nq)]

    def scores(off):
        ks = k_ref[pl.ds(off, tk), :]
        return [functools.partial(_nt_dot, ks, q) for q in qs]

    accs = _flash_loop(scores, vt_ref, s_ref, MLA_VT_ROWS, unroll=True)
    pad = jnp.zeros((HEAD_W - MLA_V_DIM, tq), F32)
    for i, acc in enumerate(accs):
        ot = jnp.concatenate([_finish(acc, MLA_V_DIM), pad], axis=0)
        o_ref[i * tq:(i + 1) * tq, :] = ot.T.astype(BF16)


def _mla_call(qm, km, vt, batch, seq, tq, tk, nq):
    qm, km = (a.reshape(batch, seq, MLA_SLAB) for a in (qm, km))
    bq = tq * nq
    out = pl.pallas_call(
        _mla_kernel,
        grid=(batch, MLA_HEADS, seq // bq),
        scratch_shapes=[pltpu.VMEM((2, nq, tk, tq), F32)],
        in_specs=[pl.BlockSpec((None, bq, HEAD_W), lambda b, h, i: (b, i, h)),
                  pl.BlockSpec((None, seq, HEAD_W), lambda b, h, i: (b, 0, h)),
                  pl.BlockSpec((None, MLA_VT_ROWS, seq), lambda b, h, i: (b, h, 0))],
        out_specs=pl.BlockSpec((None, bq, HEAD_W), lambda b, h, i: (b, i, h)),
        out_shape=jax.ShapeDtypeStruct((batch, seq, MLA_SLAB), BF16),
        compiler_params=_cparams("parallel", "parallel", "parallel"),
        name="mla",
    )(qm, km, vt)
    return out.reshape(batch * seq, MLA_SLAB)


def _diff_kernel(lam_ref, gsub_ref, q_ref, k_ref, vt_ref, o_ref, s_ref, *, lam_init):
    _, _, tk, tq = s_ref.shape
    hd = pl.program_id(1)
    q0 = pl.program_id(2) * q_ref.shape[0]
    slope = LOG2E * jnp.exp2(-2.0 * (hd + 1).astype(F32) * jnp.ones((1, 1), F32))
    lane = lax.broadcasted_iota(jnp.int32, (1, HEAD_W), 1)
    nq = q_ref.shape[0] // tq
    qs = []
    for i in range(nq):
        q = q_ref[i * tq:(i + 1) * tq, :]
        qs.append((jnp.where(lane < DIFF_HEAD_DIM, q, jnp.zeros_like(q)),
                   jnp.where(lane >= DIFF_HEAD_DIM, q, jnp.zeros_like(q))))
    rel = (lax.broadcasted_iota(jnp.int32, (tk, tq), 0) - lax.broadcasted_iota(jnp.int32, (tk, tq), 1)).astype(F32)

    def scores(off):
        ks = k_ref[pl.ds(off, tk), :]
        out = []
        for i, (q1, q2) in enumerate(qs):
            bias = jnp.abs(rel + (off - q0 - i * tq).astype(F32)) * (-slope)
            out += [lambda q=q1, b=bias: _nt_dot(ks, q) + b, lambda q=q2, b=bias: _nt_dot(ks, q) + b]
        return out

    accs = _flash_loop(scores, vt_ref, s_ref, DIFF_VT_ROWS, unroll=False)

    lam = (jnp.exp(jnp.sum(lam_ref[0:1, :] * lam_ref[1:2, :], axis=-1, keepdims=True))
           - jnp.exp(jnp.sum(lam_ref[2:3, :] * lam_ref[3:4, :], axis=-1, keepdims=True)) + lam_init)
    for i in range(nq):
        o = (_finish(accs[2 * i], HEAD_W) - lam * _finish(accs[2 * i + 1], HEAD_W)).T
        o_ref[i * tq:(i + 1) * tq, :] = (_rms(o, gsub_ref[...], LN_EPS) * (1.0 - lam_init)).astype(BF16)


def _diff_call(d, dvt, lamv, g_sub, lam_init, batch, seq, tq, tk, nq):
    d = d.reshape(batch, seq, 2 * DIFF_WIDTH)
    bq = tq * nq
    out = pl.pallas_call(
        functools.partial(_diff_kernel, lam_init=lam_init),
        grid=(batch, DIFF_HEADS, seq // bq),
        scratch_shapes=[pltpu.VMEM((2, 2 * nq, tk, tq), F32)],
        in_specs=[pl.BlockSpec((8, LANES), lambda b, h, i: (0, 0)),
                  pl.BlockSpec((1, HEAD_W), lambda b, h, i: (0, 0)),
                  pl.BlockSpec((None, bq, HEAD_W), lambda b, h, i: (b, i, h)),
                  pl.BlockSpec((None, seq, HEAD_W), lambda b, h, i: (b, 0, DIFF_HEADS + h)),
                  pl.BlockSpec((None, DIFF_VT_ROWS, seq), lambda b, h, i: (b, h, 0))],
        out_specs=pl.BlockSpec((None, bq, HEAD_W), lambda b, h, i: (b, i, h)),
        out_shape=jax.ShapeDtypeStruct((batch, seq, DIFF_WIDTH), BF16),
        compiler_params=_cparams("parallel", "parallel", "parallel"),
        name="diff",
    )(lamv, g_sub, d, d, dvt)
    return out.reshape(batch * seq, DIFF_WIDTH)


def _first_max(vals):
    mx = functools.reduce(jnp.maximum, vals)
    taken = None
    masks = []
    for v in vals:
        hit = v == mx
        if taken is None:
            masks.append(hit)
            taken = hit
        else:
            masks.append(jnp.logical_and(hit, jnp.logical_not(taken)))
            taken = jnp.logical_or(taken, hit)
    return masks, mx


def _outproj_kernel(om_ref, od_ref, x_ref, woa_ref, wob_ref, g_ref, b_ref, wrt_ref, br_ref,
                    xg_ref, meta_ref, cnt_ref, upper_ref):
    tm = x_ref.shape[0]

    @pl.when(pl.program_id(0) == 0)
    def _():
        cnt_ref[...] = jnp.zeros_like(cnt_ref)
        upper_ref[...] = jnp.where(
            lax.broadcasted_iota(jnp.int32, (tm, tm), 0) <= lax.broadcasted_iota(jnp.int32, (tm, tm), 1),
            1.0, 0.0).astype(BF16)

    mix = (jnp.dot(om_ref[...], woa_ref[...], preferred_element_type=F32)
           + jnp.dot(od_ref[...], wob_ref[...], preferred_element_type=F32))
    x1 = _layer_norm(ALPHA * x_ref[...] + mix, g_ref[...], b_ref[...])
    xg_ref[:, :D_MODEL] = x1

    logits = lax.dot_general(wrt_ref[...], x1, (((1,), (1,)), ((), ())),
                             precision=lax.Precision.HIGHEST, preferred_element_type=F32)
    scores = jax.nn.sigmoid(logits)
    biased = scores + br_ref[...]
    member = [biased[k * N_GROUPS:(k + 1) * N_GROUPS, :] for k in range(EXPERTS_PER_GROUP)]
    score_m = [scores[k * N_GROUPS:(k + 1) * N_GROUPS, :] for k in range(EXPERTS_PER_GROUP)]
    first, top1 = _first_max(member)
    rest = [jnp.where(f, -jnp.inf, v) for f, v in zip(first, member)]
    second, top2 = _first_max(rest)
    grp_score = top1 + top2
    grp_rows = [grp_score[g:g + 1, :] for g in range(N_GROUPS)]
    grp_sel, _ = _first_max(grp_rows)
    in_grp_f = jnp.concatenate([jnp.where(s, 1.0, 0.0) for s in grp_sel], axis=0)
    in_grp = in_grp_f > 0.5
    picked = [jnp.logical_and(in_grp, jnp.logical_or(f, s)) for f, s in zip(first, second)]
    w = [jnp.where(p, sc, 0.0) for p, sc in zip(picked, score_m)]
    denom = jnp.sum(functools.reduce(jnp.add, w), axis=0, keepdims=True)
    gates = jnp.concatenate(w, axis=0) / denom

    nrow = cnt_ref.shape[0]
    grp_rows_f = jnp.concatenate([in_grp_f, jnp.zeros((BF16_SUBLANES - N_GROUPS, tm), F32)], axis=0)
    incl = jnp.dot(grp_rows_f.astype(BF16), upper_ref[...], preferred_element_type=F32)[:nrow]
    onehot = grp_rows_f[:nrow]
    rank = jnp.sum(onehot * (incl - 1.0 + cnt_ref[:, 0:1]), axis=0, keepdims=True)
    row = lax.broadcasted_iota(jnp.int32, (nrow, tm), 0)
    gid = jnp.sum(onehot * row.astype(F32), axis=0, keepdims=True)
    cnt_ref[...] = cnt_ref[...] + incl[:, tm - 1:tm]
    meta_ref[...] = jnp.where(row == ROUTE_GROUP_ROW, gid, jnp.where(row == ROUTE_RANK_ROW, rank, 0.0))
    slab_t = jnp.concatenate([gates, jnp.zeros((LANES - N_EXPERTS, tm), F32)], axis=0)
    xg_ref[:, D_MODEL:] = slab_t.T


def _outproj_call(om, od, x, lw, wrt, br, tm):
    t = x.shape[0]
    const = lambda i: (0, 0)
    row = lambda i: (i, 0)
    return pl.pallas_call(
        _outproj_kernel,
        grid=(t // tm,),
        in_specs=[pl.BlockSpec((tm, MLA_SLAB), row),
                  pl.BlockSpec((tm, DIFF_WIDTH), row),
                  pl.BlockSpec((tm, D_MODEL), row),
                  pl.BlockSpec((MLA_SLAB, D_MODEL), const),
                  pl.BlockSpec((DIFF_WIDTH, D_MODEL), const),
                  pl.BlockSpec((1, D_MODEL), const),
                  pl.BlockSpec((1, D_MODEL), const),
                  pl.BlockSpec((N_EXPERTS, D_MODEL), const),
                  pl.BlockSpec((N_EXPERTS, 1), const)],
        out_specs=[pl.BlockSpec((tm, XG_W), row),
                   pl.BlockSpec((8, tm), lambda i: (0, i)),
                   pl.BlockSpec((8, LANES), const)],
        out_shape=[jax.ShapeDtypeStruct((t, XG_W), F32),
                   jax.ShapeDtypeStruct((8, t), F32),
                   jax.ShapeDtypeStruct((8, LANES), F32)],
        scratch_shapes=[pltpu.VMEM((tm, tm), BF16)],
        compiler_params=_cparams("arbitrary"),
        name="outproj",
    )(om, od, x, lw["woa"], lw["wob"], lw["ln1_g"], lw["ln1_b"], wrt, br)


ROW_DMA_UNROLL = 32


def _row_dma_all(row_copy, n):
    def start(r, c):
        row_copy(r).start()
        return c

    def wait(r, c):
        row_copy(r).wait()
        return c

    lax.fori_loop(0, n, start, 0, unroll=ROW_DMA_UNROLL)
    lax.fori_loop(0, n, wait, 0, unroll=ROW_DMA_UNROLL)


def _scatter_kernel(pos_ref, xg_ref, init_ref, xs_ref, sem):
    del init_ref
    tm = xg_ref.shape[0]
    base = pl.program_id(0) * tm
    _row_dma_all(lambda r: pltpu.make_async_copy(xg_ref.at[pl.ds(r, 1), :],
                                                 xs_ref.at[pl.ds(pos_ref[base + r], 1), :], sem), tm)


def _scatter_call(pos, xg, init, tm):
    t = xg.shape[0]
    return pl.pallas_call(
        _scatter_kernel,
        grid_spec=pltpu.PrefetchScalarGridSpec(
            num_scalar_prefetch=1,
            grid=(t // tm,),
            in_specs=[pl.BlockSpec((tm, XG_W), lambda i, pos: (i, 0)),
                      pl.BlockSpec(memory_space=pl.ANY)],
            out_specs=pl.BlockSpec(memory_space=pl.ANY),
            scratch_shapes=[pltpu.SemaphoreType.DMA]),
        out_shape=jax.ShapeDtypeStruct(init.shape, F32),
        input_output_aliases={2: 0},
        compiler_params=_cparams("arbitrary"),
        name="moe_scatter",
    )(pos, xg, init)


def _moe_kernel(tg_ref, nused_ref, xs_ref, wg_ref, wu_ref, wd_ref, ys_ref):
    del tg_ref

    @pl.when(pl.program_id(0) < nused_ref[0])
    def _():
        xb = xs_ref[:, :D_MODEL].astype(BF16)
        lane = lax.broadcasted_iota(jnp.int32, (1, LANES), 1)
        y = None
        for k in range(EXPERTS_PER_GROUP):
            hg = jnp.dot(xb, wg_ref[k], preferred_element_type=F32)
            hu = jnp.dot(xb, wu_ref[k], preferred_element_type=F32)
            mine = jnp.logical_and(lane >= k * N_GROUPS, lane < (k + 1) * N_GROUPS)
            gate = jnp.sum(jnp.where(mine, xs_ref[:, D_MODEL:], 0.0), axis=-1, keepdims=True)
            hh = (hg * jax.nn.sigmoid(hg) * hu * gate).astype(BF16)
            d = jnp.dot(hh, wd_ref[k], preferred_element_type=F32)
            y = d if y is None else y + d
        ys_ref[...] = y

    @pl.when(pl.program_id(0) >= nused_ref[0])
    def _():
        ys_ref[...] = jnp.zeros_like(ys_ref)


def _moe_call(tile_group, nused, xs, layer, moe_w, tm):
    rows = xs.shape[0]
    wmap = lambda i, tg, nu: (layer * N_GROUPS + tg[i], 0, 0, 0)
    wg, wu, wd = moe_w
    return pl.pallas_call(
        _moe_kernel,
        grid_spec=pltpu.PrefetchScalarGridSpec(
            num_scalar_prefetch=2,
            grid=(rows // tm,),
            in_specs=[pl.BlockSpec((tm, XG_W), lambda i, tg, nu: (i, 0)),
                      pl.BlockSpec((None, EXPERTS_PER_GROUP, D_MODEL, D_FF_EXPERT), wmap),
                      pl.BlockSpec((None, EXPERTS_PER_GROUP, D_MODEL, D_FF_EXPERT), wmap),
                      pl.BlockSpec((None, EXPERTS_PER_GROUP, D_FF_EXPERT, D_MODEL), wmap)],
            out_specs=pl.BlockSpec((tm, D_MODEL), lambda i, tg, nu: (i, 0))),
        out_shape=jax.ShapeDtypeStruct((rows, D_MODEL), F32),
        compiler_params=_cparams("parallel"),
        name="moe",
    )(tile_group, nused, xs, wg, wu, wd)


def _unsort_ln_kernel(pos_ref, x1_ref, ys_ref, g_ref, b_ref, o_ref, buf_ref, sem):
    tm = x1_ref.shape[0]
    base = pl.program_id(0) * tm
    _row_dma_all(lambda r: pltpu.make_async_copy(ys_ref.at[pl.ds(pos_ref[base + r], 1), :],
                                                 buf_ref.at[pl.ds(r, 1), :], sem), tm)
    o_ref[...] = _layer_norm(ALPHA * x1_ref[...] + buf_ref[...], g_ref[...], b_ref[...])


def _unsort_ln_call(pos, xg, ys, lw, tm):
    t = xg.shape[0]
    return pl.pallas_call(
        _unsort_ln_kernel,
        grid_spec=pltpu.PrefetchScalarGridSpec(
            num_scalar_prefetch=1,
            grid=(t // tm,),
            in_specs=[pl.BlockSpec((tm, D_MODEL), lambda i, pos: (i, 0)),
                      pl.BlockSpec(memory_space=pl.ANY),
                      pl.BlockSpec((1, D_MODEL), lambda i, pos: (0, 0)),
                      pl.BlockSpec((1, D_MODEL), lambda i, pos: (0, 0))],
            out_specs=pl.BlockSpec((tm, D_MODEL), lambda i, pos: (i, 0)),
            scratch_shapes=[pltpu.VMEM((tm, D_MODEL), F32), pltpu.SemaphoreType.DMA]),
        out_shape=jax.ShapeDtypeStruct((t, D_MODEL), F32),
        compiler_params=_cparams("arbitrary"),
        name="moe_unsort_ln",
    )(pos, xg, ys, lw["ln2_g"], lw["ln2_b"])


def _moe_layer(xg, meta, cnt, lw, layer, moe_w, xs_prev, tm_sorted, tm_tok):
    t = xg.shape[0]
    counts = cnt[:N_GROUPS, 0].astype(jnp.int32)
    padded = (counts + tm_sorted - 1) // tm_sorted * tm_sorted
    ends = jnp.cumsum(padded)
    starts = ends - padded
    grp = meta[ROUTE_GROUP_ROW].astype(jnp.int32)
    rank = meta[ROUTE_RANK_ROW].astype(jnp.int32)
    pos = rank + jnp.sum(jnp.where(grp[:, None] == jnp.arange(N_GROUPS)[None, :], starts[None, :], 0), axis=1)
    ntiles = t // tm_sorted + N_GROUPS
    tile_start = jnp.arange(ntiles, dtype=jnp.int32) * tm_sorted
    tile_group = jnp.minimum(jnp.sum((tile_start[:, None] >= ends[None, :]).astype(jnp.int32), axis=1), N_GROUPS - 1)
    nused = (ends[-1:] // tm_sorted).astype(jnp.int32)
    if xs_prev is None:
        xs_prev = jnp.zeros((ntiles * tm_sorted, XG_W), F32)
    xs = _scatter_call(pos, xg, xs_prev, tm_tok)
    ys = _moe_call(tile_group, nused, xs, layer, moe_w, tm_sorted)
    return _unsort_ln_call(pos, xg, ys, lw, tm_tok), xs


def _rotate_half_cols(w):
    return jnp.concatenate([-w[..., ROPE_HALF:], w[..., :ROPE_HALF]], axis=-1)


def _prep_layer(l, w_in, g_q, g_kv, w_uq, w_ukv, lam_q1, lam_k1, lam_q2, lam_k2, g_sub, w_o,
                ln1_g, ln1_b, ln2_g, ln2_b):
    d = D_MODEL
    z = lambda *s: jnp.zeros(s, F32)
    wi = w_in[l]
    c_kr = Q_LORA_RANK + KV_LORA_RANK
    c_dq = c_kr + MLA_ROPE_DIM
    w_kr = wi[:, c_kr:c_dq]
    pad_l, pad_r = z(d, MLA_NOPE_DIM), z(d, HEAD_W - MLA_NOPE_DIM - MLA_ROPE_DIM)
    w_in_wide = jnp.concatenate([
        wi[:, :c_kr],
        pad_l, w_kr, pad_r,
        pad_l, _rotate_half_cols(w_kr), pad_r,
        wi[:, c_dq:c_dq + 2 * DIFF_WIDTH]], axis=1).astype(BF16)
    wdvt = wi[:, c_dq + 2 * DIFF_WIDTH:].T.astype(BF16)

    r = Q_LORA_RANK
    wq = w_uq[l].reshape(r, MLA_HEADS, MLA_NOPE_DIM + MLA_ROPE_DIM)
    wq_rope = wq[..., MLA_NOPE_DIM:]
    tail = z(r, MLA_HEADS, HEAD_W - MLA_NOPE_DIM - MLA_ROPE_DIM)
    wqa = jnp.concatenate([wq, tail], axis=-1).reshape(r, MLA_SLAB).astype(BF16)
    wqb = jnp.concatenate([z(r, MLA_HEADS, MLA_NOPE_DIM), _rotate_half_cols(wq_rope), tail],
                          axis=-1).reshape(r, MLA_SLAB).astype(BF16)
    r = KV_LORA_RANK
    wkv = w_ukv[l].reshape(r, MLA_HEADS, MLA_NOPE_DIM + MLA_V_DIM)
    wk = jnp.concatenate([wkv[..., :MLA_NOPE_DIM], z(r, MLA_HEADS, HEAD_W - MLA_NOPE_DIM)],
                         axis=-1).reshape(r, MLA_SLAB).astype(BF16)
    wvt = wkv[..., MLA_NOPE_DIM:].reshape(r, MLA_WIDTH).T.astype(BF16)

    wo_m = w_o[l][:MLA_WIDTH].reshape(MLA_HEADS, MLA_V_DIM, d)
    woa = jnp.concatenate([wo_m, z(MLA_HEADS, HEAD_W - MLA_V_DIM, d)], axis=1).reshape(MLA_SLAB, d).astype(BF16)
    wob = w_o[l][MLA_WIDTH:].astype(BF16)

    lamv = jnp.zeros((8, LANES), F32).at[:4, :DIFF_HEAD_DIM].set(
        jnp.stack([lam_q1[l], lam_k1[l], lam_q2[l], lam_k2[l]]).astype(F32))
    return dict(
        w_in=w_in_wide, g_q=g_q[l][None], g_kv=g_kv[l][None], wqa=wqa, wqb=wqb, wk=wk, wvt=wvt, wdvt=wdvt,
        lamv=lamv, g_sub=g_sub[l][None], woa=woa, wob=wob, ln1_g=ln1_g[l][None], ln1_b=ln1_b[l][None],
        ln2_g=ln2_g[l][None], ln2_b=ln2_b[l][None])


def _rope_table(seq):
    inv = 1.0 / (ROPE_THETA ** (jnp.arange(0, MLA_ROPE_DIM, 2, dtype=F32) / MLA_ROPE_DIM))
    ang = jnp.arange(seq, dtype=F32)[:, None] * inv[None, :]
    cos, sin = jnp.cos(ang), jnp.sin(ang)
    scale = LOG2E * (MLA_NOPE_DIM + MLA_ROPE_DIM) ** -0.5
    ones = jnp.ones((seq, MLA_NOPE_DIM), F32)
    zl = jnp.zeros((seq, MLA_NOPE_DIM), F32)
    zr = jnp.zeros((seq, HEAD_W - MLA_NOPE_DIM - MLA_ROPE_DIM), F32)
    ct_q = jnp.concatenate([ones, cos, cos, zr], axis=1) * scale
    st_q = jnp.concatenate([zl, sin, sin, zr], axis=1) * scale
    ct_k = jnp.concatenate([zl, cos, cos, zr], axis=1)
    st_k = jnp.concatenate([zl, sin, sin, zr], axis=1)
    return jnp.concatenate([ct_q, st_q, ct_k, st_k], axis=1)


def _pick(n, pref):
    while n % pref:
        pref //= 2
    return pref


def _trunk(x, ln0_g, ln0_b, layers, moe_w, wrt, br):
    batch, seq, _ = x.shape
    t = batch * seq
    tm_proj = _pick(seq, 512)
    tm_tok = _pick(t, 1024)
    tm_moe = _pick(t, 512)
    tq = _pick(seq, 256)
    tk = _pick(seq // 2, 512)
    nq = _pick(seq // tq, 8)
    nq_diff = _pick(seq // tq, 8 if seq // tk >= 8 else 2)
    tab = _rope_table(seq)
    x = _ln_call(x.reshape(t, D_MODEL), ln0_g[None], ln0_b[None], tm_tok)
    xs = None
    for l, lw in enumerate(layers):
        lam_init = 0.8 - 0.6 * math.exp(-0.3 * l)
        qm, km, vt, d, dvt = _proj_call(x, tab, lw, seq, tm_proj)
        om = _mla_call(qm, km, vt, batch, seq, tq, tk, nq)
        od = _diff_call(d, dvt, lw["lamv"], lw["g_sub"], lam_init, batch, seq, tq, tk, nq_diff)
        xg, meta, cnt = _outproj_call(om, od, x, lw, wrt, br, tm_tok)
        x, xs = _moe_layer(xg, meta, cnt, lw, l, moe_w, xs, tm_moe, tm_tok)
    return x.reshape(batch, seq, D_MODEL)


def kernel(x_prompt, x_sample, ln0_g, ln0_b, w_in, g_q, g_kv, w_uq, w_ukv, lam_q1, lam_k1, lam_q2, lam_k2,
           g_sub, w_o, ln1_g, ln1_b, w_router, b_router, w_gate, w_up, w_down, ln2_g, ln2_b):
    layers = [_prep_layer(l, w_in, g_q, g_kv, w_uq, w_ukv, lam_q1, lam_k1, lam_q2, lam_k2, g_sub, w_o,
                          ln1_g, ln1_b, ln2_g, ln2_b) for l in range(DEPTH)]
    moe_w = tuple(w.astype(BF16).reshape((DEPTH * N_GROUPS, EXPERTS_PER_GROUP) + w.shape[2:])
                  for w in (w_gate, w_up, w_down))
    perm = jnp.arange(N_EXPERTS).reshape(N_GROUPS, EXPERTS_PER_GROUP).T.reshape(-1)
    wrt = w_router.T[perm].astype(F32)
    br = b_router[perm].astype(F32)[:, None]
    y_prompt = _trunk(x_prompt, ln0_g, ln0_b, layers, moe_w, wrt, br)
    y_sample = _trunk(x_sample, ln0_g, ln0_b, layers, moe_w, wrt, br)
    return (y_prompt, y_sample)
```
